```python
import jax
import jax.numpy as jnp
from jax import lax
import numpy as np

D_MODEL = 1024
BATCH = 2
SEQ = 8192
DEPTH = 1

GRID_W = 64
CTX_LEN = 256
A_HEAD_DIM = 64
A_HEADS = D_MODEL // A_HEAD_DIM
D_A = A_HEADS * A_HEAD_DIM
LORA_DECAY = 64
LORA_AAA = 64
LORA_GATE = 160
GN_EPS = 64e-5
B_BLOCK_DIM = 64
B_BLOCKS = D_MODEL // B_BLOCK_DIM
D_B = B_BLOCKS * B_BLOCK_DIM
CONV_W = 5
LRU_C = 8.0
N_GROUPS = 4
EXPERTS_PER_GROUP = 8
N_EXPERTS = N_GROUPS * EXPERTS_PER_GROUP
TOP_K = 2
D_EXPERT = 512
MOE_BLOCK = 128
A_SLAB = 3 * D_A + LORA_DECAY + LORA_AAA + LORA_GATE
B_SLAB = 2 * D_B
D_IN = A_SLAB + B_SLAB + 2 * D_MODEL
RWKV_SPLITS = (D_A, 2 * D_A, 3 * D_A, 3 * D_A + LORA_DECAY, 3 * D_A + LORA_DECAY + LORA_AAA)
LN_EPS = 1e-5
ALPHA = (2 * DEPTH) ** 0.25
BETA = (8 * DEPTH) ** -0.25

kernel_name = 'hybrid_rwkv7_rglru_hmoe_flow_block'


def layer_norm(x, g=None, b=None):
    xf = x.astype(jnp.float32)
    mu = jnp.mean(xf, -1, keepdims=True)
    var = jnp.mean(jnp.square(xf - mu), -1, keepdims=True)
    y = ((xf - mu) * lax.rsqrt(var + LN_EPS)).astype(x.dtype)
    if g is not None:
        y = y * g + b
    return y


def flip(t):
    return t[:, ::-1]


def grid_shift(z):
    b_, l_, ch = z.shape
    rows = l_ // GRID_W
    zg = z.reshape(b_, rows, GRID_W, ch // 4, 4)
    left = jnp.pad(zg[:, :, :-1, :, 0], ((0, 0), (0, 0), (1, 0), (0, 0)))
    right = jnp.pad(zg[:, :, 1:, :, 1], ((0, 0), (0, 0), (0, 1), (0, 0)))
    up = jnp.pad(zg[:, :-1, :, :, 2], ((0, 0), (1, 0), (0, 0), (0, 0)))
    down = jnp.pad(zg[:, 1:, :, :, 3], ((0, 0), (0, 1), (0, 0), (0, 0)))
    return jnp.stack([left, right, up, down], -1).reshape(b_, l_, ch)


def seq_shift(z):
    b_, l_, ch = z.shape
    zs = z.reshape(b_, l_, ch // 2, 2)
    prev = jnp.pad(zs[:, :-1, :, 0], ((0, 0), (1, 0), (0, 0)))
    nxt = jnp.pad(zs[:, 1:, :, 1], ((0, 0), (0, 1), (0, 0)))
    return jnp.stack([prev, nxt], -1).reshape(b_, l_, ch)


def depthwise_conv(x, w, b):
    y = lax.conv_general_dilated(x, w[:, None, :], window_strides=(1,),
                                 padding=[(CONV_W // 2, CONV_W // 2)],
                                 dimension_numbers=('NWC', 'WIO', 'NWC'),
                                 feature_group_count=x.shape[-1])
    return y + b


def wkv7_scan(r, w, k, v, kk, a, s0):
    xs = tuple(jnp.moveaxis(t, 1, 0) for t in (r, w, k, v, kk, a))

    def step(s, inp):
        r_t, w_t, k_t, v_t, kk_t, a_t = inp
        s_kk = jnp.einsum('bhvk,bhk->bhv', s, kk_t)
        s = (s * w_t[:, :, None, :] - s_kk[..., None] * (kk_t * a_t)[:, :, None, :]
             + v_t[..., None] * k_t[:, :, None, :])
        return s, jnp.einsum('bhvk,bhk->bhv', s, r_t)

    s_final, ys = lax.scan(step, s0, xs)
    return jnp.moveaxis(ys, 0, 1), s_final


def rwkv_mix(za, shift_fn, mu_a, w0, w2, a0, a2, g2, k_k, k_a, r_k, gn_g, gn_b, s0_f, s0_b):
    b_, l_, _ = za.shape
    za = za + mu_a * (shift_fn(za) - za)
    r, k, v, wd, ad, gd = jnp.split(za, RWKV_SPLITS, axis=-1)
    g = jax.nn.sigmoid(gd) @ g2
    w_log = -jax.nn.softplus(-(w0[:, None, None, :] + jnp.einsum('blr,erd->ebld', jnp.tanh(wd), w2))) - 0.5
    decay = jnp.exp(-jnp.exp(w_log))
    a = jax.nn.sigmoid(a0[:, None, None, :] + jnp.einsum('blr,erd->ebld', ad, a2))
    k_mod = k * (1.0 + (a - 1.0) * k_a)

    def heads(t):
        return t.reshape(t.shape[:-1] + (A_HEADS, A_HEAD_DIM))

    kk32 = heads(k * k_k).astype(jnp.float32)
    kk = (kk32 * lax.rsqrt(jnp.sum(jnp.square(kk32), -1, keepdims=True) + 1e-12)).astype(k.dtype)
    r_h, v_h = heads(r), heads(v)
    decay_h, a_h, k_h = heads(decay), heads(a), heads(k_mod)
    y_f, s_f = wkv7_scan(r_h, decay_h[0], k_h[0], v_h, kk, a_h[0], s0_f)
    y_b, s_b = wkv7_scan(flip(r_h), flip(decay_h[1]), flip(k_h[1]), flip(v_h), flip(kk), flip(a_h[1]), s0_b)
    y = y_f + flip(y_b)
    y32 = y.astype(jnp.float32)
    mu = jnp.mean(y32, -1, keepdims=True)
    var = jnp.mean(jnp.square(y32 - mu), -1, keepdims=True)
    yn = ((y32 - mu) * lax.rsqrt(var + GN_EPS)).astype(y.dtype).reshape(b_, l_, D_A) * gn_g + gn_b
    bonus = jnp.sum(r_h * (k_h[0] + k_h[1]) * r_k, -1, keepdims=True) * v_h
    return (yn + bonus.reshape(b_, l_, D_A)) * g, s_f, s_b


def linear_scan(a, u, h0):
    def combine(e1, e2):
        a1, u1 = e1
        a2_, u2 = e2
        return a1 * a2_, a2_ * u1 + u2

    a_cum, h = lax.associative_scan(combine, (a, u), axis=1)
    return h + a_cum * h0[:, None, :]


def rglru_mix(zb, conv_w, conv_b, wa, ba, wx, bx, lam, h0_f, h0_b):
    b_, l_, _ = zb.shape
    xb, gb = jnp.split(zb, 2, axis=-1)
    xb = depthwise_conv(xb, conv_w, conv_b)
    gate = jax.nn.gelu(gb)
    xh = xb.reshape(b_, l_, B_BLOCKS, B_BLOCK_DIM)
    rg = jax.nn.sigmoid(jnp.einsum('blhi,ehij->eblhj', xh, wa).reshape(2, b_, l_, D_B) + ba[:, None, None, :])
    ig = jax.nn.sigmoid(jnp.einsum('blhi,ehij->eblhj', xh, wx).reshape(2, b_, l_, D_B) + bx[:, None, None, :])
    log_a = -LRU_C * rg * jax.nn.softplus(-lam)[:, None, None, :]
    a = jnp.exp(log_a)
    u = jnp.sqrt(-jnp.expm1(2.0 * log_a)) * ig * xb
    h_f = linear_scan(a[0], u[0], h0_f)
    h_b = flip(linear_scan(flip(a[1]), flip(u[1]), h0_b))
    return (h_f + h_b) * gate, h_f[:, -1], h_b[:, 0]


def token_mix(h, shift_fn, w_in, p_a, p_b, w_o, rw, lru, states):
    slab = h @ w_in
    za = slab[..., :A_SLAB]
    zb = slab[..., A_SLAB:A_SLAB + B_SLAB]
    ga, gb = jnp.split(slab[..., A_SLAB + B_SLAB:], 2, axis=-1)
    s_f, s_b, h_f, h_b = states
    y_a, s_f, s_b = rwkv_mix(za, shift_fn, *rw, s_f, s_b)
    y_b, h_f, h_b = rglru_mix(zb, *lru, h_f, h_b)
    m = jax.nn.sigmoid(ga) * (y_a @ p_a) + jax.nn.sigmoid(gb) * (y_b @ p_b)
    return m @ w_o, (s_f, s_b, h_f, h_b)


def grouped_experts(xt, eid, wts, w1, w3, w2):
    t_, d_ = xt.shape
    n_assign = eid.shape[0]
    tok = jnp.arange(n_assign) // TOP_K
    order = jnp.argsort(eid)
    e_sorted = eid[order]
    counts = jnp.bincount(eid, length=N_EXPERTS)
    padded = (counts + MOE_BLOCK - 1) // MOE_BLOCK * MOE_BLOCK
    pad_end = jnp.cumsum(padded)
    pad_start = pad_end - padded
    start = jnp.cumsum(counts) - counts
    dest = pad_start[e_sorted] + jnp.arange(n_assign) - start[e_sorted]
    n_blocks = -(-n_assign // MOE_BLOCK) + N_EXPERTS
    n_pad = n_blocks * MOE_BLOCK
    buf_tok = jnp.full((n_pad,), t_, jnp.int32).at[dest].set(tok[order].astype(jnp.int32))
    buf_w = jnp.zeros((n_pad,), xt.dtype).at[dest].set(wts[order].astype(xt.dtype))
    blk_e = jnp.minimum(jnp.searchsorted(pad_end, jnp.arange(n_blocks) * MOE_BLOCK, side='right'), N_EXPERTS - 1)
    x_pad = jnp.concatenate([xt, jnp.zeros((1, d_), xt.dtype)], 0)
    xs = x_pad[buf_tok].reshape(n_blocks, MOE_BLOCK, d_)

    def expert_block(args):
        xb, e = args
        return (jax.nn.silu(xb @ w1[e]) * (xb @ w3[e])) @ w2[e]

    ys = lax.map(expert_block, (xs, blk_e)).reshape(n_pad, d_) * buf_w[:, None]
    return jnp.zeros((t_ + 1, d_), ys.dtype).at[buf_tok].add(ys)[:t_]


def hier_moe(h, rg_w, rg_b, re_w, re_b, w1, w3, w2):
    b_, l_, d_ = h.shape
    xt = h.reshape(b_ * l_, d_)
    glog = (xt @ rg_w + rg_b).astype(jnp.float32)
    gsel = jnp.argmax(glog, -1)
    p_g = jnp.take_along_axis(jax.nn.softmax(glog, -1), gsel[:, None], -1)
    elog = (xt @ re_w + re_b).astype(jnp.float32).reshape(-1, N_GROUPS, EXPERTS_PER_GROUP)
    elog_g = jnp.take_along_axis(elog, gsel[:, None, None], 1)[:, 0]
    top_val, top_idx = lax.top_k(elog_g, TOP_K)
    wts = jax.nn.softmax(top_val, -1) * p_g
    eid = (gsel[:, None] * EXPERTS_PER_GROUP + top_idx).reshape(-1)
    y = grouped_experts(xt, eid, wts.reshape(-1), w1, w3, w2)
    return y.reshape(b_, l_, d_)


def setup_inputs(seed: int = 0) -> dict:
    key = jax.random.key(seed)
    ks = iter(jax.random.split(key, 48))

    def nrm(shape, scale):
        return scale * jax.random.normal(next(ks), shape, jnp.float32)

    L = DEPTH
    lam_u = jax.random.uniform(next(ks), (L, 2, D_B), jnp.float32, 0.9, 0.999)
    s = lam_u ** (1.0 / LRU_C)
    return {
        'x': nrm((BATCH, SEQ, D_MODEL), 1.0),
        'c': nrm((BATCH, D_MODEL), 1.0),
        'ctx': nrm((BATCH, CTX_LEN, D_MODEL), 1.0),
        'c_ctx': nrm((D_MODEL,), 1.0),
        'w_ada': nrm((L, D_MODEL, 6 * D_MODEL), 0.5 * D_MODEL ** -0.5),
        'b_ada': nrm((L, 6 * D_MODEL), 0.01),
        'w_in': nrm((L, D_MODEL, D_IN), D_MODEL ** -0.5),
        'mu_a': jax.random.uniform(next(ks), (L, A_SLAB), jnp.float32),
        'w0': jax.random.uniform(next(ks), (L, 2, D_A), jnp.float32, -6.0, 1.0),
        'w2': nrm((L, 2, LORA_DECAY, D_A), 0.5 * LORA_DECAY ** -0.5),
        'a0': nrm((L, 2, D_A), 0.5),
        'a2': nrm((L, 2, LORA_AAA, D_A), 0.5 * LORA_AAA ** -0.5),
        'g2': nrm((L, LORA_GATE, D_A), LORA_GATE ** -0.5),
        'k_k': 0.85 + nrm((L, D_A), 0.02),
        'k_a': 1.0 + nrm((L, D_A), 0.02),
        'r_k': nrm((L, A_HEADS, A_HEAD_DIM), 0.1),
        'gn_g': 1.0 + nrm((L, D_A), 0.02),
        'gn_b': nrm((L, D_A), 0.01),
        'conv_w': nrm((L, CONV_W, D_B), CONV_W ** -0.5),
        'conv_b': nrm((L, D_B), 0.01),
        'lru_wa': nrm((L, 2, B_BLOCKS, B_BLOCK_DIM, B_BLOCK_DIM), B_BLOCK_DIM ** -0.5),
        'lru_ba': nrm((L, 2, D_B), 0.01),
        'lru_wx': nrm((L, 2, B_BLOCKS, B_BLOCK_DIM, B_BLOCK_DIM), B_BLOCK_DIM ** -0.5),
        'lru_bx': nrm((L, 2, D_B), 0.01),
        'lru_lam': jnp.log(s) - jnp.log1p(-s),
        'p_a': nrm((L, D_A, D_MODEL), BETA * D_A ** -0.5),
        'p_b': nrm((L, D_B, D_MODEL), BETA * D_B ** -0.5),
        'w_o': nrm((L, D_MODEL, D_MODEL), BETA * D_MODEL ** -0.5),
        'ln1_g': 1.0 + nrm((L, D_MODEL), 0.02),
        'ln1_b': nrm((L, D_MODEL), 0.01),
        'router_g': nrm((L, D_MODEL, N_GROUPS), D_MODEL ** -0.5),
        'router_g_b': nrm((L, N_GROUPS), 0.01),
        'router_e': nrm((L, D_MODEL, N_EXPERTS), D_MODEL ** -0.5),
        'router_e_b': nrm((L, N_EXPERTS), 0.01),
        'e_w1': nrm((L, N_EXPERTS, D_MODEL, D_EXPERT), D_MODEL ** -0.5),
        'e_w3': nrm((L, N_EXPERTS, D_MODEL, D_EXPERT), D_MODEL ** -0.5),
        'e_w2': nrm((L, N_EXPERTS, D_EXPERT, D_MODEL), BETA * D_EXPERT ** -0.5),
        'ln2_g': 1.0 + nrm((L, D_MODEL), 0.02),
        'ln2_b': nrm((L, D_MODEL), 0.01),
    }


def reference(x, c, ctx, c_ctx, w_ada, b_ada, w_in, mu_a, w0, w2, a0, a2, g2, k_k, k_a, r_k,
              gn_g, gn_b, conv_w, conv_b, lru_wa, lru_ba, lru_wx, lru_bx, lru_lam,
              p_a, p_b, w_o, ln1_g, ln1_b, router_g, router_g_b, router_e, router_e_b,
              e_w1, e_w3, e_w2, ln2_g, ln2_b):
    b_ = x.shape[0]
    x = layer_norm(x)
    ctx = layer_norm(ctx)
    for l in range(DEPTH):
        mod = jax.nn.silu(c) @ w_ada[l] + b_ada[l]
        sh1, sc1, gt1, sh2, sc2, gt2 = jnp.split(mod[:, None, :], 6, axis=-1)
        mod_c = jax.nn.silu(c_ctx) @ w_ada[l] + b_ada[l]
        csh1, csc1, cgt1, csh2, csc2, cgt2 = jnp.split(mod_c, 6)
        rw = (mu_a[l], w0[l], w2[l], a0[l], a2[l], g2[l], k_k[l], k_a[l], r_k[l], gn_g[l], gn_b[l])
        lru = (conv_w[l], conv_b[l], lru_wa[l], lru_ba[l], lru_wx[l], lru_bx[l], lru_lam[l])
        moe_p = (router_g[l], router_g_b[l], router_e[l], router_e_b[l], e_w1[l], e_w3[l], e_w2[l])
        s0 = jnp.zeros((b_, A_HEADS, A_HEAD_DIM, A_HEAD_DIM), x.dtype)
        h0 = jnp.zeros((b_, D_B), x.dtype)
        mix_c, ctx_states = token_mix(ctx * (1.0 + csc1) + csh1, seq_shift, w_in[l], p_a[l], p_b[l], w_o[l],
                                      rw, lru, (s0, s0, h0, h0))
        mix_x, _ = token_mix(x * (1.0 + sc1) + sh1, grid_shift, w_in[l], p_a[l], p_b[l], w_o[l],
                             rw, lru, ctx_states)
        x = layer_norm(ALPHA * x + gt1 * mix_x, ln1_g[l], ln1_b[l])
        x = layer_norm(ALPHA * x + gt2 * hier_moe(x * (1.0 + sc2) + sh2, *moe_p), ln2_g[l], ln2_b[l])
        if l < DEPTH - 1:
            ctx = layer_norm(ALPHA * ctx + cgt1 * mix_c, ln1_g[l], ln1_b[l])
            ctx = layer_norm(ALPHA * ctx + cgt2 * hier_moe(ctx * (1.0 + csc2) + csh2, *moe_p),
                             ln2_g[l], ln2_b[l])
    return x
```

```python
import functools

import jax
import jax.numpy as jnp
from jax import lax
from jax.experimental import pallas as pl
from jax.experimental.pallas import tpu as pltpu

F32 = jnp.float32
BF16 = jnp.bfloat16

GRID_W = 64
HEAD = 64
LORA_DECAY = 64
LORA_AAA = 64
LORA_GATE = 160
GN_EPS = 64e-5
LN_EPS = 1e-5
CONV_W = 5
LRU_C = 8.0
N_GROUPS = 4
EXPERTS_PER_GROUP = 8
N_EXPERTS = N_GROUPS * EXPERTS_PER_GROUP
TOP_K = 2
DEPTH = 1
ALPHA = (2 * DEPTH) ** 0.25

WKV_CHUNK = 64
MOE_ROWS = 256
VMEM_LIMIT = 48 * 1024 * 1024

_NN = (((1,), (0,)), ((), ()))
_NT = (((1,), (1,)), ((), ()))
_TN = (((0,), (0,)), ((), ()))


def _split(x, n):
    if x.dtype == BF16:
        return [x]
    parts, rest = [], x.astype(F32)
    for i in range(n):
        p = rest.astype(BF16)
        parts.append(p)
        if i + 1 < n:
            rest = rest - p.astype(F32)
    return parts


def _dot(a, b, dims=_NN, pa=1, pb=1):
    ap, bp = _split(a, pa), _split(b, pb)
    order = max(len(ap), len(bp))
    acc = None
    for i, x in enumerate(ap):
        for j, y in enumerate(bp):
            if i + j < order:
                t = lax.dot_general(x, y, dims, preferred_element_type=F32)
                acc = t if acc is None else acc + t
    return acc


def _ln(x):
    mu = jnp.mean(x, -1, keepdims=True)
    xc = x - mu
    var = jnp.mean(xc * xc, -1, keepdims=True)
    return xc * lax.rsqrt(var + LN_EPS)


def _ln_mod_kernel(x_ref, sc_ref, sh_ref, x0_ref, h_ref):
    x0 = _ln(x_ref[0])
    x0_ref[0] = x0
    h_ref[0] = (x0 * (1.0 + sc_ref[0]) + sh_ref[0]).astype(h_ref.dtype)


def ln_mod(x, sc, sh, tm):
    b_, l_, d_ = x.shape
    row = pl.BlockSpec((1, tm, d_), lambda b, i: (b, i, 0))
    vec = pl.BlockSpec((1, 1, d_), lambda b, i: (b, 0, 0))
    return pl.pallas_call(
        _ln_mod_kernel,
        grid=(b_, l_ // tm),
        in_specs=[row, vec, vec],
        out_specs=[row, row],
        out_shape=[jax.ShapeDtypeStruct(x.shape, F32), jax.ShapeDtypeStruct(x.shape, BF16)],
        name="ln_mod",
    )(x, sc, sh)


def _res_ln_kernel(x_ref, y_ref, gt_ref, g_ref, b_ref, sc_ref, sh_ref, o_ref, h_ref):
    z = _ln(ALPHA * x_ref[0] + gt_ref[0] * y_ref[0]) * g_ref[...] + b_ref[...]
    o_ref[0] = z
    h_ref[0] = (z * (1.0 + sc_ref[0]) + sh_ref[0]).astype(h_ref.dtype)


def res_ln(x, y, gt, g, b, sc, sh, tm):
    b_, l_, d_ = x.shape
    row = pl.BlockSpec((1, tm, d_), lambda bi, i: (bi, i, 0))
    vec = pl.BlockSpec((1, 1, d_), lambda bi, i: (bi, 0, 0))
    par = pl.BlockSpec((1, d_), lambda bi, i: (0, 0))
    return pl.pallas_call(
        _res_ln_kernel,
        grid=(b_, l_ // tm),
        in_specs=[row, row, vec, par, par, vec, vec],
        out_specs=[row, row],
        out_shape=[jax.ShapeDtypeStruct(x.shape, F32), jax.ShapeDtypeStruct(x.shape, F32)],
        name="res_ln",
    )(x, y, gt, g.reshape(1, d_), b.reshape(1, d_), sc, sh)


def _mm_kernel(x_ref, w_ref, o_ref, *, pa, pb):
    o_ref[...] = _dot(x_ref[...], w_ref[...], _NN, pa, pb)


def pmm(x, w, tm=512, tn=1024, pa=1, pb=1):
    m_, k_ = x.shape
    n_ = w.shape[1]
    tm, tn = min(tm, m_), min(tn, n_)
    assert m_ % tm == 0 and n_ % tn == 0, (x.shape, w.shape, tm, tn)
    return pl.pallas_call(
        functools.partial(_mm_kernel, pa=pa, pb=pb),
        grid=(m_ // tm, n_ // tn),
        in_specs=[pl.BlockSpec((tm, k_), lambda i, j: (i, 0)), pl.BlockSpec((k_, tn), lambda i, j: (0, j))],
        out_specs=pl.BlockSpec((tm, tn), lambda i, j: (i, j)),
        out_shape=jax.ShapeDtypeStruct((m_, n_), F32),
        compiler_params=pltpu.CompilerParams(vmem_limit_bytes=VMEM_LIMIT),
        name="pmm",
    )(x, w)


def _gates_kernel(x_ref, w_ref, *o_refs):
    y = _dot(x_ref[...], w_ref[0])
    width = x_ref.shape[1]
    for k, o in enumerate(o_refs):
        o[...] = y[:, k * width:(k + 1) * width]


def block_gates(x, ws, tm=512):
    m_, d_ = x.shape
    grp = 4
    width = grp * HEAD
    ng = d_ // width
    eye = jnp.eye(grp, dtype=F32)

    def dense(w):
        wg = w.reshape(ng, grp, HEAD, HEAD)
        return jnp.einsum("gaij,ab->gaibj", wg, eye).reshape(ng, width, width)

    w_all = jnp.concatenate([dense(w) for w in ws], axis=-1).astype(BF16)
    tm = min(tm, m_)
    out = pl.BlockSpec((tm, width), lambda i, g: (i, g))
    return pl.pallas_call(
        _gates_kernel,
        grid=(m_ // tm, ng),
        in_specs=[pl.BlockSpec((tm, width), lambda i, g: (i, g)),
                  pl.BlockSpec((1, width, width * len(ws)), lambda i, g: (g, 0, 0))],
        out_specs=[out] * len(ws),
        out_shape=[jax.ShapeDtypeStruct((m_, d_), F32)] * len(ws),
        name="block_gates",
    )(x, w_all)


def _wkv_kernel(r_ref, v_ref, kk_ref, lw_ref, k_ref, a_ref, s0_ref, y_ref, sT_ref, s_scr, *, heads, pa):
    d = pl.program_id(0)
    c = pl.program_id(2)
    n_c = pl.num_programs(2)
    cs = r_ref.shape[1]

    @pl.when(c == 0)
    def _():
        s_scr[...] = s0_ref[0, 0]

    sgn = 1 - 2 * d
    row = lax.broadcasted_iota(jnp.int32, (cs, cs), 0)
    col = lax.broadcasted_iota(jnp.int32, (cs, cs), 1)
    diff = (row - col) * sgn
    incl = diff >= 0
    strict = diff > 0
    tri = jnp.where(incl, 1.0, 0.0).astype(BF16)

    lw = lw_ref[0, 0]
    cum = _dot(tri, lw, _NN, 1, 3)
    tot = jnp.sum(lw, axis=0, keepdims=True)
    r, v, kk, k, a = r_ref[0], v_ref[0], kk_ref[0], k_ref[0, 0], a_ref[0, 0]
    b = kk * a
    e_neg = jnp.exp(-cum)
    rt = r * jnp.exp(cum)
    at = -kk * jnp.exp(cum - lw)
    bt = b * e_neg
    kt = k * e_neg
    e_end = jnp.exp(tot - cum)
    bd = b * e_end
    kd = k * e_end
    p_end = jnp.exp(tot)

    n_sq = max(cs.bit_length() - 1, 0)
    for h in range(heads):
        sl = slice(h * HEAD, (h + 1) * HEAD)
        s = s_scr[h]
        at_h, rt_h, bt_h, kt_h, v_h = at[:, sl], rt[:, sl], bt[:, sl], kt[:, sl], v[:, sl]
        a_ab = jnp.where(strict, _dot(at_h, bt_h, _NT, pa, pa), 0.0)
        a_ak = jnp.where(strict, _dot(at_h, kt_h, _NT, pa, pa), 0.0)
        a_rb = jnp.where(incl, _dot(rt_h, bt_h, _NT, pa, pa), 0.0)
        a_rk = jnp.where(incl, _dot(rt_h, kt_h, _NT, pa, pa), 0.0)
        u = _dot(at_h, s, _NT, pa, pa) + _dot(a_ak, v_h, _NN, pa, pa)
        nil = a_ab
        for i in range(n_sq):
            u = u + _dot(nil, u, _NN, pa, pa)
            if i + 1 < n_sq:
                nil = _dot(nil, nil, _NN, pa, pa)
        y = _dot(rt_h, s, _NT, pa, pa) + _dot(a_rb, u, _NN, pa, pa) + _dot(a_rk, v_h, _NN, pa, pa)
        y_ref[0, 0, :, sl] = y
        s_scr[h] = s * p_end[:, sl] + _dot(u, bd[:, sl], _TN, pa, pa) + _dot(v_h, kd[:, sl], _TN, pa, pa)

    @pl.when(c == n_c - 1)
    def _():
        sT_ref[0, 0] = s_scr[...]


def wkv7(r, v, kk, lw, k, a, s0, pa=2):
    b_, l_, d_ = r.shape
    heads = d_ // HEAD
    cs = min(WKV_CHUNK, l_)
    n_c = l_ // cs

    def tmap(d, b, c):
        return c + d * (n_c - 1 - 2 * c)

    shared = pl.BlockSpec((1, cs, d_), lambda d, b, c: (b, tmap(d, b, c), 0))
    per_dir = pl.BlockSpec((1, 1, cs, d_), lambda d, b, c: (d, b, tmap(d, b, c), 0))
    state = pl.BlockSpec((1, 1, heads, HEAD, HEAD), lambda d, b, c: (d, b, 0, 0, 0))
    return pl.pallas_call(
        functools.partial(_wkv_kernel, heads=heads, pa=pa),
        grid=(2, b_, n_c),
        in_specs=[shared, shared, shared, per_dir, per_dir, per_dir, state],
        out_specs=[per_dir, state],
        out_shape=[jax.ShapeDtypeStruct((2, b_, l_, d_), F32),
                   jax.ShapeDtypeStruct((2, b_, heads, HEAD, HEAD), F32)],
        scratch_shapes=[pltpu.VMEM((heads, HEAD, HEAD), F32)],
        compiler_params=pltpu.CompilerParams(
            dimension_semantics=("arbitrary", "arbitrary", "arbitrary"), vmem_limit_bytes=VMEM_LIMIT),
        name="wkv7",
    )(r, v, kk, lw, k, a, s0)


def _lscan_kernel(a_ref, u_ref, h0_ref, h_ref, hT_ref, h_scr):
    d = pl.program_id(0)
    c = pl.program_id(2)
    n_c = pl.num_programs(2)
    tm = a_ref.shape[2]

    @pl.when(c == 0)
    def _():
        h_scr[...] = h0_ref[0, 0]

    def step(i, h):
        t = i + d * (tm - 1 - 2 * i)
        h = a_ref[0, 0, pl.ds(t, 1), :] * h + u_ref[0, 0, pl.ds(t, 1), :]
        h_ref[0, 0, pl.ds(t, 1), :] = h
        return h

    h_scr[...] = lax.fori_loop(0, tm, step, h_scr[...], unroll=8)

    @pl.when(c == n_c - 1)
    def _():
        hT_ref[0, 0] = h_scr[...]


def linear_scan2(a, u, h0, tm=512):
    _, b_, l_, d_ = a.shape
    tm = min(tm, l_)
    n_c = l_ // tm
    blk = pl.BlockSpec((1, 1, tm, d_), lambda d, b, c: (d, b, c + d * (n_c - 1 - 2 * c), 0))
    st = pl.BlockSpec((1, 1, 1, d_), lambda d, b, c: (d, b, 0, 0))
    return pl.pallas_call(
        _lscan_kernel,
        grid=(2, b_, n_c),
        in_specs=[blk, blk, st],
        out_specs=[blk, st],
        out_shape=[jax.ShapeDtypeStruct(a.shape, F32), jax.ShapeDtypeStruct(h0.shape, F32)],
        scratch_shapes=[pltpu.VMEM((1, d_), F32)],
        compiler_params=pltpu.CompilerParams(dimension_semantics=("arbitrary", "arbitrary", "arbitrary")),
        name="lscan",
    )(a, u, h0)


def _moe_kernel(be_ref, nv_ref, x_ref, w1_ref, w3_ref, w2_ref, o_ref):
    blk = pl.program_id(0)

    @pl.when(nv_ref[blk] > 0)
    def _():
        x = x_ref[...].astype(BF16)
        h1 = _dot(x, w1_ref[0].astype(BF16))
        h3 = _dot(x, w3_ref[0].astype(BF16))
        hh = h1 * jax.nn.sigmoid(h1) * h3
        o_ref[...] = _dot(hh, w2_ref[0].astype(BF16))

    @pl.when(nv_ref[blk] == 0)
    def _():
        o_ref[...] = jnp.zeros_like(o_ref)


def moe_experts(xs, blk_e, blk_n, w1, w3, w2):
    n_pad, d_ = xs.shape
    n_blocks = n_pad // MOE_ROWS
    de = w1.shape[2]
    grid_spec = pltpu.PrefetchScalarGridSpec(
        num_scalar_prefetch=2,
        grid=(n_blocks,),
        in_specs=[pl.BlockSpec((MOE_ROWS, d_), lambda i, be, nv: (i, 0)),
                  pl.BlockSpec((1, d_, de), lambda i, be, nv: (be[i], 0, 0)),
                  pl.BlockSpec((1, d_, de), lambda i, be, nv: (be[i], 0, 0)),
                  pl.BlockSpec((1, de, d_), lambda i, be, nv: (be[i], 0, 0))],
        out_specs=pl.BlockSpec((MOE_ROWS, d_), lambda i, be, nv: (i, 0)),
    )
    return pl.pallas_call(
        _moe_kernel,
        grid_spec=grid_spec,
        out_shape=jax.ShapeDtypeStruct((n_pad, d_), F32),
        compiler_params=pltpu.CompilerParams(vmem_limit_bytes=VMEM_LIMIT),
        name="moe_experts",
    )(blk_e, blk_n, xs, w1, w3, w2)


def grid_shift(z):
    b_, l_, ch = z.shape
    rows = l_ // GRID_W
    zg = z.reshape(b_, rows, GRID_W, ch // 4, 4)
    left = jnp.pad(zg[:, :, :-1, :, 0], ((0, 0), (0, 0), (1, 0), (0, 0)))
    right = jnp.pad(zg[:, :, 1:, :, 1], ((0, 0), (0, 0), (0, 1), (0, 0)))
    up = jnp.pad(zg[:, :-1, :, :, 2], ((0, 0), (1, 0), (0, 0), (0, 0)))
    down = jnp.pad(zg[:, 1:, :, :, 3], ((0, 0), (0, 1), (0, 0), (0, 0)))
    return jnp.stack([left, right, up, down], -1).reshape(b_, l_, ch)


def seq_shift(z):
    b_, l_, ch = z.shape
    zs = z.reshape(b_, l_, ch // 2, 2)
    prev = jnp.pad(zs[:, :-1, :, 0], ((0, 0), (1, 0), (0, 0)))
    nxt = jnp.pad(zs[:, 1:, :, 1], ((0, 0), (0, 1), (0, 0)))
    return jnp.stack([prev, nxt], -1).reshape(b_, l_, ch)


def _heads(t):
    return t.reshape(t.shape[:-1] + (t.shape[-1] // HEAD, HEAD))


def token_mix(h, shift_fn, wts, states, want_mix):
    b_, l_, d_ = h.shape
    m_ = b_ * l_
    hf = h.reshape(m_, d_)
    tm = min(512, m_)
    main = pmm(hf, wts["w_main"], tm, 1024)
    lora = pmm(hf, wts["w_lora"], tm, wts["w_lora"].shape[1])
    sec = lambda j: main[:, j * d_:(j + 1) * d_].reshape(b_, l_, d_)
    mu = wts["mu_a"]
    mix = lambda z, m: z + m * (shift_fn(z) - z)
    r, k, v = mix(sec(0), mu[:d_]), mix(sec(1), mu[d_:2 * d_]), mix(sec(2), mu[2 * d_:3 * d_])
    lo = 3 * d_
    lz = mix(lora[:, :LORA_DECAY + LORA_AAA + LORA_GATE].reshape(b_, l_, -1), mu[lo:])
    wd, ad, gd = lz[..., :LORA_DECAY], lz[..., LORA_DECAY:LORA_DECAY + LORA_AAA], lz[..., LORA_DECAY + LORA_AAA:]

    g = pmm(jax.nn.sigmoid(gd).reshape(m_, -1), wts["g2"], tm, d_).reshape(b_, l_, d_)
    tw = jnp.tanh(wd).reshape(m_, -1)
    adf = ad.reshape(m_, -1)
    lw, aa, kmod = [], [], []
    for e in range(2):
        wl = -jax.nn.softplus(-(wts["w0"][e] + pmm(tw, wts["w2"][e], tm, d_).reshape(b_, l_, d_))) - 0.5
        lw.append(-jnp.exp(wl))
        a_e = jax.nn.sigmoid(wts["a0"][e] + pmm(adf, wts["a2"][e], tm, d_).reshape(b_, l_, d_))
        aa.append(a_e)
        kmod.append(k * (1.0 + (a_e - 1.0) * wts["k_a"]))
    lw, aa, kmod = jnp.stack(lw), jnp.stack(aa), jnp.stack(kmod)
    kk = _heads(k * wts["k_k"])
    kk = (kk * lax.rsqrt(jnp.sum(jnp.square(kk), -1, keepdims=True) + 1e-12)).reshape(b_, l_, d_)
    s0, h0 = states
    y2, s_new = wkv7(r, v, kk, lw, kmod, aa, s0)
    new_states = [s_new, None]
    y = _heads(y2[0] + y2[1])
    ymu = jnp.mean(y, -1, keepdims=True)
    yvar = jnp.mean(jnp.square(y - ymu), -1, keepdims=True)
    yn = ((y - ymu) * lax.rsqrt(yvar + GN_EPS)).reshape(b_, l_, d_) * wts["gn_g"] + wts["gn_b"]
    bonus = jnp.sum(_heads(r * (kmod[0] + kmod[1]) * wts["r_k"]), -1, keepdims=True) * _heads(v)
    y_a = (yn + bonus.reshape(b_, l_, d_)) * g

    xb, gb = sec(3), sec(4)
    xp = jnp.pad(xb, ((0, 0), (CONV_W // 2, CONV_W // 2), (0, 0)))
    xc = wts["conv_b"] + sum(xp[:, i:i + l_] * wts["conv_w"][i] for i in range(CONV_W))
    gates = block_gates(xc.reshape(m_, d_), [wts["lru_wa"][0], wts["lru_wa"][1], wts["lru_wx"][0], wts["lru_wx"][1]], tm)
    a_l, u_l = [], []
    for e in range(2):
        rg = jax.nn.sigmoid(gates[e].reshape(b_, l_, d_) + wts["lru_ba"][e])
        ig = jax.nn.sigmoid(gates[2 + e].reshape(b_, l_, d_) + wts["lru_bx"][e])
        log_a = -LRU_C * rg * jax.nn.softplus(-wts["lru_lam"][e])
        a_l.append(jnp.exp(log_a))
        u_l.append(jnp.sqrt(-jnp.expm1(2.0 * log_a)) * ig * xc)
    hh, h_new = linear_scan2(jnp.stack(a_l), jnp.stack(u_l), h0)
    new_states[1] = h_new
    if not want_mix:
        return None, new_states
    y_b = (hh[0] + hh[1]) * jax.nn.gelu(gb)

    ga, gbm = sec(5), sec(6)
    pa_ = pmm(y_a.reshape(m_, d_), wts["p_a"], tm, d_).reshape(b_, l_, d_)
    pb_ = pmm(y_b.reshape(m_, d_), wts["p_b"], tm, d_).reshape(b_, l_, d_)
    mm_ = jax.nn.sigmoid(ga) * pa_ + jax.nn.sigmoid(gbm) * pb_
    return pmm(mm_.reshape(m_, d_), wts["w_o"], tm, d_).reshape(b_, l_, d_), new_states


def hier_moe(h, rw, rb, w1, w3, w2):
    t_, d_ = h.shape
    logits = pmm(h, rw, min(512, t_), rw.shape[1], 2, 2)[:, :N_GROUPS + N_EXPERTS] + rb
    glog = logits[:, :N_GROUPS]
    gsel = jnp.argmax(glog, -1)
    p_g = jnp.take_along_axis(jax.nn.softmax(glog, -1), gsel[:, None], -1)
    elog = logits[:, N_GROUPS:].reshape(-1, N_GROUPS, EXPERTS_PER_GROUP)
    elog_g = jnp.take_along_axis(elog, gsel[:, None, None], 1)[:, 0]
    top_val, top_idx = lax.top_k(elog_g, TOP_K)
    wts = (jax.nn.softmax(top_val, -1) * p_g).reshape(-1)
    eid = (gsel[:, None] * EXPERTS_PER_GROUP + top_idx).reshape(-1).astype(jnp.int32)

    n_assign = eid.shape[0]
    order = jnp.argsort(eid)
    e_sorted = eid[order]
    counts = jnp.bincount(eid, length=N_EXPERTS)
    padded = (counts + MOE_ROWS - 1) // MOE_ROWS * MOE_ROWS
    pad_end = jnp.cumsum(padded)
    pad_start = pad_end - padded
    start = jnp.cumsum(counts) - counts
    dest_sorted = (pad_start[e_sorted] + jnp.arange(n_assign) - start[e_sorted]).astype(jnp.int32)
    n_blocks = -(-n_assign // MOE_ROWS) + N_EXPERTS
    n_pad = n_blocks * MOE_ROWS
    buf_tok = jnp.zeros((n_pad,), jnp.int32).at[dest_sorted].set((order // TOP_K).astype(jnp.int32))
    dest = jnp.zeros((n_assign,), jnp.int32).at[order].set(dest_sorted)
    blk_lo = jnp.arange(n_blocks) * MOE_ROWS
    blk_e = jnp.minimum(jnp.searchsorted(pad_end, blk_lo, side="right"), N_EXPERTS - 1).astype(jnp.int32)
    blk_n = jnp.clip(pad_start[blk_e] + counts[blk_e] - blk_lo, 0, MOE_ROWS).astype(jnp.int32)
    blk_n = jnp.where(blk_lo < pad_end[-1], blk_n, 0)

    xs = h[buf_tok]
    ys = moe_experts(xs, blk_e, blk_n, w1, w3, w2)
    yw = ys[dest] * wts[:, None]
    return yw.reshape(t_, TOP_K, d_).sum(1)


def kernel(x, c, ctx, c_ctx, w_ada, b_ada, w_in, mu_a, w0, w2, a0, a2, g2, k_k, k_a, r_k, gn_g, gn_b, conv_w, conv_b, lru_wa, lru_ba, lru_wx, lru_bx, lru_lam, p_a, p_b, w_o, ln1_g, ln1_b, router_g, router_g_b, router_e, router_e_b, e_w1, e_w3, e_w2, ln2_g, ln2_b):
    b_, l_, d_ = x.shape
    lc = ctx.shape[1]
    heads = d_ // HEAD
    l = 0
    cc = jnp.concatenate([c, c_ctx[None]], 0)
    cc = jnp.pad(jax.nn.silu(cc), ((0, 8 - cc.shape[0]), (0, 0)))
    mod = pmm(cc, w_ada[l], 8, 1024, 2, 2)[:b_ + 1] + b_ada[l]
    sh1, sc1, gt1, sh2, sc2, gt2 = [m[:b_, None, :] for m in jnp.split(mod, 6, axis=-1)]
    csh1, csc1 = [jnp.broadcast_to(m[b_:, None, :], (b_, 1, d_)) for m in jnp.split(mod, 6, axis=-1)[:2]]

    a_slab = 3 * d_ + LORA_DECAY + LORA_AAA + LORA_GATE
    wi = w_in[l]
    n_lora = a_slab - 3 * d_
    lora_pad = -(-n_lora // 128) * 128
    wts = dict(
        w_main=jnp.concatenate([wi[:, :3 * d_], wi[:, a_slab:]], 1).astype(BF16),
        w_lora=jnp.pad(wi[:, 3 * d_:a_slab], ((0, 0), (0, lora_pad - n_lora))).astype(BF16),
        mu_a=mu_a[l], w0=w0[l], w2=w2[l], a0=a0[l], a2=a2[l], g2=g2[l], k_k=k_k[l], k_a=k_a[l],
        r_k=r_k[l].reshape(-1), gn_g=gn_g[l], gn_b=gn_b[l], conv_w=conv_w[l], conv_b=conv_b[l],
        lru_wa=lru_wa[l], lru_ba=lru_ba[l], lru_wx=lru_wx[l], lru_bx=lru_bx[l], lru_lam=lru_lam[l],
        p_a=p_a[l], p_b=p_b[l], w_o=w_o[l])

    s0 = jnp.zeros((2, b_, heads, HEAD, HEAD), F32)
    h0 = jnp.zeros((2, b_, 1, d_), F32)
    _, hc = ln_mod(ctx, csc1, csh1, min(512, lc))
    _, ctx_states = token_mix(hc, seq_shift, wts, (s0, h0), want_mix=False)
    x0, hx = ln_mod(x, sc1, sh1, 512)
    mix_x, _ = token_mix(hx, grid_shift, wts, ctx_states, want_mix=True)
    x1, h2 = res_ln(x0, mix_x, gt1, ln1_g[l], ln1_b[l], sc2, sh2, 512)

    rw = jnp.pad(jnp.concatenate([router_g[l], router_e[l]], 1), ((0, 0), (0, 128 - N_GROUPS - N_EXPERTS)))
    rb = jnp.concatenate([router_g_b[l], router_e_b[l]])
    moe = hier_moe(h2.reshape(b_ * l_, d_), rw, rb, e_w1[l], e_w3[l], e_w2[l]).reshape(b_, l_, d_)
    zero = jnp.zeros_like(sc2)
    out, _ = res_ln(x1, moe, gt2, ln2_g[l], ln2_b[l], zero, zero, 512)
    return out
```

```python
import functools

import jax
import jax.numpy as jnp
from jax import lax
from jax.experimental import pallas as pl
from jax.experimental.pallas import tpu as pltpu

F32 = jnp.float32
BF16 = jnp.bfloat16

GRID_W = 64
HEAD = 64
LORA_DECAY = 64
LORA_AAA = 64
LORA_GATE = 160
GN_EPS = 64e-5
LN_EPS = 1e-5
CONV_W = 5
LRU_C = 8.0
N_GROUPS = 4
EXPERTS_PER_GROUP = 8
N_EXPERTS = N_GROUPS * EXPERTS_PER_GROUP
TOP_K = 2
DEPTH = 1
ALPHA = (2 * DEPTH) ** 0.25

WKV_CHUNK = 64
MOE_ROWS = 256
VMEM_LIMIT = 48 * 1024 * 1024

_NN = (((1,), (0,)), ((), ()))
_NT = (((1,), (1,)), ((), ()))
_TN = (((0,), (0,)), ((), ()))


def _split(x, n):
    if x.dtype == BF16:
        return [x]
    parts, rest = [], x.astype(F32)
    for i in range(n):
        p = rest.astype(BF16)
        parts.append(p)
        if i + 1 < n:
            rest = rest - p.astype(F32)
    return parts


def _dot(a, b, dims=_NN, pa=1, pb=1):
    ap, bp = _split(a, pa), _split(b, pb)
    order = max(len(ap), len(bp))
    acc = None
    for i, x in enumerate(ap):
        for j, y in enumerate(bp):
            if i + j < order:
                t = lax.dot_general(x, y, dims, preferred_element_type=F32)
                acc = t if acc is None else acc + t
    return acc


def _ln(x):
    mu = jnp.mean(x, -1, keepdims=True)
    xc = x - mu
    var = jnp.mean(xc * xc, -1, keepdims=True)
    return xc * lax.rsqrt(var + LN_EPS)


def _ln_mod_kernel(x_ref, sc_ref, sh_ref, x0_ref, h_ref):
    x0 = _ln(x_ref[0])
    x0_ref[0] = x0
    h_ref[0] = (x0 * (1.0 + sc_ref[0]) + sh_ref[0]).astype(h_ref.dtype)


def ln_mod(x, sc, sh, tm):
    b_, l_, d_ = x.shape
    row = pl.BlockSpec((1, tm, d_), lambda b, i: (b, i, 0))
    vec = pl.BlockSpec((1, 1, d_), lambda b, i: (b, 0, 0))
    return pl.pallas_call(
        _ln_mod_kernel,
        grid=(b_, l_ // tm),
        in_specs=[row, vec, vec],
        out_specs=[row, row],
        out_shape=[jax.ShapeDtypeStruct(x.shape, F32), jax.ShapeDtypeStruct(x.shape, BF16)],
        name="ln_mod",
    )(x, sc, sh)


def _res_ln_kernel(x_ref, y_ref, gt_ref, g_ref, b_ref, sc_ref, sh_ref, o_ref, h_ref):
    z = _ln(ALPHA * x_ref[0] + gt_ref[0] * y_ref[0]) * g_ref[...] + b_ref[...]
    o_ref[0] = z
    h_ref[0] = (z * (1.0 + sc_ref[0]) + sh_ref[0]).astype(h_ref.dtype)


def res_ln(x, y, gt, g, b, sc, sh, tm):
    b_, l_, d_ = x.shape
    row = pl.BlockSpec((1, tm, d_), lambda bi, i: (bi, i, 0))
    vec = pl.BlockSpec((1, 1, d_), lambda bi, i: (bi, 0, 0))
    par = pl.BlockSpec((1, d_), lambda bi, i: (0, 0))
    return pl.pallas_call(
        _res_ln_kernel,
        grid=(b_, l_ // tm),
        in_specs=[row, row, vec, par, par, vec, vec],
        out_specs=[row, row],
        out_shape=[jax.ShapeDtypeStruct(x.shape, F32), jax.ShapeDtypeStruct(x.shape, F32)],
        name="res_ln",
    )(x, y, gt, g.reshape(1, d_), b.reshape(1, d_), sc, sh)


def _mm_kernel(x_ref, w_ref, o_ref, *, pa, pb):
    o_ref[...] = _dot(x_ref[...], w_ref[...], _NN, pa, pb)


def pmm(x, w, tm=512, tn=1024, pa=1, pb=1):
    m_, k_ = x.shape
    n_ = w.shape[1]
    tm, tn = min(tm, m_), min(tn, n_)
    assert m_ % tm == 0 and n_ % tn == 0, (x.shape, w.shape, tm, tn)
    return pl.pallas_call(
        functools.partial(_mm_kernel, pa=pa, pb=pb),
        grid=(m_ // tm, n_ // tn),
        in_specs=[pl.BlockSpec((tm, k_), lambda i, j: (i, 0)), pl.BlockSpec((k_, tn), lambda i, j: (0, j))],
        out_specs=pl.BlockSpec((tm, tn), lambda i, j: (i, j)),
        out_shape=jax.ShapeDtypeStruct((m_, n_), F32),
        compiler_params=pltpu.CompilerParams(vmem_limit_bytes=VMEM_LIMIT),
        name="pmm",
    )(x, w)


def _gates_kernel(x_ref, w_ref, *o_refs):
    y = _dot(x_ref[...], w_ref[0])
    width = x_ref.shape[1]
    for k, o in enumerate(o_refs):
        o[...] = y[:, k * width:(k + 1) * width]


def block_gates(x, ws, tm=512):
    m_, d_ = x.shape
    grp = 4
    width = grp * HEAD
    ng = d_ // width
    eye = jnp.eye(grp, dtype=F32)

    def dense(w):
        wg = w.reshape(ng, grp, HEAD, HEAD)
        return jnp.einsum("gaij,ab->gaibj", wg, eye).reshape(ng, width, width)

    w_all = jnp.concatenate([dense(w) for w in ws], axis=-1).astype(BF16)
    tm = min(tm, m_)
    out = pl.BlockSpec((tm, width), lambda i, g: (i, g))
    return pl.pallas_call(
        _gates_kernel,
        grid=(m_ // tm, ng),
        in_specs=[pl.BlockSpec((tm, width), lambda i, g: (i, g)),
                  pl.BlockSpec((1, width, width * len(ws)), lambda i, g: (g, 0, 0))],
        out_specs=[out] * len(ws),
        out_shape=[jax.ShapeDtypeStruct((m_, d_), F32)] * len(ws),
        name="block_gates",
    )(x, w_all)


def _wkv_kernel(r_ref, v_ref, kk_ref, lw_ref, k_ref, a_ref, s0_ref, y_ref, sT_ref, s_scr):
    d = pl.program_id(0)
    c = pl.program_id(2)
    n_c = pl.num_programs(2)
    cs = r_ref.shape[1]
    pairs = s_scr.shape[0]
    pw = 2 * HEAD

    @pl.when(c == 0)
    def _():
        s_scr[...] = s0_ref[0, 0]

    sgn = 1 - 2 * d
    row = lax.broadcasted_iota(jnp.int32, (cs, cs), 0)
    col = lax.broadcasted_iota(jnp.int32, (cs, cs), 1)
    tri = jnp.where((row - col) * sgn >= 0, 1.0, 0.0).astype(BF16)
    row2 = lax.broadcasted_iota(jnp.int32, (2 * cs, pw), 0)
    col2 = lax.broadcasted_iota(jnp.int32, (2 * cs, pw), 1)
    tdiff = (row2 % cs - col2 % cs) * sgn
    amask = (tdiff > 0) | ((row2 >= cs) & (tdiff == 0))
    same_head = (row2 >= cs) == (col2 >= HEAD)
    eye = jnp.where(row2 == col2, 1.0, 0.0)
    lane = lax.broadcasted_iota(jnp.int32, (cs, pw), 1)
    m0, m1 = lane < HEAD, lane >= HEAD
    lane2 = lax.broadcasted_iota(jnp.int32, (2 * cs, pw), 1)
    mm0, mm1 = lane2 < HEAD, lane2 >= HEAD

    lw = lw_ref[0, 0]
    cum = _dot(tri, lw, _NN, 1, 3)
    tot = jnp.sum(lw, axis=0, keepdims=True)
    r, kk, k, a = r_ref[0], kk_ref[0], k_ref[0, 0], a_ref[0, 0]
    b = kk * a
    e_neg = jnp.exp(-cum)
    e_end = jnp.exp(tot - cum)
    rt = (r * jnp.exp(cum)).astype(BF16)
    at = (-kk * jnp.exp(cum - lw)).astype(BF16)
    bt = (b * e_neg).astype(BF16)
    kt = (k * e_neg).astype(BF16)
    bd = (b * e_end).astype(BF16)
    kd = (k * e_end).astype(BF16)
    vb = v_ref[0].astype(BF16)
    p_end = jnp.exp(tot)
    zero = jnp.zeros((cs, pw), BF16)
    cat = lambda x, y: jnp.concatenate([x, y], axis=0)
    dot = lambda x, y, dims=_NN: lax.dot_general(x, y, dims, preferred_element_type=F32)
    sel = lambda m, x: jnp.where(m, x, jnp.zeros_like(x))

    ar, aa0, aa1, xs, ts, nbs = [], [], [], [], [], []
    for p in range(pairs):
        sl = slice(p * pw, (p + 1) * pw)
        ar_p = cat(at[:, sl], rt[:, sl])
        q0 = sel(amask, dot(sel(mm0, ar_p), cat(bt[:, sl], kt[:, sl]), _NT))
        q1 = sel(amask, dot(sel(mm1, ar_p), cat(kt[:, sl], bt[:, sl]), _NT))
        nil = cat(sel(m0, q0[:cs]), sel(m1, q1[:cs]))
        ar.append(ar_p); aa0.append(q0.astype(BF16)); aa1.append(q1.astype(BF16))
        ts.append(eye + nil); nbs.append(nil.astype(BF16))
    for p in range(pairs):
        v_p = vb[:, p * pw:(p + 1) * pw]
        xs.append(dot(aa0[p][:cs], cat(zero, sel(m0, v_p))) + dot(aa1[p][:cs], cat(sel(m1, v_p), zero)))
    for i in range(max(cs.bit_length() - 2, 0)):
        nbs = [dot(nb, nb).astype(BF16) for nb in nbs]
        ts = [t + dot(t.astype(BF16), nb) for t, nb in zip(ts, nbs)]
    ts = [t.astype(BF16) for t in ts]

    st = [s_scr[p] for p in range(pairs)]
    as_ = [dot(ar[p], st[p].astype(BF16)) for p in range(pairs)]
    us = []
    for p in range(pairs):
        rhs = (as_[p][:cs] + xs[p]).astype(BF16)
        uu = dot(ts[p], cat(sel(m0, rhs), sel(m1, rhs)))
        us.append((uu[:cs] + uu[cs:]).astype(BF16))
    for p in range(pairs):
        sl = slice(p * pw, (p + 1) * pw)
        u_p, v_p = us[p], vb[:, sl]
        y = (as_[p][cs:] + dot(aa0[p][cs:], cat(sel(m0, u_p), sel(m0, v_p)))
             + dot(aa1[p][cs:], cat(sel(m1, v_p), sel(m1, u_p))))
        y_ref[0, 0, :, sl] = y
        p_col = jnp.broadcast_to(p_end[:, sl], (pw, pw)).T
        upd = dot(cat(bd[:, sl], kd[:, sl]), cat(u_p, v_p), _TN)
        s_scr[p] = p_col * st[p] + sel(same_head, upd)

    @pl.when(c == n_c - 1)
    def _():
        sT_ref[0, 0] = s_scr[...]


def _pair_states(s):
    st = jnp.swapaxes(s, -1, -2)
    st = st.reshape(s.shape[:-3] + (s.shape[-3] // 2, 2, HEAD, HEAD))
    eye = jnp.eye(2, dtype=s.dtype)
    out = jnp.einsum("...pakv,ab->...pakbv", st, eye)
    return out.reshape(s.shape[:-3] + (s.shape[-3] // 2, 2 * HEAD, 2 * HEAD))


def _unpair_states(sp):
    lead = sp.shape[:-3]
    pairs = sp.shape[-3]
    s5 = sp.reshape(lead + (pairs, 2, HEAD, 2, HEAD))
    st = jnp.stack([s5[..., 0, :, 0, :], s5[..., 1, :, 1, :]], axis=-3)
    return jnp.swapaxes(st.reshape(lead + (2 * pairs, HEAD, HEAD)), -1, -2)


def wkv7(r, v, kk, lw, k, a, s0):
    b_, l_, d_ = r.shape
    heads = d_ // HEAD
    pairs, pw = heads // 2, 2 * HEAD
    cs = min(WKV_CHUNK, l_)
    assert cs == HEAD and l_ % cs == 0
    n_c = l_ // cs

    def tmap(d, b, c):
        return c + d * (n_c - 1 - 2 * c)

    shared = pl.BlockSpec((1, cs, d_), lambda d, b, c: (b, tmap(d, b, c), 0))
    per_dir = pl.BlockSpec((1, 1, cs, d_), lambda d, b, c: (d, b, tmap(d, b, c), 0))
    state = pl.BlockSpec((1, 1, pairs, pw, pw), lambda d, b, c: (d, b, 0, 0, 0))
    y, s_end = pl.pallas_call(
        _wkv_kernel,
        grid=(2, b_, n_c),
        in_specs=[shared, shared, shared, per_dir, per_dir, per_dir, state],
        out_specs=[per_dir, state],
        out_shape=[jax.ShapeDtypeStruct((2, b_, l_, d_), F32),
                   jax.ShapeDtypeStruct((2, b_, pairs, pw, pw), F32)],
        scratch_shapes=[pltpu.VMEM((pairs, pw, pw), F32)],
        compiler_params=pltpu.CompilerParams(
            dimension_semantics=("arbitrary", "arbitrary", "arbitrary"), vmem_limit_bytes=VMEM_LIMIT),
        name="wkv7",
    )(r, v, kk, lw, k, a, _pair_states(s0))
    return y, _unpair_states(s_end)


def _lscan_kernel(a_ref, u_ref, h0_ref, h_ref, hT_ref, h_scr):
    d = pl.program_id(0)
    c = pl.program_id(2)
    n_c = pl.num_programs(2)
    tm = a_ref.shape[2]

    @pl.when(c == 0)
    def _():
        h_scr[...] = h0_ref[0, 0]

    def step(i, h):
        t = i + d * (tm - 1 - 2 * i)
        h = a_ref[0, 0, pl.ds(t, 1), :] * h + u_ref[0, 0, pl.ds(t, 1), :]
        h_ref[0, 0, pl.ds(t, 1), :] = h
        return h

    h_scr[...] = lax.fori_loop(0, tm, step, h_scr[...], unroll=8)

    @pl.when(c == n_c - 1)
    def _():
        hT_ref[0, 0] = h_scr[...]


def linear_scan2(a, u, h0, tm=512):
    _, b_, l_, d_ = a.shape
    tm = min(tm, l_)
    n_c = l_ // tm
    blk = pl.BlockSpec((1, 1, tm, d_), lambda d, b, c: (d, b, c + d * (n_c - 1 - 2 * c), 0))
    st = pl.BlockSpec((1, 1, 1, d_), lambda d, b, c: (d, b, 0, 0))
    return pl.pallas_call(
        _lscan_kernel,
        grid=(2, b_, n_c),
        in_specs=[blk, blk, st],
        out_specs=[blk, st],
        out_shape=[jax.ShapeDtypeStruct(a.shape, F32), jax.ShapeDtypeStruct(h0.shape, F32)],
        scratch_shapes=[pltpu.VMEM((1, d_), F32)],
        compiler_params=pltpu.CompilerParams(dimension_semantics=("arbitrary", "arbitrary", "arbitrary")),
        name="lscan",
    )(a, u, h0)


def _moe_kernel(be_ref, nv_ref, x_ref, w1_ref, w3_ref, w2_ref, o_ref):
    blk = pl.program_id(0)

    @pl.when(nv_ref[blk] > 0)
    def _():
        x = x_ref[...].astype(BF16)
        h1 = _dot(x, w1_ref[0].astype(BF16))
        h3 = _dot(x, w3_ref[0].astype(BF16))
        hh = h1 * jax.nn.sigmoid(h1) * h3
        o_ref[...] = _dot(hh, w2_ref[0].astype(BF16))

    @pl.when(nv_ref[blk] == 0)
    def _():
        o_ref[...] = jnp.zeros_like(o_ref)


def moe_experts(xs, blk_e, blk_n, w1, w3, w2):
    n_pad, d_ = xs.shape
    n_blocks = n_pad // MOE_ROWS
    de = w1.shape[2]
    grid_spec = pltpu.PrefetchScalarGridSpec(
        num_scalar_prefetch=2,
        grid=(n_blocks,),
        in_specs=[pl.BlockSpec((MOE_ROWS, d_), lambda i, be, nv: (i, 0)),
                  pl.BlockSpec((1, d_, de), lambda i, be, nv: (be[i], 0, 0)),
                  pl.BlockSpec((1, d_, de), lambda i, be, nv: (be[i], 0, 0)),
                  pl.BlockSpec((1, de, d_), lambda i, be, nv: (be[i], 0, 0))],
        out_specs=pl.BlockSpec((MOE_ROWS, d_), lambda i, be, nv: (i, 0)),
    )
    return pl.pallas_call(
        _moe_kernel,
        grid_spec=grid_spec,
        out_shape=jax.ShapeDtypeStruct((n_pad, d_), F32),
        compiler_params=pltpu.CompilerParams(vmem_limit_bytes=VMEM_LIMIT),
        name="moe_experts",
    )(blk_e, blk_n, xs, w1, w3, w2)


def grid_shift(z):
    b_, l_, ch = z.shape
    rows = l_ // GRID_W
    zg = z.reshape(b_, rows, GRID_W, ch // 4, 4)
    left = jnp.pad(zg[:, :, :-1, :, 0], ((0, 0), (0, 0), (1, 0), (0, 0)))
    right = jnp.pad(zg[:, :, 1:, :, 1], ((0, 0), (0, 0), (0, 1), (0, 0)))
    up = jnp.pad(zg[:, :-1, :, :, 2], ((0, 0), (1, 0), (0, 0), (0, 0)))
    down = jnp.pad(zg[:, 1:, :, :, 3], ((0, 0), (0, 1), (0, 0), (0, 0)))
    return jnp.stack([left, right, up, down], -1).reshape(b_, l_, ch)


def seq_shift(z):
    b_, l_, ch = z.shape
    zs = z.reshape(b_, l_, ch // 2, 2)
    prev = jnp.pad(zs[:, :-1, :, 0], ((0, 0), (1, 0), (0, 0)))
    nxt = jnp.pad(zs[:, 1:, :, 1], ((0, 0), (0, 1), (0, 0)))
    return jnp.stack([prev, nxt], -1).reshape(b_, l_, ch)


def _heads(t):
    return t.reshape(t.shape[:-1] + (t.shape[-1] // HEAD, HEAD))


def token_mix(h, shift_fn, wts, states, want_mix):
    b_, l_, d_ = h.shape
    m_ = b_ * l_
    hf = h.reshape(m_, d_)
    tm = min(512, m_)
    main = pmm(hf, wts["w_main"], tm, 1024)
    lora = pmm(hf, wts["w_lora"], tm, wts["w_lora"].shape[1])
    sec = lambda j: main[:, j * d_:(j + 1) * d_].reshape(b_, l_, d_)
    mu = wts["mu_a"]
    mix = lambda z, m: z + m * (shift_fn(z) - z)
    r, k, v = mix(sec(0), mu[:d_]), mix(sec(1), mu[d_:2 * d_]), mix(sec(2), mu[2 * d_:3 * d_])
    lo = 3 * d_
    lz = mix(lora[:, :LORA_DECAY + LORA_AAA + LORA_GATE].reshape(b_, l_, -1), mu[lo:])
    wd, ad, gd = lz[..., :LORA_DECAY], lz[..., LORA_DECAY:LORA_DECAY + LORA_AAA], lz[..., LORA_DECAY + LORA_AAA:]

    g = pmm(jax.nn.sigmoid(gd).reshape(m_, -1), wts["g2"], tm, d_).reshape(b_, l_, d_)
    tw = jnp.tanh(wd).reshape(m_, -1)
    adf = ad.reshape(m_, -1)
    lw, aa, kmod = [], [], []
    for e in range(2):
        wl = -jax.nn.softplus(-(wts["w0"][e] + pmm(tw, wts["w2"][e], tm, d_).reshape(b_, l_, d_))) - 0.5
        lw.append(-jnp.exp(wl))
        a_e = jax.nn.sigmoid(wts["a0"][e] + pmm(adf, wts["a2"][e], tm, d_).reshape(b_, l_, d_))
        aa.append(a_e)
        kmod.append(k * (1.0 + (a_e - 1.0) * wts["k_a"]))
    lw, aa, kmod = jnp.stack(lw), jnp.stack(aa), jnp.stack(kmod)
    kk = _heads(k * wts["k_k"])
    kk = (kk * lax.rsqrt(jnp.sum(jnp.square(kk), -1, keepdims=True) + 1e-12)).reshape(b_, l_, d_)
    s0, h0 = states
    y2, s_new = wkv7(r, v, kk, lw, kmod, aa, s0)
    new_states = [s_new, None]
    y = _heads(y2[0] + y2[1])
    ymu = jnp.mean(y, -1, keepdims=True)
    yvar = jnp.mean(jnp.square(y - ymu), -1, keepdims=True)
    yn = ((y - ymu) * lax.rsqrt(yvar + GN_EPS)).reshape(b_, l_, d_) * wts["gn_g"] + wts["gn_b"]
    bonus = jnp.sum(_heads(r * (kmod[0] + kmod[1]) * wts["r_k"]), -1, keepdims=True) * _heads(v)
    y_a = (yn + bonus.reshape(b_, l_, d_)) * g

    xb, gb = sec(3), sec(4)
    xp = jnp.pad(xb, ((0, 0), (CONV_W // 2, CONV_W // 2), (0, 0)))
    xc = wts["conv_b"] + sum(xp[:, i:i + l_] * wts["conv_w"][i] for i in range(CONV_W))
    gates = block_gates(xc.reshape(m_, d_), [wts["lru_wa"][0], wts["lru_wa"][1], wts["lru_wx"][0], wts["lru_wx"][1]], tm)
    a_l, u_l = [], []
    for e in range(2):
        rg = jax.nn.sigmoid(gates[e].reshape(b_, l_, d_) + wts["lru_ba"][e])
        ig = jax.nn.sigmoid(gates[2 + e].reshape(b_, l_, d_) + wts["lru_bx"][e])
        log_a = -LRU_C * rg * jax.nn.softplus(-wts["lru_lam"][e])
        a_l.append(jnp.exp(log_a))
        u_l.append(jnp.sqrt(-jnp.expm1(2.0 * log_a)) * ig * xc)
    hh, h_new = linear_scan2(jnp.stack(a_l), jnp.stack(u_l), h0)
    new_states[1] = h_new
    if not want_mix:
        return None, new_states
    y_b = (hh[0] + hh[1]) * jax.nn.gelu(gb)

    ga, gbm = sec(5), sec(6)
    pa_ = pmm(y_a.reshape(m_, d_), wts["p_a"], tm, d_).reshape(b_, l_, d_)
    pb_ = pmm(y_b.reshape(m_, d_), wts["p_b"], tm, d_).reshape(b_, l_, d_)
    mm_ = jax.nn.sigmoid(ga) * pa_ + jax.nn.sigmoid(gbm) * pb_
    return pmm(mm_.reshape(m_, d_), wts["w_o"], tm, d_).reshape(b_, l_, d_), new_states


def hier_moe(h, rw, rb, w1, w3, w2):
    t_, d_ = h.shape
    logits = pmm(h, rw, min(512, t_), rw.shape[1], 2, 2)[:, :N_GROUPS + N_EXPERTS] + rb
    glog = logits[:, :N_GROUPS]
    gsel = jnp.argmax(glog, -1)
    p_g = jnp.take_along_axis(jax.nn.softmax(glog, -1), gsel[:, None], -1)
    elog = logits[:, N_GROUPS:].reshape(-1, N_GROUPS, EXPERTS_PER_GROUP)
    elog_g = jnp.take_along_axis(elog, gsel[:, None, None], 1)[:, 0]
    top_val, top_idx = lax.top_k(elog_g, TOP_K)
    wts = (jax.nn.softmax(top_val, -1) * p_g).reshape(-1)
    eid = (gsel[:, None] * EXPERTS_PER_GROUP + top_idx).reshape(-1).astype(jnp.int32)

    n_assign = eid.shape[0]
    order = jnp.argsort(eid)
    e_sorted = eid[order]
    counts = jnp.bincount(eid, length=N_EXPERTS)
    padded = (counts + MOE_ROWS - 1) // MOE_ROWS * MOE_ROWS
    pad_end = jnp.cumsum(padded)
    pad_start = pad_end - padded
    start = jnp.cumsum(counts) - counts
    dest_sorted = (pad_start[e_sorted] + jnp.arange(n_assign) - start[e_sorted]).astype(jnp.int32)
    n_blocks = -(-n_assign // MOE_ROWS) + N_EXPERTS
    n_pad = n_blocks * MOE_ROWS
    buf_tok = jnp.zeros((n_pad,), jnp.int32).at[dest_sorted].set((order // TOP_K).astype(jnp.int32))
    dest = jnp.zeros((n_assign,), jnp.int32).at[order].set(dest_sorted)
    blk_lo = jnp.arange(n_blocks) * MOE_ROWS
    blk_e = jnp.minimum(jnp.searchsorted(pad_end, blk_lo, side="right"), N_EXPERTS - 1).astype(jnp.int32)
    blk_n = jnp.clip(pad_start[blk_e] + counts[blk_e] - blk_lo, 0, MOE_ROWS).astype(jnp.int32)
    blk_n = jnp.where(blk_lo < pad_end[-1], blk_n, 0)

    xs = h[buf_tok]
    ys = moe_experts(xs, blk_e, blk_n, w1, w3, w2)
    yw = ys[dest] * wts[:, None]
    return yw.reshape(t_, TOP_K, d_).sum(1)


def kernel(x, c, ctx, c_ctx, w_ada, b_ada, w_in, mu_a, w0, w2, a0, a2, g2, k_k, k_a, r_k, gn_g, gn_b, conv_w, conv_b, lru_wa, lru_ba, lru_wx, lru_bx, lru_lam, p_a, p_b, w_o, ln1_g, ln1_b, router_g, router_g_b, router_e, router_e_b, e_w1, e_w3, e_w2, ln2_g, ln2_b):
    b_, l_, d_ = x.shape
    lc = ctx.shape[1]
    heads = d_ // HEAD
    l = 0
    cc = jnp.concatenate([c, c_ctx[None]], 0)
    cc = jnp.pad(jax.nn.silu(cc), ((0, 8 - cc.shape[0]), (0, 0)))
    mod = pmm(cc, w_ada[l], 8, 1024, 2, 2)[:b_ + 1] + b_ada[l]
    sh1, sc1, gt1, sh2, sc2, gt2 = [m[:b_, None, :] for m in jnp.split(mod, 6, axis=-1)]
    csh1, csc1 = [jnp.broadcast_to(m[b_:, None, :], (b_, 1, d_)) for m in jnp.split(mod, 6, axis=-1)[:2]]

    a_slab = 3 * d_ + LORA_DECAY + LORA_AAA + LORA_GATE
    wi = w_in[l]
    n_lora = a_slab - 3 * d_
    lora_pad = -(-n_lora // 128) * 128
    wts = dict(
        w_main=jnp.concatenate([wi[:, :3 * d_], wi[:, a_slab:]], 1).astype(BF16),
        w_lora=jnp.pad(wi[:, 3 * d_:a_slab], ((0, 0), (0, lora_pad - n_lora))).astype(BF16),
        mu_a=mu_a[l], w0=w0[l], w2=w2[l], a0=a0[l], a2=a2[l], g2=g2[l], k_k=k_k[l], k_a=k_a[l],
        r_k=r_k[l].reshape(-1), gn_g=gn_g[l], gn_b=gn_b[l], conv_w=conv_w[l], conv_b=conv_b[l],
        lru_wa=lru_wa[l], lru_ba=lru_ba[l], lru_wx=lru_wx[l], lru_bx=lru_bx[l], lru_lam=lru_lam[l],
        p_a=p_a[l], p_b=p_b[l], w_o=w_o[l])

    s0 = jnp.zeros((2, b_, heads, HEAD, HEAD), F32)
    h0 = jnp.zeros((2, b_, 1, d_), F32)
    _, hc = ln_mod(ctx, csc1, csh1, min(512, lc))
    _, ctx_states = token_mix(hc, seq_shift, wts, (s0, h0), want_mix=False)
    x0, hx = ln_mod(x, sc1, sh1, 512)
    mix_x, _ = token_mix(hx, grid_shift, wts, ctx_states, want_mix=True)
    x1, h2 = res_ln(x0, mix_x, gt1, ln1_g[l], ln1_b[l], sc2, sh2, 512)

    rw = jnp.pad(jnp.concatenate([router_g[l], router_e[l]], 1), ((0, 0), (0, 128 - N_GROUPS - N_EXPERTS)))
    rb = jnp.concatenate([router_g_b[l], router_e_b[l]])
    moe = hier_moe(h2.reshape(b_ * l_, d_), rw, rb, e_w1[l], e_w3[l], e_w2[l]).reshape(b_, l_, d_)
    zero = jnp.zeros_like(sc2)
    out, _ = res_ln(x1, moe, gt2, ln2_g[l], ln2_b[l], zero, zero, 512)
    return out
```

```python
import functools

import jax
import jax.numpy as jnp
from jax import lax
from jax.experimental import pallas as pl
from jax.experimental.pallas import tpu as pltpu

F32 = jnp.float32
BF16 = jnp.bfloat16

GRID_W = 64
HEAD = 64
LORA_DECAY = 64
LORA_AAA = 64
LORA_GATE = 160
LORA_PAD = 384
GN_EPS = 64e-5
LN_EPS = 1e-5
CONV_W = 5
LRU_C = 8.0
N_GROUPS = 4
EXPERTS_PER_GROUP = 8
N_EXPERTS = N_GROUPS * EXPERTS_PER_GROUP
TOP_K = 2
DEPTH = 1
ALPHA = (2 * DEPTH) ** 0.25

WKV_CHUNK = 64
MOE_ROWS = 256
GATE_GROUP = 4 * HEAD
HALO = 8
VMEM_LIMIT = 56 * 1024 * 1024

_NN = (((1,), (0,)), ((), ()))
_NT = (((1,), (1,)), ((), ()))
_TN = (((0,), (0,)), ((), ()))


def _split(x, n):
    if x.dtype == BF16:
        return [x]
    parts, rest = [], x.astype(F32)
    for i in range(n):
        p = rest.astype(BF16)
        parts.append(p)
        if i + 1 < n:
            rest = rest - p.astype(F32)
    return parts


def _dot(a, b, dims=_NN, pa=1, pb=1):
    ap, bp = _split(a, pa), _split(b, pb)
    order = max(len(ap), len(bp))
    acc = None
    for i, x in enumerate(ap):
        for j, y in enumerate(bp):
            if i + j < order:
                t = lax.dot_general(x, y, dims, preferred_element_type=F32)
                acc = t if acc is None else acc + t
    return acc


def _ln(x):
    mu = jnp.mean(x, -1, keepdims=True)
    xc = x - mu
    var = jnp.mean(xc * xc, -1, keepdims=True)
    return xc * lax.rsqrt(var + LN_EPS)


def _sigmoid(x):
    return 1.0 / (1.0 + jnp.exp(-x))


def _softplus(x):
    return jnp.maximum(x, 0.0) + jnp.log(1.0 + jnp.exp(-jnp.abs(x)))


def _head_sums(x):
    pw = 2 * HEAD
    row = lax.broadcasted_iota(jnp.int32, (pw, pw), 0)
    col = lax.broadcasted_iota(jnp.int32, (pw, pw), 1)
    ones = jnp.where((row >= HEAD) == (col >= HEAD), 1.0, 0.0).astype(BF16)
    return jnp.concatenate([_dot(x[:, p:p + pw], ones, _NN, 2, 1) for p in range(0, x.shape[1], pw)], axis=1)


def _inproj_kernel(x_ref, sc_ref, sh_ref, wm_ref, wl_ref, main_ref, lora_ref, h_scr, *, nm):
    j = pl.program_id(2)

    @pl.when(j == 0)
    def _():
        h_scr[...] = (_ln(x_ref[0]) * (1.0 + sc_ref[0]) + sh_ref[0]).astype(BF16)

    @pl.when(j < nm)
    def _():
        main_ref[0] = _dot(h_scr[...], wm_ref[...])

    @pl.when(j == nm)
    def _():
        lora_ref[0] = _dot(h_scr[...], wl_ref[...])


def inproj(x, sc, sh, w_main, w_lora, tm):
    b_, l_, d_ = x.shape
    nm = w_main.shape[1] // d_
    row = pl.BlockSpec((1, tm, d_), lambda b, i, j: (b, i, 0))
    vec = pl.BlockSpec((1, 1, d_), lambda b, i, j: (b, 0, 0))
    return pl.pallas_call(
        functools.partial(_inproj_kernel, nm=nm),
        grid=(b_, l_ // tm, nm + 1),
        in_specs=[row, vec, vec,
                  pl.BlockSpec((d_, d_), lambda b, i, j: (0, jnp.minimum(j, nm - 1))),
                  pl.BlockSpec((d_, LORA_PAD), lambda b, i, j: (0, 0))],
        out_specs=[pl.BlockSpec((1, tm, d_), lambda b, i, j: (b, i, jnp.minimum(j, nm - 1))),
                   pl.BlockSpec((1, tm, LORA_PAD), lambda b, i, j: (b, i, 0))],
        out_shape=[jax.ShapeDtypeStruct((b_, l_, nm * d_), F32), jax.ShapeDtypeStruct((b_, l_, LORA_PAD), F32)],
        scratch_shapes=[pltpu.VMEM((tm, d_), BF16)],
        compiler_params=pltpu.CompilerParams(vmem_limit_bytes=VMEM_LIMIT),
        name="inproj",
    )(x, sc, sh, w_main, w_lora)


def _mix_kernel(cur_ref, prev_ref, next_ref, lcur_ref, lprev_ref, lnext_ref, mu_ref, mul_ref, kk_ref,
                mixed_ref, lz_ref, *, grid_mode):
    i = pl.program_id(1)
    n_i = pl.num_programs(1)
    tm = cur_ref.shape[1]
    d_ = kk_ref.shape[1]

    def shifted(cur, prev, nxt):
        rows, ch = cur.shape
        rowi = lax.broadcasted_iota(jnp.int32, (rows, ch), 0)
        lane = lax.broadcasted_iota(jnp.int32, (rows, ch), 1)
        if grid_mode:
            prev = jnp.where(i > 0, prev, 0.0)
            nxt = jnp.where(i < n_i - 1, nxt, 0.0)
            up = jnp.concatenate([prev, cur[:rows - GRID_W]], axis=0)
            down = jnp.concatenate([cur[GRID_W:], nxt], axis=0)
            col = rowi % GRID_W
            left = jnp.where(col == 0, 0.0, pltpu.roll(cur, 1, 0))
            right = jnp.where(col == GRID_W - 1, 0.0, pltpu.roll(cur, rows - 1, 0))
            l4 = lane % 4
            return jnp.where(l4 == 0, left, jnp.where(l4 == 1, right, jnp.where(l4 == 2, up, down)))
        before = jnp.where(rowi == 0, 0.0, pltpu.roll(cur, 1, 0))
        after = jnp.where(rowi == rows - 1, 0.0, pltpu.roll(cur, rows - 1, 0))
        return jnp.where(lane % 2 == 0, before, after)

    def mixed(cur, prev, nxt, mu):
        return cur + mu * (shifted(cur, prev, nxt) - cur)

    for s in range(3):
        sl = slice(s * d_, (s + 1) * d_)
        z = mixed(cur_ref[0, :, sl], prev_ref[0, :, sl], next_ref[0, :, sl], mu_ref[:, sl])
        mixed_ref[0, :, sl] = z.astype(BF16)
        if s == 1:
            kq = z * kk_ref[...]
            kq = kq * lax.rsqrt(_head_sums(kq * kq) + 1e-12)
            mixed_ref[0, :, 3 * d_:4 * d_] = kq.astype(BF16)
    lz = mixed(lcur_ref[0], lprev_ref[0], lnext_ref[0], mul_ref[...])
    wa = lz[:, :LORA_DECAY + LORA_AAA]
    lane = lax.broadcasted_iota(jnp.int32, wa.shape, 1)
    lz_ref[0, :, :LORA_DECAY + LORA_AAA] = jnp.where(lane < LORA_DECAY, jnp.tanh(wa), wa).astype(BF16)
    lz_ref[0, :, LORA_DECAY + LORA_AAA:] = _sigmoid(lz[:, LORA_DECAY + LORA_AAA:]).astype(BF16)


def rwkv_mix(main, lora, mu, mu_l, k_k, grid_mode, tm):
    b_, l_, _ = main.shape
    d_ = k_k.shape[-1]
    if grid_mode:
        assert tm % GRID_W == 0 and l_ % tm == 0
        halo, per = GRID_W, tm // GRID_W
    else:
        assert tm == l_
        halo, per = 8, tm // 8
    n_h = l_ // halo
    cur = lambda w: pl.BlockSpec((1, tm, w), lambda b, i: (b, i, 0))
    prv = lambda w: pl.BlockSpec((1, halo, w), lambda b, i: (b, jnp.maximum(i * per - 1, 0), 0))
    nxt = lambda w: pl.BlockSpec((1, halo, w), lambda b, i: (b, jnp.minimum((i + 1) * per, n_h - 1), 0))
    par = lambda w: pl.BlockSpec((1, w), lambda b, i: (0, 0))
    return pl.pallas_call(
        functools.partial(_mix_kernel, grid_mode=grid_mode),
        grid=(b_, l_ // tm),
        in_specs=[cur(3 * d_), prv(3 * d_), nxt(3 * d_), cur(LORA_PAD), prv(LORA_PAD), nxt(LORA_PAD),
                  par(3 * d_), par(LORA_PAD), par(d_)],
        out_specs=[cur(4 * d_), cur(LORA_PAD)],
        out_shape=[jax.ShapeDtypeStruct((b_, l_, 4 * d_), BF16), jax.ShapeDtypeStruct((b_, l_, LORA_PAD), BF16)],
        compiler_params=pltpu.CompilerParams(vmem_limit_bytes=VMEM_LIMIT),
        name="rwkv_mix",
    )(main, main, main, lora, lora, lora, mu, mu_l, k_k)


def _wkv_kernel(r_ref, k_ref, v_ref, kk_ref, lz_ref, wlo_ref, wa0_ref, ka_ref, s0_ref, y_ref, sT_ref, s_scr):
    d = pl.program_id(0)
    c = pl.program_id(2)
    n_c = pl.num_programs(2)
    cs = r_ref.shape[1]
    d_ = r_ref.shape[2]
    pairs = s_scr.shape[0]
    pw = 2 * HEAD

    @pl.when(c == 0)
    def _():
        s_scr[...] = s0_ref[0, 0]

    sgn = 1 - 2 * d
    row = lax.broadcasted_iota(jnp.int32, (cs, cs), 0)
    col = lax.broadcasted_iota(jnp.int32, (cs, cs), 1)
    tri = jnp.where((row - col) * sgn >= 0, 1.0, 0.0).astype(BF16)
    row2 = lax.broadcasted_iota(jnp.int32, (2 * cs, pw), 0)
    col2 = lax.broadcasted_iota(jnp.int32, (2 * cs, pw), 1)
    tdiff = (row2 % cs - col2 % cs) * sgn
    amask = (tdiff > 0) | ((row2 >= cs) & (tdiff == 0))
    same_head = (row2 >= cs) == (col2 >= HEAD)
    eye = jnp.where(row2 == col2, 1.0, 0.0)
    lane = lax.broadcasted_iota(jnp.int32, (cs, pw), 1)
    m0, m1 = lane < HEAD, lane >= HEAD
    lane2 = lax.broadcasted_iota(jnp.int32, (2 * cs, pw), 1)
    mm0, mm1 = lane2 < HEAD, lane2 >= HEAD

    z = _dot(lz_ref[0, :, :LORA_DECAY + LORA_AAA], wlo_ref[0]) + wa0_ref[0]
    lw = -jnp.exp(-_softplus(-z[:, :d_]) - 0.5)
    a = _sigmoid(z[:, d_:])
    r, kk = r_ref[0].astype(F32), kk_ref[0].astype(F32)
    k = k_ref[0].astype(F32) * (1.0 + (a - 1.0) * ka_ref[...])

    cum = _dot(tri, lw, _NN, 1, 3)
    tot = jnp.sum(lw, axis=0, keepdims=True)
    b = kk * a
    e_neg = jnp.exp(-cum)
    e_end = jnp.exp(tot - cum)
    rt = (r * jnp.exp(cum)).astype(BF16)
    at = (-kk * jnp.exp(cum - lw)).astype(BF16)
    bt = (b * e_neg).astype(BF16)
    kt = (k * e_neg).astype(BF16)
    bd = (b * e_end).astype(BF16)
    kd = (k * e_end).astype(BF16)
    vb = v_ref[0]
    p_end = jnp.exp(tot)
    zero = jnp.zeros((cs, pw), BF16)
    cat = lambda x, y: jnp.concatenate([x, y], axis=0)
    dot = lambda x, y, dims=_NN: lax.dot_general(x, y, dims, preferred_element_type=F32)
    sel = lambda m, x: jnp.where(m, x, jnp.zeros_like(x))

    ar, aa0, aa1, xs, ts, nbs = [], [], [], [], [], []
    for p in range(pairs):
        sl = slice(p * pw, (p + 1) * pw)
        ar_p = cat(at[:, sl], rt[:, sl])
        q0 = sel(amask, dot(sel(mm0, ar_p), cat(bt[:, sl], kt[:, sl]), _NT))
        q1 = sel(amask, dot(sel(mm1, ar_p), cat(kt[:, sl], bt[:, sl]), _NT))
        nil = cat(sel(m0, q0[:cs]), sel(m1, q1[:cs]))
        ar.append(ar_p); aa0.append(q0.astype(BF16)); aa1.append(q1.astype(BF16))
        ts.append(eye + nil); nbs.append(nil.astype(BF16))
    for p in range(pairs):
        v_p = vb[:, p * pw:(p + 1) * pw]
        xs.append(dot(aa0[p][:cs], cat(zero, sel(m0, v_p))) + dot(aa1[p][:cs], cat(sel(m1, v_p), zero)))
    for i in range(max(cs.bit_length() - 2, 0)):
        nbs = [dot(nb, nb).astype(BF16) for nb in nbs]
        ts = [t + dot(t.astype(BF16), nb) for t, nb in zip(ts, nbs)]
    ts = [t.astype(BF16) for t in ts]

    st = [s_scr[p] for p in range(pairs)]
    as_ = [dot(ar[p], st[p].astype(BF16)) for p in range(pairs)]
    us = []
    for p in range(pairs):
        rhs = (as_[p][:cs] + xs[p]).astype(BF16)
        uu = dot(ts[p], cat(sel(m0, rhs), sel(m1, rhs)))
        us.append((uu[:cs] + uu[cs:]).astype(BF16))
    for p in range(pairs):
        sl = slice(p * pw, (p + 1) * pw)
        u_p, v_p = us[p], vb[:, sl]
        y = (as_[p][cs:] + dot(aa0[p][cs:], cat(sel(m0, u_p), sel(m0, v_p)))
             + dot(aa1[p][cs:], cat(sel(m1, v_p), sel(m1, u_p))))
        y_ref[0, 0, :, sl] = y
        p_col = jnp.broadcast_to(p_end[:, sl], (pw, pw)).T
        upd = dot(cat(bd[:, sl], kd[:, sl]), cat(u_p, v_p), _TN)
        s_scr[p] = p_col * st[p] + sel(same_head, upd)

    @pl.when(c == n_c - 1)
    def _():
        sT_ref[0, 0] = s_scr[...]


def _pair_states(s):
    st = jnp.swapaxes(s, -1, -2)
    st = st.reshape(s.shape[:-3] + (s.shape[-3] // 2, 2, HEAD, HEAD))
    eye = jnp.eye(2, dtype=s.dtype)
    out = jnp.einsum("...pakv,ab->...pakbv", st, eye)
    return out.reshape(s.shape[:-3] + (s.shape[-3] // 2, 2 * HEAD, 2 * HEAD))


def wkv7(mixed, lz, w_lo, wa0, k_a, s0p):
    b_, l_, d4 = mixed.shape
    d_ = d4 // 4
    pairs, pw = d_ // (2 * HEAD), 2 * HEAD
    cs = WKV_CHUNK
    assert cs == HEAD and l_ % cs == 0
    n_c = l_ // cs

    def tmap(d, b, c):
        return c + d * (n_c - 1 - 2 * c)

    sec = lambda j: pl.BlockSpec((1, cs, d_), lambda d, b, c: (b, tmap(d, b, c), j))
    state = pl.BlockSpec((1, 1, pairs, pw, pw), lambda d, b, c: (d, b, 0, 0, 0))
    return pl.pallas_call(
        _wkv_kernel,
        grid=(2, b_, n_c),
        in_specs=[sec(0), sec(1), sec(2), sec(3),
                  pl.BlockSpec((1, cs, LORA_PAD), lambda d, b, c: (b, tmap(d, b, c), 0)),
                  pl.BlockSpec((1, pw, 2 * d_), lambda d, b, c: (d, 0, 0)),
                  pl.BlockSpec((1, 1, 2 * d_), lambda d, b, c: (d, 0, 0)),
                  pl.BlockSpec((1, d_), lambda d, b, c: (0, 0)),
                  state],
        out_specs=[pl.BlockSpec((1, 1, cs, d_), lambda d, b, c: (d, b, tmap(d, b, c), 0)), state],
        out_shape=[jax.ShapeDtypeStruct((2, b_, l_, d_), F32),
                   jax.ShapeDtypeStruct((2, b_, pairs, pw, pw), F32)],
        scratch_shapes=[pltpu.VMEM((pairs, pw, pw), F32)],
        compiler_params=pltpu.CompilerParams(
            dimension_semantics=("arbitrary", "arbitrary", "arbitrary"), vmem_limit_bytes=VMEM_LIMIT),
        name="wkv7",
    )(mixed, mixed, mixed, mixed, lz, w_lo, wa0, k_a, s0p)


def _lru_kernel(cur_ref, prev_ref, next_ref, cw_ref, cb_ref, wg_ref, bg_ref, cl_ref, h0_ref,
                h_ref, hT_ref, a_scr, u_scr, h_scr):
    d = pl.program_id(0)
    c = pl.program_id(2)
    n_c = pl.num_programs(2)
    ti = c + d * (n_c - 1 - 2 * c)
    tm, d_ = a_scr.shape
    gw = GATE_GROUP

    @pl.when(c == 0)
    def _():
        h_scr[...] = h0_ref[0, 0]

    prev = jnp.where(ti > 0, prev_ref[0], 0.0)
    nxt = jnp.where(ti < n_c - 1, next_ref[0], 0.0)
    ext = jnp.concatenate([prev, cur_ref[0], nxt], axis=0)
    xc = cb_ref[...]
    for j in range(CONV_W):
        o = HALO - CONV_W // 2 + j
        xc = xc + cw_ref[j:j + 1, :] * ext[o:o + tm]
    for g in range(d_ // gw):
        sl = slice(g * gw, (g + 1) * gw)
        z = _dot(xc[:, sl], wg_ref[0, g])
        rg = _sigmoid(z[:, :gw] + bg_ref[0, :, sl])
        ig = _sigmoid(z[:, gw:] + bg_ref[0, :, d_ + g * gw:d_ + (g + 1) * gw])
        log_a = cl_ref[0, :, sl] * rg
        a_scr[:, sl] = jnp.exp(log_a)
        u_scr[:, sl] = jnp.sqrt(1.0 - jnp.exp(2.0 * log_a)) * ig * xc[:, sl]

    rowid = lax.broadcasted_iota(jnp.int32, (8, d_), 0)

    def tile_scan(i, h, rev):
        t8 = (tm // 8 - 1 - i) if rev else i
        r0 = pl.multiple_of(t8 * 8, 8)
        a8 = a_scr[pl.ds(r0, 8), :]
        u8 = u_scr[pl.ds(r0, 8), :]
        for s in (1, 2, 4):
            ok = (rowid < 8 - s) if rev else (rowid >= s)
            sh = (8 - s) if rev else s
            a_sh = jnp.where(ok, pltpu.roll(a8, sh, 0), 1.0)
            u_sh = jnp.where(ok, pltpu.roll(u8, sh, 0), 0.0)
            u8 = a8 * u_sh + u8
            a8 = a8 * a_sh
        h8 = u8 + a8 * h
        h_ref[0, 0, pl.ds(r0, 8), :] = h8
        return h8[0:1] if rev else h8[7:8]

    @pl.when(d == 0)
    def _():
        h_scr[...] = lax.fori_loop(0, tm // 8, lambda i, h: tile_scan(i, h, False), h_scr[...], unroll=2)

    @pl.when(d == 1)
    def _():
        h_scr[...] = lax.fori_loop(0, tm // 8, lambda i, h: tile_scan(i, h, True), h_scr[...], unroll=2)

    @pl.when(c == n_c - 1)
    def _():
        hT_ref[0, 0] = h_scr[...]


def rglru(main, xb_col, conv_w, conv_b, w_gate, b_gate, c_lam, h0, tm):
    b_, l_, _ = main.shape
    d_ = conv_w.shape[1]
    tm = min(tm, l_)
    n_c = l_ // tm
    per = tm // HALO
    n_h = l_ // HALO
    tmap = lambda d, c: c + d * (n_c - 1 - 2 * c)
    dirp = lambda shape: pl.BlockSpec((1,) + shape, lambda d, b, c: (d,) + (0,) * len(shape))
    st = pl.BlockSpec((1, 1, 1, d_), lambda d, b, c: (d, b, 0, 0))
    return pl.pallas_call(
        _lru_kernel,
        grid=(2, b_, n_c),
        in_specs=[pl.BlockSpec((1, tm, d_), lambda d, b, c: (b, tmap(d, c), xb_col)),
                  pl.BlockSpec((1, HALO, d_), lambda d, b, c: (b, jnp.maximum(tmap(d, c) * per - 1, 0), xb_col)),
                  pl.BlockSpec((1, HALO, d_), lambda d, b, c: (b, jnp.minimum((tmap(d, c) + 1) * per, n_h - 1), xb_col)),
                  pl.BlockSpec((CONV_W, d_), lambda d, b, c: (0, 0)),
                  pl.BlockSpec((1, d_), lambda d, b, c: (0, 0)),
                  dirp(w_gate.shape[1:]), dirp((1, 2 * d_)), dirp((1, d_)), st],
        out_specs=[pl.BlockSpec((1, 1, tm, d_), lambda d, b, c: (d, b, tmap(d, c), 0)), st],
        out_shape=[jax.ShapeDtypeStruct((2, b_, l_, d_), F32), jax.ShapeDtypeStruct((2, b_, 1, d_), F32)],
        scratch_shapes=[pltpu.VMEM((tm, d_), F32), pltpu.VMEM((tm, d_), F32), pltpu.VMEM((1, d_), F32)],
        compiler_params=pltpu.CompilerParams(
            dimension_semantics=("arbitrary", "arbitrary", "arbitrary"), vmem_limit_bytes=VMEM_LIMIT),
        name="rglru",
    )(main, main, main, conv_w, conv_b, w_gate, b_gate, c_lam, h0)


def _post_kernel(yf_ref, yb_ref, r_ref, k_ref, v_ref, lz_ref, hf_ref, hb_ref, gb_ref, ga_ref, gm_ref, x_ref,
                 gt_ref, sc_ref, sh_ref, wa2_ref, a0_ref, g2_ref, ka_ref, rk_ref, gg_ref, gnb_ref,
                 pa_ref, pb_ref, wo_ref, lg_ref, lb_ref, rwh_ref, rwl_ref, rb_ref,
                 x1_ref, h2_ref, lg_out_ref):
    d_ = x_ref.shape[2]
    y = yf_ref[0, 0] + yb_ref[0, 0]
    mu = _head_sums(y) * (1.0 / HEAD)
    yc = y - mu
    var = _head_sums(yc * yc) * (1.0 / HEAD)
    yn = yc * lax.rsqrt(var + GN_EPS) * gg_ref[...] + gnb_ref[...]
    lz = lz_ref[0]
    a2 = _sigmoid(_dot(lz[:, :LORA_DECAY + LORA_AAA], wa2_ref[...]) + a0_ref[...])
    r, k, v = r_ref[0].astype(F32), k_ref[0].astype(F32), v_ref[0].astype(F32)
    ksum = k * (2.0 + (a2[:, :d_] + a2[:, d_:] - 2.0) * ka_ref[...])
    bonus = _head_sums(r * ksum * rk_ref[...]) * v
    g = _dot(lz[:, LORA_DECAY + LORA_AAA:], g2_ref[...])
    y_a = ((yn + bonus) * g).astype(BF16)
    gb = gb_ref[0]
    gelu = 0.5 * gb * (1.0 + jnp.tanh(0.7978845608028654 * (gb + 0.044715 * gb * gb * gb)))
    y_b = ((hf_ref[0, 0] + hb_ref[0, 0]) * gelu).astype(BF16)
    m = _sigmoid(ga_ref[0]) * _dot(y_a, pa_ref[...]) + _sigmoid(gm_ref[0]) * _dot(y_b, pb_ref[...])
    mix = _dot(m, wo_ref[...])
    x1 = _ln(ALPHA * _ln(x_ref[0]) + gt_ref[0] * mix) * lg_ref[...] + lb_ref[...]
    x1_ref[0] = x1
    h2 = x1 * (1.0 + sc_ref[0]) + sh_ref[0]
    h2_ref[0] = h2.astype(BF16)
    hh, hl = _split(h2, 2)
    lg_out_ref[0] = (_dot(hh, rwh_ref[...]) + _dot(hl, rwh_ref[...]) + _dot(hh, rwl_ref[...])) + rb_ref[...]


def post_mix(y2, mixed, lz, hh, main, x, gt, sc, sh, params, tm):
    b_, l_, d_ = x.shape
    dsec = lambda e: pl.BlockSpec((1, 1, tm, d_), lambda b, i: (e, b, i, 0))
    col = lambda j: pl.BlockSpec((1, tm, d_), lambda b, i: (b, i, j))
    vec = pl.BlockSpec((1, 1, d_), lambda b, i: (b, 0, 0))
    full = lambda a: pl.BlockSpec(a.shape, lambda b, i: (0,) * a.ndim)
    return pl.pallas_call(
        _post_kernel,
        grid=(b_, l_ // tm),
        in_specs=[dsec(0), dsec(1), col(0), col(1), col(2),
                  pl.BlockSpec((1, tm, LORA_PAD), lambda b, i: (b, i, 0)),
                  dsec(0), dsec(1), col(4), col(5), col(6), col(0), vec, vec, vec] + [full(p) for p in params],
        out_specs=[col(0), col(0), pl.BlockSpec((1, tm, 128), lambda b, i: (b, i, 0))],
        out_shape=[jax.ShapeDtypeStruct((b_, l_, d_), F32), jax.ShapeDtypeStruct((b_, l_, d_), BF16),
                   jax.ShapeDtypeStruct((b_, l_, 128), F32)],
        compiler_params=pltpu.CompilerParams(vmem_limit_bytes=VMEM_LIMIT),
        name="post_mix",
    )(y2, y2, mixed, mixed, mixed, lz, hh, hh, main, main, main, x, gt, sc, sh, *params)


def _mm_kernel(x_ref, w_ref, o_ref, *, pa, pb):
    o_ref[...] = _dot(x_ref[...], w_ref[...], _NN, pa, pb)


def pmm(x, w, tm=512, tn=1024, pa=1, pb=1):
    m_, k_ = x.shape
    n_ = w.shape[1]
    tm, tn = min(tm, m_), min(tn, n_)
    assert m_ % tm == 0 and n_ % tn == 0, (x.shape, w.shape, tm, tn)
    return pl.pallas_call(
        functools.partial(_mm_kernel, pa=pa, pb=pb),
        grid=(m_ // tm, n_ // tn),
        in_specs=[pl.BlockSpec((tm, k_), lambda i, j: (i, 0)), pl.BlockSpec((k_, tn), lambda i, j: (0, j))],
        out_specs=pl.BlockSpec((tm, tn), lambda i, j: (i, j)),
        out_shape=jax.ShapeDtypeStruct((m_, n_), F32),
        compiler_params=pltpu.CompilerParams(vmem_limit_bytes=VMEM_LIMIT),
        name="pmm",
    )(x, w)


def _res_ln_kernel(x_ref, y_ref, gt_ref, g_ref, b_ref, o_ref):
    o_ref[0] = _ln(ALPHA * x_ref[0] + gt_ref[0] * y_ref[0]) * g_ref[...] + b_ref[...]


def res_ln(x, y, gt, g, b, tm):
    b_, l_, d_ = x.shape
    row = pl.BlockSpec((1, tm, d_), lambda bi, i: (bi, i, 0))
    vec = pl.BlockSpec((1, 1, d_), lambda bi, i: (bi, 0, 0))
    par = pl.BlockSpec((1, d_), lambda bi, i: (0, 0))
    return pl.pallas_call(
        _res_ln_kernel,
        grid=(b_, l_ // tm),
        in_specs=[row, row, vec, par, par],
        out_specs=row,
        out_shape=jax.ShapeDtypeStruct(x.shape, F32),
        name="res_ln",
    )(x, y, gt, g.reshape(1, d_), b.reshape(1, d_))


def _moe_kernel(be_ref, nv_ref, x_ref, w1_ref, w3_ref, w2_ref, o_ref):
    blk = pl.program_id(0)

    @pl.when(nv_ref[blk] > 0)
    def _():
        x = x_ref[...]
        h1 = _dot(x, w1_ref[0].astype(BF16))
        h3 = _dot(x, w3_ref[0].astype(BF16))
        hh = h1 * _sigmoid(h1) * h3
        o_ref[...] = _dot(hh, w2_ref[0].astype(BF16))

    @pl.when(nv_ref[blk] == 0)
    def _():
        o_ref[...] = jnp.zeros_like(o_ref)


def moe_experts(xs, blk_e, blk_n, w1, w3, w2):
    n_pad, d_ = xs.shape
    n_blocks = n_pad // MOE_ROWS
    de = w1.shape[2]
    grid_spec = pltpu.PrefetchScalarGridSpec(
        num_scalar_prefetch=2,
        grid=(n_blocks,),
        in_specs=[pl.BlockSpec((MOE_ROWS, d_), lambda i, be, nv: (i, 0)),
                  pl.BlockSpec((1, d_, de), lambda i, be, nv: (be[i], 0, 0)),
                  pl.BlockSpec((1, d_, de), lambda i, be, nv: (be[i], 0, 0)),
                  pl.BlockSpec((1, de, d_), lambda i, be, nv: (be[i], 0, 0))],
        out_specs=pl.BlockSpec((MOE_ROWS, d_), lambda i, be, nv: (i, 0)),
    )
    return pl.pallas_call(
        _moe_kernel,
        grid_spec=grid_spec,
        out_shape=jax.ShapeDtypeStruct((n_pad, d_), F32),
        compiler_params=pltpu.CompilerParams(vmem_limit_bytes=VMEM_LIMIT),
        name="moe_experts",
    )(blk_e, blk_n, xs, w1, w3, w2)


def hier_moe(h, logits, w1, w3, w2):
    t_, d_ = h.shape
    glog = logits[:, :N_GROUPS]
    gsel = jnp.argmax(glog, -1)
    p_g = jnp.take_along_axis(jax.nn.softmax(glog, -1), gsel[:, None], -1)
    elog = logits[:, N_GROUPS:N_GROUPS + N_EXPERTS].reshape(-1, N_GROUPS, EXPERTS_PER_GROUP)
    elog_g = jnp.take_along_axis(elog, gsel[:, None, None], 1)[:, 0]
    top_val, top_idx = lax.top_k(elog_g, TOP_K)
    wts = (jax.nn.softmax(top_val, -1) * p_g).reshape(-1)
    eid = (gsel[:, None] * EXPERTS_PER_GROUP + top_idx).reshape(-1).astype(jnp.int32)

    n_assign = eid.shape[0]
    order = jnp.argsort(eid)
    e_sorted = eid[order]
    counts = jnp.bincount(eid, length=N_EXPERTS)
    padded = (counts + MOE_ROWS - 1) // MOE_ROWS * MOE_ROWS
    pad_end = jnp.cumsum(padded)
    pad_start = pad_end - padded
    start = jnp.cumsum(counts) - counts
    dest_sorted = (pad_start[e_sorted] + jnp.arange(n_assign) - start[e_sorted]).astype(jnp.int32)
    n_blocks = -(-n_assign // MOE_ROWS) + N_EXPERTS
    n_pad = n_blocks * MOE_ROWS
    buf_tok = jnp.zeros((n_pad,), jnp.int32).at[dest_sorted].set((order // TOP_K).astype(jnp.int32))
    dest = jnp.zeros((n_assign,), jnp.int32).at[order].set(dest_sorted)
    blk_lo = jnp.arange(n_blocks) * MOE_ROWS
    blk_e = jnp.minimum(jnp.searchsorted(pad_end, blk_lo, side="right"), N_EXPERTS - 1).astype(jnp.int32)
    blk_n = jnp.clip(pad_start[blk_e] + counts[blk_e] - blk_lo, 0, MOE_ROWS).astype(jnp.int32)

    xs = h[buf_tok]
    ys = moe_experts(xs, blk_e, blk_n, w1, w3, w2)
    yw = ys[dest] * wts[:, None]
    return yw.reshape(t_, TOP_K, d_).sum(1)


def _block_diag_groups(w, grp):
    n = w.shape[0]
    wg = w.reshape(n // grp, grp, HEAD, HEAD)
    eye = jnp.eye(grp, dtype=w.dtype)
    return jnp.einsum("gaij,ab->gaibj", wg, eye).reshape(n // grp, grp * HEAD, grp * HEAD)


def token_scans(h_in, sc, sh, wts, grid_mode, states):
    b_, l_, d_ = h_in.shape
    main, lora = inproj(h_in, sc, sh, wts["w_main"], wts["w_lora"], min(512, l_))
    mixed, lz = rwkv_mix(main, lora, wts["mu"], wts["mu_l"], wts["k_k"], grid_mode, 256 if grid_mode else l_)
    y2, s_new = wkv7(mixed, lz, wts["w_lo"], wts["wa0"], wts["k_a"], states[0])
    hh, h_new = rglru(main, 3, wts["conv_w"], wts["conv_b"], wts["w_gate"], wts["b_gate"], wts["c_lam"], states[1], 512)
    return (main, mixed, lz, y2, hh), (s_new, h_new)


def kernel(x, c, ctx, c_ctx, w_ada, b_ada, w_in, mu_a, w0, w2, a0, a2, g2, k_k, k_a, r_k, gn_g, gn_b, conv_w, conv_b, lru_wa, lru_ba, lru_wx, lru_bx, lru_lam, p_a, p_b, w_o, ln1_g, ln1_b, router_g, router_g_b, router_e, router_e_b, e_w1, e_w3, e_w2, ln2_g, ln2_b):
    b_, l_, d_ = x.shape
    heads = d_ // HEAD
    l = 0
    row = lambda v: v.reshape(1, -1)
    cc = jnp.concatenate([c, c_ctx[None]], 0)
    cc = jnp.pad(jax.nn.silu(cc), ((0, 8 - cc.shape[0]), (0, 0)))
    mod = pmm(cc, w_ada[l], 8, 1024, 2, 2)[:b_ + 1] + b_ada[l]
    mods = jnp.split(mod, 6, axis=-1)
    sh1, sc1, gt1, sh2, sc2, gt2 = [m[:b_, None, :] for m in mods]
    csh1, csc1 = [jnp.broadcast_to(m[b_:, None, :], (b_, 1, d_)) for m in mods[:2]]

    a_slab = 3 * d_ + LORA_DECAY + LORA_AAA + LORA_GATE
    n_lora = a_slab - 3 * d_
    wi, mu = w_in[l], mu_a[l]
    zeros_lo = jnp.zeros((LORA_DECAY, d_), F32)
    w_lo = jnp.stack([jnp.concatenate([jnp.concatenate([w2[l, e], zeros_lo], 1),
                                       jnp.concatenate([zeros_lo, a2[l, e]], 1)], 0) for e in range(2)])
    wts = dict(
        w_main=jnp.concatenate([wi[:, :3 * d_], wi[:, a_slab:]], 1).astype(BF16),
        w_lora=jnp.pad(wi[:, 3 * d_:a_slab], ((0, 0), (0, LORA_PAD - n_lora))).astype(BF16),
        mu=row(mu[:3 * d_]), mu_l=row(jnp.pad(mu[3 * d_:], (0, LORA_PAD - n_lora))), k_k=row(k_k[l]), k_a=row(k_a[l]),
        w_lo=w_lo.astype(BF16),
        wa0=jnp.concatenate([w0[l], a0[l]], -1)[:, None, :],
        conv_w=conv_w[l], conv_b=row(conv_b[l]),
        w_gate=jnp.stack([jnp.concatenate([_block_diag_groups(lru_wa[l, e], 4), _block_diag_groups(lru_wx[l, e], 4)], -1)
                          for e in range(2)]).astype(BF16),
        b_gate=jnp.concatenate([lru_ba[l], lru_bx[l]], -1)[:, None, :],
        c_lam=(-LRU_C * jax.nn.softplus(-lru_lam[l]))[:, None, :])

    s0 = jnp.zeros((2, b_, heads // 2, 2 * HEAD, 2 * HEAD), F32)
    h0 = jnp.zeros((2, b_, 1, d_), F32)
    _, ctx_states = token_scans(ctx, csc1, csh1, wts, False, (s0, h0))
    (main, mixed, lz, y2, hh), _ = token_scans(x, sc1, sh1, wts, True, ctx_states)

    zeros_a = jnp.zeros((LORA_DECAY, 2 * d_), F32)
    rw = jnp.pad(jnp.concatenate([router_g[l], router_e[l]], 1), ((0, 0), (0, 128 - N_GROUPS - N_EXPERTS)))
    rw_hi = rw.astype(BF16)
    params = [
        jnp.concatenate([zeros_a, jnp.concatenate([a2[l, 0], a2[l, 1]], 1)], 0).astype(BF16),
        row(jnp.concatenate([a0[l, 0], a0[l, 1]])),
        jnp.pad(g2[l], ((0, LORA_PAD - LORA_DECAY - LORA_AAA - LORA_GATE), (0, 0))).astype(BF16),
        row(k_a[l]), row(r_k[l]), row(gn_g[l]), row(gn_b[l]),
        p_a[l].astype(BF16), p_b[l].astype(BF16), w_o[l].astype(BF16), row(ln1_g[l]), row(ln1_b[l]),
        rw_hi, (rw - rw_hi.astype(F32)).astype(BF16),
        row(jnp.pad(jnp.concatenate([router_g_b[l], router_e_b[l]]), (0, 128 - N_GROUPS - N_EXPERTS)))]
    x1, h2, logits = post_mix(y2, mixed, lz, hh, main, x, gt1, sc2, sh2, params, 256)

    moe = hier_moe(h2.reshape(b_ * l_, d_), logits.reshape(b_ * l_, -1), e_w1[l], e_w3[l], e_w2[l]).reshape(b_, l_, d_)
    return res_ln(x1, moe, gt2, ln2_g[l], ln2_b[l], 512)
```

```python
import functools

import jax
import jax.numpy as jnp
from jax import lax
from jax.experimental import pallas as pl
from jax.experimental.pallas import tpu as pltpu

F32 = jnp.float32
BF16 = jnp.bfloat16

GRID_W = 64
HEAD = 64
LORA_DECAY = 64
LORA_AAA = 64
LORA_GATE = 160
LORA_PAD = 384
GN_EPS = 64e-5
LN_EPS = 1e-5
CONV_W = 5
LRU_C = 8.0
N_GROUPS = 4
EXPERTS_PER_GROUP = 8
N_EXPERTS = N_GROUPS * EXPERTS_PER_GROUP
TOP_K = 2
DEPTH = 1
ALPHA = (2 * DEPTH) ** 0.25

WKV_CHUNK = 64
MOE_ROWS = 256
GATE_GROUP = 4 * HEAD
HALO = 8
VMEM_LIMIT = 56 * 1024 * 1024

_NN = (((1,), (0,)), ((), ()))
_NT = (((1,), (1,)), ((), ()))
_TN = (((0,), (0,)), ((), ()))


def _split(x, n):
    if x.dtype == BF16:
        return [x]
    parts, rest = [], x.astype(F32)
    for i in range(n):
        p = rest.astype(BF16)
        parts.append(p)
        if i + 1 < n:
            rest = rest - p.astype(F32)
    return parts


def _dot(a, b, dims=_NN, pa=1, pb=1):
    ap, bp = _split(a, pa), _split(b, pb)
    order = max(len(ap), len(bp))
    acc = None
    for i, x in enumerate(ap):
        for j, y in enumerate(bp):
            if i + j < order:
                t = lax.dot_general(x, y, dims, preferred_element_type=F32)
                acc = t if acc is None else acc + t
    return acc


def _ln(x):
    mu = jnp.mean(x, -1, keepdims=True)
    xc = x - mu
    var = jnp.mean(xc * xc, -1, keepdims=True)
    return xc * lax.rsqrt(var + LN_EPS)


def _sigmoid(x):
    return 1.0 / (1.0 + jnp.exp(-x))


def _softplus(x):
    return jnp.maximum(x, 0.0) + jnp.log(1.0 + jnp.exp(-jnp.abs(x)))


def _head_sums(x):
    pw = 2 * HEAD
    row = lax.broadcasted_iota(jnp.int32, (pw, pw), 0)
    col = lax.broadcasted_iota(jnp.int32, (pw, pw), 1)
    ones = jnp.where((row >= HEAD) == (col >= HEAD), 1.0, 0.0).astype(BF16)
    return jnp.concatenate([_dot(x[:, p:p + pw], ones, _NN, 2, 1) for p in range(0, x.shape[1], pw)], axis=1)


def _inproj_kernel(x_ref, sc_ref, sh_ref, wm_ref, wl_ref, main_ref, lora_ref, h_scr, *, nm):
    j = pl.program_id(2)

    @pl.when(j == 0)
    def _():
        h_scr[...] = (_ln(x_ref[0]) * (1.0 + sc_ref[0]) + sh_ref[0]).astype(BF16)

    @pl.when(j < nm)
    def _():
        main_ref[0] = _dot(h_scr[...], wm_ref[...])

    @pl.when(j == nm)
    def _():
        lora_ref[0] = _dot(h_scr[...], wl_ref[...])


def inproj(x, sc, sh, w_main, w_lora, tm):
    b_, l_, d_ = x.shape
    nm = w_main.shape[1] // d_
    row = pl.BlockSpec((1, tm, d_), lambda b, i, j: (b, i, 0))
    vec = pl.BlockSpec((1, 1, d_), lambda b, i, j: (b, 0, 0))
    return pl.pallas_call(
        functools.partial(_inproj_kernel, nm=nm),
        grid=(b_, l_ // tm, nm + 1),
        in_specs=[row, vec, vec,
                  pl.BlockSpec((d_, d_), lambda b, i, j: (0, jnp.minimum(j, nm - 1))),
                  pl.BlockSpec((d_, LORA_PAD), lambda b, i, j: (0, 0))],
        out_specs=[pl.BlockSpec((1, tm, d_), lambda b, i, j: (b, i, jnp.minimum(j, nm - 1))),
                   pl.BlockSpec((1, tm, LORA_PAD), lambda b, i, j: (b, i, 0))],
        out_shape=[jax.ShapeDtypeStruct((b_, l_, nm * d_), F32), jax.ShapeDtypeStruct((b_, l_, LORA_PAD), F32)],
        scratch_shapes=[pltpu.VMEM((tm, d_), BF16)],
        compiler_params=pltpu.CompilerParams(vmem_limit_bytes=VMEM_LIMIT),
        name="inproj",
    )(x, sc, sh, w_main, w_lora)


def _mix_kernel(cur_ref, prev_ref, next_ref, lcur_ref, lprev_ref, lnext_ref, mu_ref, mul_ref, kk_ref,
                mixed_ref, lz_ref, *, grid_mode):
    i = pl.program_id(1)
    n_i = pl.num_programs(1)
    tm = cur_ref.shape[1]
    d_ = kk_ref.shape[1]

    def shifted(cur, prev, nxt):
        rows, ch = cur.shape
        rowi = lax.broadcasted_iota(jnp.int32, (rows, ch), 0)
        lane = lax.broadcasted_iota(jnp.int32, (rows, ch), 1)
        if grid_mode:
            prev = jnp.where(i > 0, prev, 0.0)
            nxt = jnp.where(i < n_i - 1, nxt, 0.0)
            up = jnp.concatenate([prev, cur[:rows - GRID_W]], axis=0)
            down = jnp.concatenate([cur[GRID_W:], nxt], axis=0)
            col = rowi % GRID_W
            left = jnp.where(col == 0, 0.0, pltpu.roll(cur, 1, 0))
            right = jnp.where(col == GRID_W - 1, 0.0, pltpu.roll(cur, rows - 1, 0))
            l4 = lane % 4
            return jnp.where(l4 == 0, left, jnp.where(l4 == 1, right, jnp.where(l4 == 2, up, down)))
        before = jnp.where(rowi == 0, 0.0, pltpu.roll(cur, 1, 0))
        after = jnp.where(rowi == rows - 1, 0.0, pltpu.roll(cur, rows - 1, 0))
        return jnp.where(lane % 2 == 0, before, after)

    def mixed(cur, prev, nxt, mu):
        return cur + mu * (shifted(cur, prev, nxt) - cur)

    for s in range(3):
        sl = slice(s * d_, (s + 1) * d_)
        z = mixed(cur_ref[0, :, sl], prev_ref[0, :, sl], next_ref[0, :, sl], mu_ref[:, sl])
        mixed_ref[0, :, sl] = z.astype(BF16)
        if s == 1:
            kq = z * kk_ref[...]
            kq = kq * lax.rsqrt(_head_sums(kq * kq) + 1e-12)
            mixed_ref[0, :, 3 * d_:4 * d_] = kq.astype(BF16)
    lz = mixed(lcur_ref[0], lprev_ref[0], lnext_ref[0], mul_ref[...])
    wa = lz[:, :LORA_DECAY + LORA_AAA]
    lane = lax.broadcasted_iota(jnp.int32, wa.shape, 1)
    lz_ref[0, :, :LORA_DECAY + LORA_AAA] = jnp.where(lane < LORA_DECAY, jnp.tanh(wa), wa).astype(BF16)
    lz_ref[0, :, LORA_DECAY + LORA_AAA:] = _sigmoid(lz[:, LORA_DECAY + LORA_AAA:]).astype(BF16)


def rwkv_mix(main, lora, mu, mu_l, k_k, grid_mode, tm):
    b_, l_, _ = main.shape
    d_ = k_k.shape[-1]
    if grid_mode:
        assert tm % GRID_W == 0 and l_ % tm == 0
        halo, per = GRID_W, tm // GRID_W
    else:
        assert tm == l_
        halo, per = 8, tm // 8
    n_h = l_ // halo
    cur = lambda w: pl.BlockSpec((1, tm, w), lambda b, i: (b, i, 0))
    prv = lambda w: pl.BlockSpec((1, halo, w), lambda b, i: (b, jnp.maximum(i * per - 1, 0), 0))
    nxt = lambda w: pl.BlockSpec((1, halo, w), lambda b, i: (b, jnp.minimum((i + 1) * per, n_h - 1), 0))
    par = lambda w: pl.BlockSpec((1, w), lambda b, i: (0, 0))
    return pl.pallas_call(
        functools.partial(_mix_kernel, grid_mode=grid_mode),
        grid=(b_, l_ // tm),
        in_specs=[cur(3 * d_), prv(3 * d_), nxt(3 * d_), cur(LORA_PAD), prv(LORA_PAD), nxt(LORA_PAD),
                  par(3 * d_), par(LORA_PAD), par(d_)],
        out_specs=[cur(4 * d_), cur(LORA_PAD)],
        out_shape=[jax.ShapeDtypeStruct((b_, l_, 4 * d_), BF16), jax.ShapeDtypeStruct((b_, l_, LORA_PAD), BF16)],
        compiler_params=pltpu.CompilerParams(vmem_limit_bytes=VMEM_LIMIT),
        name="rwkv_mix",
    )(main, main, main, lora, lora, lora, mu, mu_l, k_k)


def _wkv_kernel(r_ref, k_ref, v_ref, kk_ref, lz_ref, wlo_ref, wa0_ref, ka_ref, s0_ref, y_ref, sT_ref, s_scr):
    d = pl.program_id(0)
    c = pl.program_id(2)
    n_c = pl.num_programs(2)
    cs = r_ref.shape[1]
    d_ = r_ref.shape[2]
    pairs = s_scr.shape[0]
    pw = 2 * HEAD

    @pl.when(c == 0)
    def _():
        s_scr[...] = s0_ref[0, 0]

    sgn = 1 - 2 * d
    row = lax.broadcasted_iota(jnp.int32, (cs, cs), 0)
    col = lax.broadcasted_iota(jnp.int32, (cs, cs), 1)
    tri = jnp.where((row - col) * sgn >= 0, 1.0, 0.0).astype(BF16)
    row2 = lax.broadcasted_iota(jnp.int32, (2 * cs, pw), 0)
    col2 = lax.broadcasted_iota(jnp.int32, (2 * cs, pw), 1)
    tdiff = (row2 % cs - col2 % cs) * sgn
    amask = (tdiff > 0) | ((row2 >= cs) & (tdiff == 0))
    same_head = (row2 >= cs) == (col2 >= HEAD)
    eye = jnp.where(row2 == col2, 1.0, 0.0)
    lane = lax.broadcasted_iota(jnp.int32, (cs, pw), 1)
    m0, m1 = lane < HEAD, lane >= HEAD
    lane2 = lax.broadcasted_iota(jnp.int32, (2 * cs, pw), 1)
    mm0, mm1 = lane2 < HEAD, lane2 >= HEAD

    z = _dot(lz_ref[0, :, :LORA_DECAY + LORA_AAA], wlo_ref[0]) + wa0_ref[0]
    lw = -jnp.exp(-_softplus(-z[:, :d_]) - 0.5)
    a = _sigmoid(z[:, d_:])
    r, kk = r_ref[0].astype(F32), kk_ref[0].astype(F32)
    k = k_ref[0].astype(F32) * (1.0 + (a - 1.0) * ka_ref[...])

    cum = _dot(tri, lw, _NN, 1, 3)
    tot = jnp.sum(lw, axis=0, keepdims=True)
    b = kk * a
    e_neg = jnp.exp(-cum)
    e_end = jnp.exp(tot - cum)
    rt = (r * jnp.exp(cum)).astype(BF16)
    at = (-kk * jnp.exp(cum - lw)).astype(BF16)
    bt = (b * e_neg).astype(BF16)
    kt = (k * e_neg).astype(BF16)
    bd = (b * e_end).astype(BF16)
    kd = (k * e_end).astype(BF16)
    vb = v_ref[0]
    p_end = jnp.exp(tot)
    zero = jnp.zeros((cs, pw), BF16)
    cat = lambda x, y: jnp.concatenate([x, y], axis=0)
    dot = lambda x, y, dims=_NN: lax.dot_general(x, y, dims, preferred_element_type=F32)
    sel = lambda m, x: jnp.where(m, x, jnp.zeros_like(x))

    ar, aa0, aa1, xs, ts, nbs = [], [], [], [], [], []
    for p in range(pairs):
        sl = slice(p * pw, (p + 1) * pw)
        ar_p = cat(at[:, sl], rt[:, sl])
        q0 = sel(amask, dot(sel(mm0, ar_p), cat(bt[:, sl], kt[:, sl]), _NT))
        q1 = sel(amask, dot(sel(mm1, ar_p), cat(kt[:, sl], bt[:, sl]), _NT))
        nil = cat(sel(m0, q0[:cs]), sel(m1, q1[:cs]))
        ar.append(ar_p); aa0.append(q0.astype(BF16)); aa1.append(q1.astype(BF16))
        ts.append(eye + nil); nbs.append(nil.astype(BF16))
    for p in range(pairs):
        v_p = vb[:, p * pw:(p + 1) * pw]
        xs.append(dot(aa0[p][:cs], cat(zero, sel(m0, v_p))) + dot(aa1[p][:cs], cat(sel(m1, v_p), zero)))
    for i in range(max(cs.bit_length() - 2, 0)):
        nbs = [dot(nb, nb).astype(BF16) for nb in nbs]
        ts = [t + dot(t.astype(BF16), nb) for t, nb in zip(ts, nbs)]
    ts = [t.astype(BF16) for t in ts]

    st = [s_scr[p] for p in range(pairs)]
    as_ = [dot(ar[p], st[p].astype(BF16)) for p in range(pairs)]
    us = []
    for p in range(pairs):
        rhs = (as_[p][:cs] + xs[p]).astype(BF16)
        uu = dot(ts[p], cat(sel(m0, rhs), sel(m1, rhs)))
        us.append((uu[:cs] + uu[cs:]).astype(BF16))
    for p in range(pairs):
        sl = slice(p * pw, (p + 1) * pw)
        u_p, v_p = us[p], vb[:, sl]
        y = (as_[p][cs:] + dot(aa0[p][cs:], cat(sel(m0, u_p), sel(m0, v_p)))
             + dot(aa1[p][cs:], cat(sel(m1, v_p), sel(m1, u_p))))
        y_ref[0, 0, :, sl] = y
        p_col = jnp.broadcast_to(p_end[:, sl], (pw, pw)).T
        upd = dot(cat(bd[:, sl], kd[:, sl]), cat(u_p, v_p), _TN)
        s_scr[p] = p_col * st[p] + sel(same_head, upd)

    @pl.when(c == n_c - 1)
    def _():
        sT_ref[0, 0] = s_scr[...]


def _pair_states(s):
    st = jnp.swapaxes(s, -1, -2)
    st = st.reshape(s.shape[:-3] + (s.shape[-3] // 2, 2, HEAD, HEAD))
    eye = jnp.eye(2, dtype=s.dtype)
    out = jnp.einsum("...pakv,ab->...pakbv", st, eye)
    return out.reshape(s.shape[:-3] + (s.shape[-3] // 2, 2 * HEAD, 2 * HEAD))


def wkv7(mixed, lz, w_lo, wa0, k_a, s0p):
    b_, l_, d4 = mixed.shape
    d_ = d4 // 4
    pairs, pw = d_ // (2 * HEAD), 2 * HEAD
    cs = WKV_CHUNK
    assert cs == HEAD and l_ % cs == 0
    n_c = l_ // cs

    def tmap(d, b, c):
        return c + d * (n_c - 1 - 2 * c)

    sec = lambda j: pl.BlockSpec((1, cs, d_), lambda d, b, c: (b, tmap(d, b, c), j))
    state = pl.BlockSpec((1, 1, pairs, pw, pw), lambda d, b, c: (d, b, 0, 0, 0))
    return pl.pallas_call(
        _wkv_kernel,
        grid=(2, b_, n_c),
        in_specs=[sec(0), sec(1), sec(2), sec(3),
                  pl.BlockSpec((1, cs, LORA_PAD), lambda d, b, c: (b, tmap(d, b, c), 0)),
                  pl.BlockSpec((1, pw, 2 * d_), lambda d, b, c: (d, 0, 0)),
                  pl.BlockSpec((1, 1, 2 * d_), lambda d, b, c: (d, 0, 0)),
                  pl.BlockSpec((1, d_), lambda d, b, c: (0, 0)),
                  state],
        out_specs=[pl.BlockSpec((1, 1, cs, d_), lambda d, b, c: (d, b, tmap(d, b, c), 0)), state],
        out_shape=[jax.ShapeDtypeStruct((2, b_, l_, d_), F32),
                   jax.ShapeDtypeStruct((2, b_, pairs, pw, pw), F32)],
        scratch_shapes=[pltpu.VMEM((pairs, pw, pw), F32)],
        compiler_params=pltpu.CompilerParams(
            dimension_semantics=("arbitrary", "arbitrary", "arbitrary"), vmem_limit_bytes=VMEM_LIMIT),
        name="wkv7",
    )(mixed, mixed, mixed, mixed, lz, w_lo, wa0, k_a, s0p)


def _lru_kernel(cur_ref, prev_ref, next_ref, cw_ref, cb_ref, wg_ref, bg_ref, cl_ref, h0_ref,
                h_ref, hT_ref, a_scr, u_scr, h_scr):
    d = pl.program_id(0)
    c = pl.program_id(2)
    n_c = pl.num_programs(2)
    ti = c + d * (n_c - 1 - 2 * c)
    tm, d_ = a_scr.shape
    gw = GATE_GROUP

    @pl.when(c == 0)
    def _():
        h_scr[...] = h0_ref[0, 0]

    prev = jnp.where(ti > 0, prev_ref[0], 0.0)
    nxt = jnp.where(ti < n_c - 1, next_ref[0], 0.0)
    ext = jnp.concatenate([prev, cur_ref[0], nxt], axis=0)
    xc = cb_ref[...]
    for j in range(CONV_W):
        o = HALO - CONV_W // 2 + j
        xc = xc + cw_ref[j:j + 1, :] * ext[o:o + tm]
    for g in range(d_ // gw):
        sl = slice(g * gw, (g + 1) * gw)
        z = _dot(xc[:, sl], wg_ref[0, g])
        rg = _sigmoid(z[:, :gw] + bg_ref[0, :, sl])
        ig = _sigmoid(z[:, gw:] + bg_ref[0, :, d_ + g * gw:d_ + (g + 1) * gw])
        log_a = cl_ref[0, :, sl] * rg
        a_scr[:, sl] = jnp.exp(log_a)
        u_scr[:, sl] = jnp.sqrt(1.0 - jnp.exp(2.0 * log_a)) * ig * xc[:, sl]

    rowid = lax.broadcasted_iota(jnp.int32, (8, d_), 0)

    def tile_scan(i, h, rev):
        t8 = (tm // 8 - 1 - i) if rev else i
        r0 = pl.multiple_of(t8 * 8, 8)
        a8 = a_scr[pl.ds(r0, 8), :]
        u8 = u_scr[pl.ds(r0, 8), :]
        for s in (1, 2, 4):
            ok = (rowid < 8 - s) if rev else (rowid >= s)
            sh = (8 - s) if rev else s
            a_sh = jnp.where(ok, pltpu.roll(a8, sh, 0), 1.0)
            u_sh = jnp.where(ok, pltpu.roll(u8, sh, 0), 0.0)
            u8 = a8 * u_sh + u8
            a8 = a8 * a_sh
        h8 = u8 + a8 * h
        h_ref[0, 0, pl.ds(r0, 8), :] = h8
        return h8[0:1] if rev else h8[7:8]

    @pl.when(d == 0)
    def _():
        h_scr[...] = lax.fori_loop(0, tm // 8, lambda i, h: tile_scan(i, h, False), h_scr[...], unroll=2)

    @pl.when(d == 1)
    def _():
        h_scr[...] = lax.fori_loop(0, tm // 8, lambda i, h: tile_scan(i, h, True), h_scr[...], unroll=2)

    @pl.when(c == n_c - 1)
    def _():
        hT_ref[0, 0] = h_scr[...]


def rglru(main, xb_col, conv_w, conv_b, w_gate, b_gate, c_lam, h0, tm):
    b_, l_, _ = main.shape
    d_ = conv_w.shape[1]
    tm = min(tm, l_)
    n_c = l_ // tm
    per = tm // HALO
    n_h = l_ // HALO
    tmap = lambda d, c: c + d * (n_c - 1 - 2 * c)
    dirp = lambda shape: pl.BlockSpec((1,) + shape, lambda d, b, c: (d,) + (0,) * len(shape))
    st = pl.BlockSpec((1, 1, 1, d_), lambda d, b, c: (d, b, 0, 0))
    return pl.pallas_call(
        _lru_kernel,
        grid=(2, b_, n_c),
        in_specs=[pl.BlockSpec((1, tm, d_), lambda d, b, c: (b, tmap(d, c), xb_col)),
                  pl.BlockSpec((1, HALO, d_), lambda d, b, c: (b, jnp.maximum(tmap(d, c) * per - 1, 0), xb_col)),
                  pl.BlockSpec((1, HALO, d_), lambda d, b, c: (b, jnp.minimum((tmap(d, c) + 1) * per, n_h - 1), xb_col)),
                  pl.BlockSpec((CONV_W, d_), lambda d, b, c: (0, 0)),
                  pl.BlockSpec((1, d_), lambda d, b, c: (0, 0)),
                  dirp(w_gate.shape[1:]), dirp((1, 2 * d_)), dirp((1, d_)), st],
        out_specs=[pl.BlockSpec((1, 1, tm, d_), lambda d, b, c: (d, b, tmap(d, c), 0)), st],
        out_shape=[jax.ShapeDtypeStruct((2, b_, l_, d_), F32), jax.ShapeDtypeStruct((2, b_, 1, d_), F32)],
        scratch_shapes=[pltpu.VMEM((tm, d_), F32), pltpu.VMEM((tm, d_), F32), pltpu.VMEM((1, d_), F32)],
        compiler_params=pltpu.CompilerParams(
            dimension_semantics=("arbitrary", "arbitrary", "arbitrary"), vmem_limit_bytes=VMEM_LIMIT),
        name="rglru",
    )(main, main, main, conv_w, conv_b, w_gate, b_gate, c_lam, h0)


def _post_kernel(yf_ref, yb_ref, r_ref, k_ref, v_ref, lz_ref, hf_ref, hb_ref, gb_ref, ga_ref, gm_ref, x_ref,
                 gt_ref, sc_ref, sh_ref, wa2_ref, a0_ref, g2_ref, ka_ref, rk_ref, gg_ref, gnb_ref,
                 pa_ref, pb_ref, wo_ref, lg_ref, lb_ref, rwh_ref, rwl_ref, rb_ref,
                 x1_ref, h2_ref, lg_out_ref):
    d_ = x_ref.shape[2]
    y = yf_ref[0, 0] + yb_ref[0, 0]
    mu = _head_sums(y) * (1.0 / HEAD)
    yc = y - mu
    var = _head_sums(yc * yc) * (1.0 / HEAD)
    yn = yc * lax.rsqrt(var + GN_EPS) * gg_ref[...] + gnb_ref[...]
    lz = lz_ref[0]
    a2 = _sigmoid(_dot(lz[:, :LORA_DECAY + LORA_AAA], wa2_ref[...]) + a0_ref[...])
    r, k, v = r_ref[0].astype(F32), k_ref[0].astype(F32), v_ref[0].astype(F32)
    ksum = k * (2.0 + (a2[:, :d_] + a2[:, d_:] - 2.0) * ka_ref[...])
    bonus = _head_sums(r * ksum * rk_ref[...]) * v
    g = _dot(lz[:, LORA_DECAY + LORA_AAA:], g2_ref[...])
    y_a = ((yn + bonus) * g).astype(BF16)
    gb = gb_ref[0]
    gelu = 0.5 * gb * (1.0 + jnp.tanh(0.7978845608028654 * (gb + 0.044715 * gb * gb * gb)))
    y_b = ((hf_ref[0, 0] + hb_ref[0, 0]) * gelu).astype(BF16)
    m = _sigmoid(ga_ref[0]) * _dot(y_a, pa_ref[...]) + _sigmoid(gm_ref[0]) * _dot(y_b, pb_ref[...])
    mix = _dot(m, wo_ref[...])
    x1 = _ln(ALPHA * _ln(x_ref[0]) + gt_ref[0] * mix) * lg_ref[...] + lb_ref[...]
    x1_ref[0] = x1
    h2 = x1 * (1.0 + sc_ref[0]) + sh_ref[0]
    h2_ref[0] = h2
    hh, hl = _split(h2, 2)
    lg_out_ref[0] = (_dot(hh, rwh_ref[...]) + _dot(hl, rwh_ref[...]) + _dot(hh, rwl_ref[...])) + rb_ref[...]


def post_mix(y2, mixed, lz, hh, main, x, gt, sc, sh, params, tm):
    b_, l_, d_ = x.shape
    dsec = lambda e: pl.BlockSpec((1, 1, tm, d_), lambda b, i: (e, b, i, 0))
    col = lambda j: pl.BlockSpec((1, tm, d_), lambda b, i: (b, i, j))
    vec = pl.BlockSpec((1, 1, d_), lambda b, i: (b, 0, 0))
    full = lambda a: pl.BlockSpec(a.shape, lambda b, i: (0,) * a.ndim)
    return pl.pallas_call(
        _post_kernel,
        grid=(b_, l_ // tm),
        in_specs=[dsec(0), dsec(1), col(0), col(1), col(2),
                  pl.BlockSpec((1, tm, LORA_PAD), lambda b, i: (b, i, 0)),
                  dsec(0), dsec(1), col(4), col(5), col(6), col(0), vec, vec, vec] + [full(p) for p in params],
        out_specs=[col(0), col(0), pl.BlockSpec((1, tm, 128), lambda b, i: (b, i, 0))],
        out_shape=[jax.ShapeDtypeStruct((b_, l_, d_), F32), jax.ShapeDtypeStruct((b_, l_, d_), F32),
                   jax.ShapeDtypeStruct((b_, l_, 128), F32)],
        compiler_params=pltpu.CompilerParams(vmem_limit_bytes=VMEM_LIMIT),
        name="post_mix",
    )(y2, y2, mixed, mixed, mixed, lz, hh, hh, main, main, main, x, gt, sc, sh, *params)


def _mm_kernel(x_ref, w_ref, o_ref, *, pa, pb):
    o_ref[...] = _dot(x_ref[...], w_ref[...], _NN, pa, pb)


def pmm(x, w, tm=512, tn=1024, pa=1, pb=1):
    m_, k_ = x.shape
    n_ = w.shape[1]
    tm, tn = min(tm, m_), min(tn, n_)
    assert m_ % tm == 0 and n_ % tn == 0, (x.shape, w.shape, tm, tn)
    return pl.pallas_call(
        functools.partial(_mm_kernel, pa=pa, pb=pb),
        grid=(m_ // tm, n_ // tn),
        in_specs=[pl.BlockSpec((tm, k_), lambda i, j: (i, 0)), pl.BlockSpec((k_, tn), lambda i, j: (0, j))],
        out_specs=pl.BlockSpec((tm, tn), lambda i, j: (i, j)),
        out_shape=jax.ShapeDtypeStruct((m_, n_), F32),
        compiler_params=pltpu.CompilerParams(vmem_limit_bytes=VMEM_LIMIT),
        name="pmm",
    )(x, w)


ROUTE_LANES = 128
R_E, R_RANK, R_W = 0, 2, 4
DMA_UNROLL = 4


def _route_kernel(lg_ref, rec_ref, cnt_ref, carry):
    i = pl.program_id(0)
    tm = lg_ref.shape[0]

    @pl.when(i == 0)
    def _():
        carry[...] = jnp.zeros_like(carry)

    lg = lg_ref[...]
    lane = lax.broadcasted_iota(jnp.int32, lg.shape, 1)
    neg = -jnp.inf
    first = lambda m: jnp.min(jnp.where(m, lane, ROUTE_LANES), axis=1, keepdims=True)
    is_g = lane < N_GROUPS
    gmax = jnp.max(jnp.where(is_g, lg, neg), axis=1, keepdims=True)
    gsel = first(is_g & (lg == gmax))
    p_g = 1.0 / jnp.sum(jnp.where(is_g, jnp.exp(lg - gmax), 0.0), axis=1, keepdims=True)
    in_grp = (lane >= N_GROUPS) & (lane < N_GROUPS + N_EXPERTS) & ((lane - N_GROUPS) // EXPERTS_PER_GROUP == gsel)
    el = jnp.where(in_grp, lg, neg)
    v1 = jnp.max(el, axis=1, keepdims=True)
    i1 = first(in_grp & (el == v1))
    rest = in_grp & (lane != i1)
    el2 = jnp.where(rest, lg, neg)
    v2 = jnp.max(el2, axis=1, keepdims=True)
    i2 = first(rest & (el2 == v2))
    e21 = jnp.exp(v2 - v1)
    w1 = p_g / (1.0 + e21)
    w2 = w1 * e21
    oh1, oh2 = lane == i1, lane == i2
    both = jnp.where(oh1 | oh2, 1.0, 0.0)
    row = lax.broadcasted_iota(jnp.int32, (tm, tm), 0)
    col = lax.broadcasted_iota(jnp.int32, (tm, tm), 1)
    before = jnp.where(col < row, 1.0, 0.0).astype(BF16)
    cnt = _dot(before, both.astype(BF16)) + carry[...]
    rank1 = jnp.sum(jnp.where(oh1, cnt, 0.0), axis=1, keepdims=True)
    rank2 = jnp.sum(jnp.where(oh2, cnt, 0.0), axis=1, keepdims=True)
    carry[...] = carry[...] + jnp.sum(both, axis=0, keepdims=True)
    rec = jnp.zeros_like(lg)
    for k, val in ((R_E, (i1 - N_GROUPS).astype(F32)), (R_E + 1, (i2 - N_GROUPS).astype(F32)),
                   (R_RANK, rank1), (R_RANK + 1, rank2), (R_W, w1), (R_W + 1, w2)):
        rec = jnp.where(lane == k, val, rec)
    rec_ref[...] = rec
    cnt_ref[...] = carry[...]


def route(logits, tm=256):
    t_ = logits.shape[0]
    return pl.pallas_call(
        _route_kernel,
        grid=(t_ // tm,),
        in_specs=[pl.BlockSpec((tm, ROUTE_LANES), lambda i: (i, 0))],
        out_specs=[pl.BlockSpec((tm, ROUTE_LANES), lambda i: (i, 0)), pl.BlockSpec((1, ROUTE_LANES), lambda i: (0, 0))],
        out_shape=[jax.ShapeDtypeStruct((t_, ROUTE_LANES), F32), jax.ShapeDtypeStruct((1, ROUTE_LANES), F32)],
        scratch_shapes=[pltpu.VMEM((1, ROUTE_LANES), F32)],
        compiler_params=pltpu.CompilerParams(dimension_semantics=("arbitrary",)),
        name="route",
    )(logits)


def _dispatch_kernel(dest_ref, h_ref, xs_in_ref, xs_ref, sem):
    del xs_in_ref
    tm = h_ref.shape[0]
    base = pl.program_id(0) * tm * TOP_K

    def copy(r, s):
        return pltpu.make_async_copy(h_ref.at[pl.ds(r, 1)], xs_ref.at[pl.ds(dest_ref[base + r * TOP_K + s], 1)], sem)

    def start(r, carry):
        for s in range(TOP_K):
            copy(r, s).start()
        return carry

    def wait(r, carry):
        for s in range(TOP_K):
            copy(r, s).wait()
        return carry

    lax.fori_loop(0, tm, start, 0, unroll=DMA_UNROLL)
    lax.fori_loop(0, tm, wait, 0, unroll=DMA_UNROLL)


def dispatch(dest, h, n_pad, tm=512):
    t_, d_ = h.shape
    grid_spec = pltpu.PrefetchScalarGridSpec(
        num_scalar_prefetch=1,
        grid=(t_ // tm,),
        in_specs=[pl.BlockSpec((tm, d_), lambda i, dest: (i, 0)), pl.BlockSpec(memory_space=pl.ANY)],
        out_specs=pl.BlockSpec(memory_space=pl.ANY),
        scratch_shapes=[pltpu.SemaphoreType.DMA(())],
    )
    return pl.pallas_call(
        _dispatch_kernel,
        grid_spec=grid_spec,
        out_shape=jax.ShapeDtypeStruct((n_pad, d_), F32),
        input_output_aliases={2: 0},
        compiler_params=pltpu.CompilerParams(dimension_semantics=("arbitrary",), has_side_effects=True),
        name="dispatch",
    )(dest, h, jnp.zeros((n_pad, d_), F32))


def _moe_kernel(be_ref, nv_ref, x_ref, w1_ref, w3_ref, w2_ref, o_ref):
    blk = pl.program_id(0)

    @pl.when(nv_ref[blk] > 0)
    def _():
        x = x_ref[...].astype(BF16)
        h1 = _dot(x, w1_ref[0].astype(BF16))
        h3 = _dot(x, w3_ref[0].astype(BF16))
        hh = h1 * _sigmoid(h1) * h3
        o_ref[...] = _dot(hh, w2_ref[0].astype(BF16))

    @pl.when(nv_ref[blk] == 0)
    def _():
        o_ref[...] = jnp.zeros_like(o_ref)


def moe_experts(xs, blk_e, blk_n, w1, w3, w2):
    n_pad, d_ = xs.shape
    n_blocks = n_pad // MOE_ROWS
    de = w1.shape[2]
    grid_spec = pltpu.PrefetchScalarGridSpec(
        num_scalar_prefetch=2,
        grid=(n_blocks,),
        in_specs=[pl.BlockSpec((MOE_ROWS, d_), lambda i, be, nv: (i, 0)),
                  pl.BlockSpec((1, d_, de), lambda i, be, nv: (be[i], 0, 0)),
                  pl.BlockSpec((1, d_, de), lambda i, be, nv: (be[i], 0, 0)),
                  pl.BlockSpec((1, de, d_), lambda i, be, nv: (be[i], 0, 0))],
        out_specs=pl.BlockSpec((MOE_ROWS, d_), lambda i, be, nv: (i, 0)),
    )
    return pl.pallas_call(
        _moe_kernel,
        grid_spec=grid_spec,
        out_shape=jax.ShapeDtypeStruct((n_pad, d_), F32),
        compiler_params=pltpu.CompilerParams(vmem_limit_bytes=VMEM_LIMIT),
        name="moe_experts",
    )(blk_e, blk_n, xs, w1, w3, w2)


def _combine_kernel(dest_ref, x_ref, rec_ref, gt_ref, g_ref, b_ref, ys_ref, o_ref, buf, sem):
    tm = x_ref.shape[1]
    base = (pl.program_id(0) * pl.num_programs(1) + pl.program_id(1)) * tm * TOP_K

    def copy(r, s):
        return pltpu.make_async_copy(ys_ref.at[pl.ds(dest_ref[base + r * TOP_K + s], 1)], buf.at[s, pl.ds(r, 1)], sem)

    def start(r, carry):
        for s in range(TOP_K):
            copy(r, s).start()
        return carry

    def wait(r, carry):
        for s in range(TOP_K):
            copy(r, s).wait()
        return carry

    lax.fori_loop(0, tm, start, 0, unroll=DMA_UNROLL)
    lax.fori_loop(0, tm, wait, 0, unroll=DMA_UNROLL)
    rec = rec_ref[...]
    moe = rec[:, R_W:R_W + 1] * buf[0]
    for s in range(1, TOP_K):
        moe = moe + rec[:, R_W + s:R_W + s + 1] * buf[s]
    o_ref[0] = _ln(ALPHA * x_ref[0] + gt_ref[0] * moe) * g_ref[...] + b_ref[...]


def combine_ln(dest, x, rec, gt, g, b, ys, tm=512):
    b_, l_, d_ = x.shape
    n_i = l_ // tm
    grid_spec = pltpu.PrefetchScalarGridSpec(
        num_scalar_prefetch=1,
        grid=(b_, n_i),
        in_specs=[pl.BlockSpec((1, tm, d_), lambda bi, i, dest: (bi, i, 0)),
                  pl.BlockSpec((tm, ROUTE_LANES), lambda bi, i, dest: (bi * n_i + i, 0)),
                  pl.BlockSpec((1, 1, d_), lambda bi, i, dest: (bi, 0, 0)),
                  pl.BlockSpec((1, d_), lambda bi, i, dest: (0, 0)),
                  pl.BlockSpec((1, d_), lambda bi, i, dest: (0, 0)),
                  pl.BlockSpec(memory_space=pl.ANY)],
        out_specs=pl.BlockSpec((1, tm, d_), lambda bi, i, dest: (bi, i, 0)),
        scratch_shapes=[pltpu.VMEM((TOP_K, tm, d_), F32), pltpu.SemaphoreType.DMA(())],
    )
    return pl.pallas_call(
        _combine_kernel,
        grid_spec=grid_spec,
        out_shape=jax.ShapeDtypeStruct(x.shape, F32),
        compiler_params=pltpu.CompilerParams(
            dimension_semantics=("arbitrary", "arbitrary"), vmem_limit_bytes=VMEM_LIMIT),
        name="combine_ln",
    )(dest, x, rec, gt, g.reshape(1, d_), b.reshape(1, d_), ys)


def hier_moe_ln(x1, h2, logits, gt, g, b, w1, w3, w2):
    b_, l_, d_ = x1.shape
    t_ = b_ * l_
    rec, cnt = route(logits)
    counts = cnt[0, N_GROUPS:N_GROUPS + N_EXPERTS].astype(jnp.int32)
    padded = (counts + MOE_ROWS - 1) // MOE_ROWS * MOE_ROWS
    pad_end = jnp.cumsum(padded)
    pad_start = pad_end - padded
    eid = rec[:, R_E:R_E + TOP_K].astype(jnp.int32)
    rank = rec[:, R_RANK:R_RANK + TOP_K].astype(jnp.int32)
    onehot = eid[..., None] == jnp.arange(N_EXPERTS, dtype=jnp.int32)
    dest = (rank + jnp.sum(jnp.where(onehot, pad_start, 0), -1)).reshape(-1)
    n_blocks = -(-(t_ * TOP_K) // MOE_ROWS) + N_EXPERTS
    blk_lo = jnp.arange(n_blocks, dtype=jnp.int32) * MOE_ROWS
    blk_e = jnp.minimum(jnp.sum(blk_lo[:, None] >= pad_end[None, :], -1), N_EXPERTS - 1).astype(jnp.int32)
    blk_n = jnp.clip(jnp.sum(jnp.where(blk_e[:, None] == jnp.arange(N_EXPERTS), pad_start + counts, 0), -1) - blk_lo,
                     0, MOE_ROWS).astype(jnp.int32)
    xs = dispatch(dest, h2.reshape(t_, d_), n_blocks * MOE_ROWS)
    ys = moe_experts(xs, blk_e, blk_n, w1, w3, w2)
    return combine_ln(dest, x1, rec, gt, g, b, ys)


def _block_diag_groups(w, grp):
    n = w.shape[0]
    wg = w.reshape(n // grp, grp, HEAD, HEAD)
    eye = jnp.eye(grp, dtype=w.dtype)
    return jnp.einsum("gaij,ab->gaibj", wg, eye).reshape(n // grp, grp * HEAD, grp * HEAD)


def token_scans(h_in, sc, sh, wts, grid_mode, states):
    b_, l_, d_ = h_in.shape
    main, lora = inproj(h_in, sc, sh, wts["w_main"], wts["w_lora"], min(512, l_))
    mixed, lz = rwkv_mix(main, lora, wts["mu"], wts["mu_l"], wts["k_k"], grid_mode, 256 if grid_mode else l_)
    y2, s_new = wkv7(mixed, lz, wts["w_lo"], wts["wa0"], wts["k_a"], states[0])
    hh, h_new = rglru(main, 3, wts["conv_w"], wts["conv_b"], wts["w_gate"], wts["b_gate"], wts["c_lam"], states[1], 512)
    return (main, mixed, lz, y2, hh), (s_new, h_new)


def kernel(x, c, ctx, c_ctx, w_ada, b_ada, w_in, mu_a, w0, w2, a0, a2, g2, k_k, k_a, r_k, gn_g, gn_b, conv_w, conv_b, lru_wa, lru_ba, lru_wx, lru_bx, lru_lam, p_a, p_b, w_o, ln1_g, ln1_b, router_g, router_g_b, router_e, router_e_b, e_w1, e_w3, e_w2, ln2_g, ln2_b):
    b_, l_, d_ = x.shape
    heads = d_ // HEAD
    l = 0
    row = lambda v: v.reshape(1, -1)
    cc = jnp.concatenate([c, c_ctx[None]], 0)
    cc = jnp.pad(jax.nn.silu(cc), ((0, 8 - cc.shape[0]), (0, 0)))
    mod = pmm(cc, w_ada[l], 8, 1024, 2, 2)[:b_ + 1] + b_ada[l]
    mods = jnp.split(mod, 6, axis=-1)
    sh1, sc1, gt1, sh2, sc2, gt2 = [m[:b_, None, :] for m in mods]
    csh1, csc1 = [jnp.broadcast_to(m[b_:, None, :], (b_, 1, d_)) for m in mods[:2]]

    a_slab = 3 * d_ + LORA_DECAY + LORA_AAA + LORA_GATE
    n_lora = a_slab - 3 * d_
    wi, mu = w_in[l], mu_a[l]
    zeros_lo = jnp.zeros((LORA_DECAY, d_), F32)
    w_lo = jnp.stack([jnp.concatenate([jnp.concatenate([w2[l, e], zeros_lo], 1),
                                       jnp.concatenate([zeros_lo, a2[l, e]], 1)], 0) for e in range(2)])
    wts = dict(
        w_main=jnp.concatenate([wi[:, :3 * d_], wi[:, a_slab:]], 1).astype(BF16),
        w_lora=jnp.pad(wi[:, 3 * d_:a_slab], ((0, 0), (0, LORA_PAD - n_lora))).astype(BF16),
        mu=row(mu[:3 * d_]), mu_l=row(jnp.pad(mu[3 * d_:], (0, LORA_PAD - n_lora))), k_k=row(k_k[l]), k_a=row(k_a[l]),
        w_lo=w_lo.astype(BF16),
        wa0=jnp.concatenate([w0[l], a0[l]], -1)[:, None, :],
        conv_w=conv_w[l], conv_b=row(conv_b[l]),
        w_gate=jnp.stack([jnp.concatenate([_block_diag_groups(lru_wa[l, e], 4), _block_diag_groups(lru_wx[l, e], 4)], -1)
                          for e in range(2)]).astype(BF16),
        b_gate=jnp.concatenate([lru_ba[l], lru_bx[l]], -1)[:, None, :],
        c_lam=(-LRU_C * jax.nn.softplus(-lru_lam[l]))[:, None, :])

    s0 = jnp.zeros((2, b_, heads // 2, 2 * HEAD, 2 * HEAD), F32)
    h0 = jnp.zeros((2, b_, 1, d_), F32)
    _, ctx_states = token_scans(ctx, csc1, csh1, wts, False, (s0, h0))
    (main, mixed, lz, y2, hh), _ = token_scans(x, sc1, sh1, wts, True, ctx_states)

    zeros_a = jnp.zeros((LORA_DECAY, 2 * d_), F32)
    rw = jnp.pad(jnp.concatenate([router_g[l], router_e[l]], 1), ((0, 0), (0, 128 - N_GROUPS - N_EXPERTS)))
    rw_hi = rw.astype(BF16)
    params = [
        jnp.concatenate([zeros_a, jnp.concatenate([a2[l, 0], a2[l, 1]], 1)], 0).astype(BF16),
        row(jnp.concatenate([a0[l, 0], a0[l, 1]])),
        jnp.pad(g2[l], ((0, LORA_PAD - LORA_DECAY - LORA_AAA - LORA_GATE), (0, 0))).astype(BF16),
        row(k_a[l]), row(r_k[l]), row(gn_g[l]), row(gn_b[l]),
        p_a[l].astype(BF16), p_b[l].astype(BF16), w_o[l].astype(BF16), row(ln1_g[l]), row(ln1_b[l]),
        rw_hi, (rw - rw_hi.astype(F32)).astype(BF16),
        row(jnp.pad(jnp.concatenate([router_g_b[l], router_e_b[l]]), (0, 128 - N_GROUPS - N_EXPERTS)))]
    x1, h2, logits = post_mix(y2, mixed, lz, hh, main, x, gt1, sc2, sh2, params, 256)

    return hier_moe_ln(x1, h2, logits.reshape(b_ * l_, -1), gt2, ln2_g[l], ln2_b[l], e_w1[l], e_w3[l], e_w2[l])
```

```python
import functools

import jax
import jax.numpy as jnp
from jax import lax
from jax.experimental import pallas as pl
from jax.experimental.pallas import tpu as pltpu

F32 = jnp.float32
BF16 = jnp.bfloat16

GRID_W = 64
HEAD = 64
LORA_DECAY = 64
LORA_AAA = 64
LORA_GATE = 160
LORA_PAD = 384
GN_EPS = 64e-5
LN_EPS = 1e-5
CONV_W = 5
LRU_C = 8.0
N_GROUPS = 4
EXPERTS_PER_GROUP = 8
N_EXPERTS = N_GROUPS * EXPERTS_PER_GROUP
TOP_K = 2
DEPTH = 1
ALPHA = (2 * DEPTH) ** 0.25

WKV_CHUNK = 64
MOE_ROWS = 256
GATE_GROUP = 4 * HEAD
HALO = 8
VMEM_LIMIT = 56 * 1024 * 1024

_NN = (((1,), (0,)), ((), ()))
_NT = (((1,), (1,)), ((), ()))
_TN = (((0,), (0,)), ((), ()))


def _split(x, n):
    if x.dtype == BF16:
        return [x]
    parts, rest = [], x.astype(F32)
    for i in range(n):
        p = rest.astype(BF16)
        parts.append(p)
        if i + 1 < n:
            rest = rest - p.astype(F32)
    return parts


def _dot(a, b, dims=_NN, pa=1, pb=1):
    ap, bp = _split(a, pa), _split(b, pb)
    order = max(len(ap), len(bp))
    acc = None
    for i, x in enumerate(ap):
        for j, y in enumerate(bp):
            if i + j < order:
                t = lax.dot_general(x, y, dims, preferred_element_type=F32)
                acc = t if acc is None else acc + t
    return acc


def _ln(x):
    mu = jnp.mean(x, -1, keepdims=True)
    xc = x - mu
    var = jnp.mean(xc * xc, -1, keepdims=True)
    return xc * lax.rsqrt(var + LN_EPS)


def _sigmoid(x):
    return 0.5 + 0.5 * jnp.tanh(0.5 * x)


def _softplus(x):
    return jnp.maximum(x, 0.0) + jnp.log(1.0 + jnp.exp(-jnp.abs(x)))


def _head_sums(x):
    pw = 2 * HEAD
    row = lax.broadcasted_iota(jnp.int32, (pw, pw), 0)
    col = lax.broadcasted_iota(jnp.int32, (pw, pw), 1)
    ones = jnp.where((row >= HEAD) == (col >= HEAD), 1.0, 0.0).astype(BF16)
    return jnp.concatenate([_dot(x[:, p:p + pw], ones, _NN, 2, 1) for p in range(0, x.shape[1], pw)], axis=1)


def _inproj_kernel(x_ref, sc_ref, sh_ref, wm_ref, wl_ref, main_ref, lora_ref, h_scr, *, nm):
    j = pl.program_id(2)

    @pl.when(j == 0)
    def _():
        h_scr[...] = (_ln(x_ref[0]) * (1.0 + sc_ref[0]) + sh_ref[0]).astype(BF16)

    @pl.when(j < nm)
    def _():
        main_ref[0] = _dot(h_scr[...], wm_ref[...])

    @pl.when(j == nm)
    def _():
        lora_ref[0] = _dot(h_scr[...], wl_ref[...])


def inproj(x, sc, sh, w_main, w_lora, tm):
    b_, l_, d_ = x.shape
    nm = w_main.shape[1] // d_
    row = pl.BlockSpec((1, tm, d_), lambda b, i, j: (b, i, 0))
    vec = pl.BlockSpec((1, 1, d_), lambda b, i, j: (b, 0, 0))
    return pl.pallas_call(
        functools.partial(_inproj_kernel, nm=nm),
        grid=(b_, l_ // tm, nm + 1),
        in_specs=[row, vec, vec,
                  pl.BlockSpec((d_, d_), lambda b, i, j: (0, jnp.minimum(j, nm - 1))),
                  pl.BlockSpec((d_, LORA_PAD), lambda b, i, j: (0, 0))],
        out_specs=[pl.BlockSpec((1, tm, d_), lambda b, i, j: (b, i, jnp.minimum(j, nm - 1))),
                   pl.BlockSpec((1, tm, LORA_PAD), lambda b, i, j: (b, i, 0))],
        out_shape=[jax.ShapeDtypeStruct((b_, l_, nm * d_), F32), jax.ShapeDtypeStruct((b_, l_, LORA_PAD), F32)],
        scratch_shapes=[pltpu.VMEM((tm, d_), BF16)],
        compiler_params=pltpu.CompilerParams(vmem_limit_bytes=VMEM_LIMIT),
        name="inproj",
    )(x, sc, sh, w_main, w_lora)


def _mix_kernel(cur_ref, prev_ref, next_ref, lcur_ref, lprev_ref, lnext_ref, mu_ref, mul_ref, kk_ref,
                mixed_ref, lz_ref, *, grid_mode):
    i = pl.program_id(1)
    n_i = pl.num_programs(1)
    tm = cur_ref.shape[1]
    d_ = kk_ref.shape[1]

    def shifted(cur, prev, nxt):
        rows, ch = cur.shape
        rowi = lax.broadcasted_iota(jnp.int32, (rows, ch), 0)
        lane = lax.broadcasted_iota(jnp.int32, (rows, ch), 1)
        if grid_mode:
            prev = jnp.where(i > 0, prev, 0.0)
            nxt = jnp.where(i < n_i - 1, nxt, 0.0)
            up = jnp.concatenate([prev, cur[:rows - GRID_W]], axis=0)
            down = jnp.concatenate([cur[GRID_W:], nxt], axis=0)
            col = rowi % GRID_W
            left = jnp.where(col == 0, 0.0, pltpu.roll(cur, 1, 0))
            right = jnp.where(col == GRID_W - 1, 0.0, pltpu.roll(cur, rows - 1, 0))
            l4 = lane % 4
            return jnp.where(l4 == 0, left, jnp.where(l4 == 1, right, jnp.where(l4 == 2, up, down)))
        before = jnp.where(rowi == 0, 0.0, pltpu.roll(cur, 1, 0))
        after = jnp.where(rowi == rows - 1, 0.0, pltpu.roll(cur, rows - 1, 0))
        return jnp.where(lane % 2 == 0, before, after)

    def mixed(cur, prev, nxt, mu):
        return cur + mu * (shifted(cur, prev, nxt) - cur)

    for s in range(3):
        sl = slice(s * d_, (s + 1) * d_)
        z = mixed(cur_ref[0, :, sl], prev_ref[0, :, sl], next_ref[0, :, sl], mu_ref[:, sl])
        mixed_ref[0, :, sl] = z.astype(BF16)
        if s == 1:
            kq = z * kk_ref[...]
            kq = kq * lax.rsqrt(_head_sums(kq * kq) + 1e-12)
            mixed_ref[0, :, 3 * d_:4 * d_] = kq.astype(BF16)
    lz = mixed(lcur_ref[0], lprev_ref[0], lnext_ref[0], mul_ref[...])
    wa = lz[:, :LORA_DECAY + LORA_AAA]
    lane = lax.broadcasted_iota(jnp.int32, wa.shape, 1)
    lz_ref[0, :, :LORA_DECAY + LORA_AAA] = jnp.where(lane < LORA_DECAY, jnp.tanh(wa), wa).astype(BF16)
    lz_ref[0, :, LORA_DECAY + LORA_AAA:] = _sigmoid(lz[:, LORA_DECAY + LORA_AAA:]).astype(BF16)


def rwkv_mix(main, lora, mu, mu_l, k_k, grid_mode, tm):
    b_, l_, _ = main.shape
    d_ = k_k.shape[-1]
    if grid_mode:
        assert tm % GRID_W == 0 and l_ % tm == 0
        halo, per = GRID_W, tm // GRID_W
    else:
        assert tm == l_
        halo, per = 8, tm // 8
    n_h = l_ // halo
    cur = lambda w: pl.BlockSpec((1, tm, w), lambda b, i: (b, i, 0))
    prv = lambda w: pl.BlockSpec((1, halo, w), lambda b, i: (b, jnp.maximum(i * per - 1, 0), 0))
    nxt = lambda w: pl.BlockSpec((1, halo, w), lambda b, i: (b, jnp.minimum((i + 1) * per, n_h - 1), 0))
    par = lambda w: pl.BlockSpec((1, w), lambda b, i: (0, 0))
    return pl.pallas_call(
        functools.partial(_mix_kernel, grid_mode=grid_mode),
        grid=(b_, l_ // tm),
        in_specs=[cur(3 * d_), prv(3 * d_), nxt(3 * d_), cur(LORA_PAD), prv(LORA_PAD), nxt(LORA_PAD),
                  par(3 * d_), par(LORA_PAD), par(d_)],
        out_specs=[cur(4 * d_), cur(LORA_PAD)],
        out_shape=[jax.ShapeDtypeStruct((b_, l_, 4 * d_), BF16), jax.ShapeDtypeStruct((b_, l_, LORA_PAD), BF16)],
        compiler_params=pltpu.CompilerParams(vmem_limit_bytes=VMEM_LIMIT),
        name="rwkv_mix",
    )(main, main, main, lora, lora, lora, mu, mu_l, k_k)


def _wkv_kernel(rf_ref, kf_ref, vf_ref, kkf_ref, lzf_ref, rb_ref, kb_ref, vb_ref, kkb_ref, lzb_ref,
                wlo_ref, wa0_ref, ka_ref, s0_ref, yf_ref, yb_ref, sT_ref, s_scr):
    c = pl.program_id(1)
    n_c = pl.num_programs(1)
    cs, d_ = rf_ref.shape[1], rf_ref.shape[2]
    pairs = s_scr.shape[1]
    pw = 2 * HEAD

    @pl.when(c == 0)
    def _():
        s_scr[...] = s0_ref[:, 0]

    row = lax.broadcasted_iota(jnp.int32, (cs, cs), 0)
    col = lax.broadcasted_iota(jnp.int32, (cs, cs), 1)
    row2 = lax.broadcasted_iota(jnp.int32, (2 * cs, pw), 0)
    col2 = lax.broadcasted_iota(jnp.int32, (2 * cs, pw), 1)
    same_head = (row2 >= cs) == (col2 >= HEAD)
    eye = jnp.where(row2 == col2, 1.0, 0.0)
    lane = lax.broadcasted_iota(jnp.int32, (cs, pw), 1)
    m0, m1 = lane < HEAD, lane >= HEAD
    mm0, mm1 = col2 < HEAD, col2 >= HEAD
    cat = lambda x, y: jnp.concatenate([x, y], axis=0)
    dot = lambda x, y, dims=_NN: lax.dot_general(x, y, dims, preferred_element_type=F32)
    sel = lambda m, x: jnp.where(m, x, jnp.zeros_like(x))

    streams = []
    for d, (r_ref, k_ref, v_ref, kk_ref, lz_ref, y_ref) in enumerate(
            ((rf_ref, kf_ref, vf_ref, kkf_ref, lzf_ref, yf_ref), (rb_ref, kb_ref, vb_ref, kkb_ref, lzb_ref, yb_ref))):
        tdiff = (row2 % cs - col2 % cs) * (1 - 2 * d)
        amask = (tdiff > 0) | ((row2 >= cs) & (tdiff == 0))
        tri = jnp.where((row - col) * (1 - 2 * d) >= 0, 1.0, 0.0).astype(BF16)
        z = _dot(lz_ref[0, :, :LORA_DECAY + LORA_AAA], wlo_ref[d]) + wa0_ref[d]
        lw = -jnp.exp(-_softplus(-z[:, :d_]) - 0.5)
        a = _sigmoid(z[:, d_:])
        r, kk = r_ref[0].astype(F32), kk_ref[0].astype(F32)
        k = k_ref[0].astype(F32) * (1.0 + (a - 1.0) * ka_ref[...])
        cum = _dot(tri, lw, _NN, 1, 3)
        tot = jnp.sum(lw, axis=0, keepdims=True)
        b = kk * a
        e_neg = jnp.exp(-cum)
        e_end = jnp.exp(tot - cum)
        rt = (r * jnp.exp(cum)).astype(BF16)
        at = (-kk * jnp.exp(cum - lw)).astype(BF16)
        bt = (b * e_neg).astype(BF16)
        kt = (k * e_neg).astype(BF16)
        bd = (b * e_end).astype(BF16)
        kd = (k * e_end).astype(BF16)
        vb = v_ref[0]
        p_end = jnp.exp(tot)
        for p in range(pairs):
            sl = slice(p * pw, (p + 1) * pw)
            streams.append(dict(d=d, p=p, sl=sl, y_ref=y_ref, amask=amask, ar=cat(at[:, sl], rt[:, sl]),
                                bt=bt[:, sl], kt=kt[:, sl], bdkd=cat(bd[:, sl], kd[:, sl]), v=vb[:, sl],
                                p_end=p_end[:, sl]))

    for st in streams:
        q = dot(cat(sel(mm0, st["ar"]), sel(mm1, st["ar"])),
                cat(jnp.where(m0, st["bt"], st["kt"]), jnp.where(m0, st["kt"], st["bt"])), _NT)
        q0, q1 = sel(st["amask"], q[:2 * cs]), sel(st["amask"], q[2 * cs:])
        st["nil"] = cat(sel(m0, q0[:cs]), sel(m1, q1[:cs]))
        st["q0"], st["q1"] = q0.astype(BF16), q1.astype(BF16)
    for st in streams:
        st["x"] = dot(jnp.where(m0, st["q1"][:cs], st["q0"][:cs]), cat(sel(m1, st["v"]), sel(m0, st["v"])))
    for st in streams:
        nb = st["nil"].astype(BF16)
        st["t"] = eye + st["nil"]
        st["nb"] = dot(nb, nb).astype(BF16)
    for i in range(max(cs.bit_length() - 3, 0)):
        for st in streams:
            xx = dot(cat(st["t"].astype(BF16), st["nb"]), st["nb"])
            st["t"] = st["t"] + xx[:2 * cs]
            st["nb"] = xx[2 * cs:].astype(BF16)
    for st in streams:
        st["t"] = (st["t"] + dot(st["t"].astype(BF16), st["nb"])).astype(BF16)

    for st in streams:
        st["s"] = s_scr[st["d"], st["p"]]
        st["as"] = dot(st["ar"], st["s"].astype(BF16))
    for st in streams:
        rhs = (st["as"][:cs] + st["x"]).astype(BF16)
        uu = dot(st["t"], cat(sel(m0, rhs), sel(m1, rhs)))
        st["u"] = (uu[:cs] + uu[cs:]).astype(BF16)
    for st in streams:
        u_p, v_p = st["u"], st["v"]
        y = (st["as"][cs:] + dot(st["q0"][cs:], cat(sel(m0, u_p), sel(m0, v_p)))
             + dot(st["q1"][cs:], cat(sel(m1, v_p), sel(m1, u_p))))
        st["y_ref"][0, :, st["sl"]] = y
        p_col = jnp.broadcast_to(st["p_end"], (pw, pw)).T
        upd = dot(st["bdkd"], cat(u_p, v_p), _TN)
        s_scr[st["d"], st["p"]] = p_col * st["s"] + sel(same_head, upd)

    @pl.when(c == n_c - 1)
    def _():
        sT_ref[:, 0] = s_scr[...]


def _pair_states(s):
    st = jnp.swapaxes(s, -1, -2)
    st = st.reshape(s.shape[:-3] + (s.shape[-3] // 2, 2, HEAD, HEAD))
    eye = jnp.eye(2, dtype=s.dtype)
    out = jnp.einsum("...pakv,ab->...pakbv", st, eye)
    return out.reshape(s.shape[:-3] + (s.shape[-3] // 2, 2 * HEAD, 2 * HEAD))


def wkv7(mixed, lz, w_lo, wa0, k_a, s0p):
    b_, l_, d4 = mixed.shape
    d_ = d4 // 4
    pairs, pw = d_ // (2 * HEAD), 2 * HEAD
    cs = WKV_CHUNK
    assert cs == HEAD and l_ % cs == 0
    n_c = l_ // cs

    fwd = lambda w, j: pl.BlockSpec((1, cs, w), lambda b, c: (b, c, j))
    bwd = lambda w, j: pl.BlockSpec((1, cs, w), lambda b, c: (b, n_c - 1 - c, j))
    full = lambda a: pl.BlockSpec(a.shape, lambda b, c: (0,) * a.ndim)
    state = pl.BlockSpec((2, 1, pairs, pw, pw), lambda b, c: (0, b, 0, 0, 0))
    return pl.pallas_call(
        _wkv_kernel,
        grid=(b_, n_c),
        in_specs=[fwd(d_, 0), fwd(d_, 1), fwd(d_, 2), fwd(d_, 3), fwd(LORA_PAD, 0),
                  bwd(d_, 0), bwd(d_, 1), bwd(d_, 2), bwd(d_, 3), bwd(LORA_PAD, 0),
                  full(w_lo), full(wa0), full(k_a), state],
        out_specs=[fwd(d_, 0), bwd(d_, 0), state],
        out_shape=[jax.ShapeDtypeStruct((b_, l_, d_), F32), jax.ShapeDtypeStruct((b_, l_, d_), F32),
                   jax.ShapeDtypeStruct((2, b_, pairs, pw, pw), F32)],
        scratch_shapes=[pltpu.VMEM((2, pairs, pw, pw), F32)],
        compiler_params=pltpu.CompilerParams(
            dimension_semantics=("arbitrary", "arbitrary"), vmem_limit_bytes=VMEM_LIMIT),
        name="wkv7",
    )(mixed, mixed, mixed, mixed, lz, mixed, mixed, mixed, mixed, lz, w_lo, wa0, k_a, s0p)


def _lru_kernel(cur_ref, prev_ref, next_ref, cw_ref, cb_ref, wg_ref, bg_ref, cl_ref, h0_ref,
                h_ref, hT_ref, a_scr, u_scr, h_scr):
    d = pl.program_id(0)
    c = pl.program_id(2)
    n_c = pl.num_programs(2)
    ti = c + d * (n_c - 1 - 2 * c)
    tm, d_ = a_scr.shape
    gw = GATE_GROUP

    @pl.when(c == 0)
    def _():
        h_scr[...] = h0_ref[0, 0]

    prev = jnp.where(ti > 0, prev_ref[0], 0.0)
    nxt = jnp.where(ti < n_c - 1, next_ref[0], 0.0)
    ext = jnp.concatenate([prev, cur_ref[0], nxt], axis=0)
    xc = cb_ref[...]
    for j in range(CONV_W):
        o = HALO - CONV_W // 2 + j
        xc = xc + cw_ref[j:j + 1, :] * ext[o:o + tm]
    for g in range(d_ // gw):
        sl = slice(g * gw, (g + 1) * gw)
        z = _dot(xc[:, sl], wg_ref[0, g])
        rg = _sigmoid(z[:, :gw] + bg_ref[0, :, sl])
        ig = _sigmoid(z[:, gw:] + bg_ref[0, :, d_ + g * gw:d_ + (g + 1) * gw])
        log_a = cl_ref[0, :, sl] * rg
        a_scr[:, sl] = jnp.exp(log_a)
        u_scr[:, sl] = jnp.sqrt(1.0 - jnp.exp(2.0 * log_a)) * ig * xc[:, sl]

    rowid = lax.broadcasted_iota(jnp.int32, (8, d_), 0)

    def tile_scan(i, h, rev):
        t8 = (tm // 8 - 1 - i) if rev else i
        r0 = pl.multiple_of(t8 * 8, 8)
        a8 = a_scr[pl.ds(r0, 8), :]
        u8 = u_scr[pl.ds(r0, 8), :]
        for s in (1, 2, 4):
            ok = (rowid < 8 - s) if rev else (rowid >= s)
            sh = (8 - s) if rev else s
            a_sh = jnp.where(ok, pltpu.roll(a8, sh, 0), 1.0)
            u_sh = jnp.where(ok, pltpu.roll(u8, sh, 0), 0.0)
            u8 = a8 * u_sh + u8
            a8 = a8 * a_sh
        h8 = u8 + a8 * h
        h_ref[0, 0, pl.ds(r0, 8), :] = h8
        return h8[0:1] if rev else h8[7:8]

    @pl.when(d == 0)
    def _():
        h_scr[...] = lax.fori_loop(0, tm // 8, lambda i, h: tile_scan(i, h, False), h_scr[...], unroll=2)

    @pl.when(d == 1)
    def _():
        h_scr[...] = lax.fori_loop(0, tm // 8, lambda i, h: tile_scan(i, h, True), h_scr[...], unroll=2)

    @pl.when(c == n_c - 1)
    def _():
        hT_ref[0, 0] = h_scr[...]


def rglru(main, xb_col, conv_w, conv_b, w_gate, b_gate, c_lam, h0, tm):
    b_, l_, _ = main.shape
    d_ = conv_w.shape[1]
    tm = min(tm, l_)
    n_c = l_ // tm
    per = tm // HALO
    n_h = l_ // HALO
    tmap = lambda d, c: c + d * (n_c - 1 - 2 * c)
    dirp = lambda shape: pl.BlockSpec((1,) + shape, lambda d, b, c: (d,) + (0,) * len(shape))
    st = pl.BlockSpec((1, 1, 1, d_), lambda d, b, c: (d, b, 0, 0))
    return pl.pallas_call(
        _lru_kernel,
        grid=(2, b_, n_c),
        in_specs=[pl.BlockSpec((1, tm, d_), lambda d, b, c: (b, tmap(d, c), xb_col)),
                  pl.BlockSpec((1, HALO, d_), lambda d, b, c: (b, jnp.maximum(tmap(d, c) * per - 1, 0), xb_col)),
                  pl.BlockSpec((1, HALO, d_), lambda d, b, c: (b, jnp.minimum((tmap(d, c) + 1) * per, n_h - 1), xb_col)),
                  pl.BlockSpec((CONV_W, d_), lambda d, b, c: (0, 0)),
                  pl.BlockSpec((1, d_), lambda d, b, c: (0, 0)),
                  dirp(w_gate.shape[1:]), dirp((1, 2 * d_)), dirp((1, d_)), st],
        out_specs=[pl.BlockSpec((1, 1, tm, d_), lambda d, b, c: (d, b, tmap(d, c), 0)), st],
        out_shape=[jax.ShapeDtypeStruct((2, b_, l_, d_), F32), jax.ShapeDtypeStruct((2, b_, 1, d_), F32)],
        scratch_shapes=[pltpu.VMEM((tm, d_), F32), pltpu.VMEM((tm, d_), F32), pltpu.VMEM((1, d_), F32)],
        compiler_params=pltpu.CompilerParams(
            dimension_semantics=("arbitrary", "arbitrary", "arbitrary"), vmem_limit_bytes=VMEM_LIMIT),
        name="rglru",
    )(main, main, main, conv_w, conv_b, w_gate, b_gate, c_lam, h0)


def _post_kernel(yf_ref, yb_ref, r_ref, k_ref, v_ref, lz_ref, hf_ref, hb_ref, gb_ref, ga_ref, gm_ref, x_ref,
                 gt_ref, sc_ref, sh_ref, wa2_ref, a0_ref, g2_ref, ka_ref, rk_ref, gg_ref, gnb_ref,
                 pa_ref, pb_ref, wo_ref, lg_ref, lb_ref, rwh_ref, rwl_ref, rb_ref,
                 x1_ref, h2_ref, lg_out_ref):
    d_ = x_ref.shape[2]
    y = yf_ref[0] + yb_ref[0]
    mu = _head_sums(y) * (1.0 / HEAD)
    yc = y - mu
    var = _head_sums(yc * yc) * (1.0 / HEAD)
    yn = yc * lax.rsqrt(var + GN_EPS) * gg_ref[...] + gnb_ref[...]
    lz = lz_ref[0]
    a2 = _sigmoid(_dot(lz[:, :LORA_DECAY + LORA_AAA], wa2_ref[...]) + a0_ref[...])
    r, k, v = r_ref[0].astype(F32), k_ref[0].astype(F32), v_ref[0].astype(F32)
    ksum = k * (2.0 + (a2[:, :d_] + a2[:, d_:] - 2.0) * ka_ref[...])
    bonus = _head_sums(r * ksum * rk_ref[...]) * v
    g = _dot(lz[:, LORA_DECAY + LORA_AAA:], g2_ref[...])
    y_a = ((yn + bonus) * g).astype(BF16)
    gb = gb_ref[0]
    gelu = 0.5 * gb * (1.0 + jnp.tanh(0.7978845608028654 * (gb + 0.044715 * gb * gb * gb)))
    y_b = ((hf_ref[0, 0] + hb_ref[0, 0]) * gelu).astype(BF16)
    m = _sigmoid(ga_ref[0]) * _dot(y_a, pa_ref[...]) + _sigmoid(gm_ref[0]) * _dot(y_b, pb_ref[...])
    mix = _dot(m, wo_ref[...])
    x1 = _ln(ALPHA * _ln(x_ref[0]) + gt_ref[0] * mix) * lg_ref[...] + lb_ref[...]
    x1_ref[0] = x1
    h2 = x1 * (1.0 + sc_ref[0]) + sh_ref[0]
    h2_ref[0] = h2
    hh, hl = _split(h2, 2)
    lg_out_ref[0] = (_dot(hh, rwh_ref[...]) + _dot(hl, rwh_ref[...]) + _dot(hh, rwl_ref[...])) + rb_ref[...]


def post_mix(y_f, y_b, mixed, lz, hh, main, x, gt, sc, sh, params, tm):
    b_, l_, d_ = x.shape
    dsec = lambda e: pl.BlockSpec((1, 1, tm, d_), lambda b, i: (e, b, i, 0))
    col = lambda j: pl.BlockSpec((1, tm, d_), lambda b, i: (b, i, j))
    vec = pl.BlockSpec((1, 1, d_), lambda b, i: (b, 0, 0))
    full = lambda a: pl.BlockSpec(a.shape, lambda b, i: (0,) * a.ndim)
    return pl.pallas_call(
        _post_kernel,
        grid=(b_, l_ // tm),
        in_specs=[col(0), col(0), col(0), col(1), col(2),
                  pl.BlockSpec((1, tm, LORA_PAD), lambda b, i: (b, i, 0)),
                  dsec(0), dsec(1), col(4), col(5), col(6), col(0), vec, vec, vec] + [full(p) for p in params],
        out_specs=[col(0), col(0), pl.BlockSpec((1, tm, 128), lambda b, i: (b, i, 0))],
        out_shape=[jax.ShapeDtypeStruct((b_, l_, d_), F32), jax.ShapeDtypeStruct((b_, l_, d_), F32),
                   jax.ShapeDtypeStruct((b_, l_, 128), F32)],
        compiler_params=pltpu.CompilerParams(vmem_limit_bytes=VMEM_LIMIT),
        name="post_mix",
    )(y_f, y_b, mixed, mixed, mixed, lz, hh, hh, main, main, main, x, gt, sc, sh, *params)


def _mm_kernel(x_ref, w_ref, o_ref, *, pa, pb):
    o_ref[...] = _dot(x_ref[...], w_ref[...], _NN, pa, pb)


def pmm(x, w, tm=512, tn=1024, pa=1, pb=1):
    m_, k_ = x.shape
    n_ = w.shape[1]
    tm, tn = min(tm, m_), min(tn, n_)
    assert m_ % tm == 0 and n_ % tn == 0, (x.shape, w.shape, tm, tn)
    return pl.pallas_call(
        functools.partial(_mm_kernel, pa=pa, pb=pb),
        grid=(m_ // tm, n_ // tn),
        in_specs=[pl.BlockSpec((tm, k_), lambda i, j: (i, 0)), pl.BlockSpec((k_, tn), lambda i, j: (0, j))],
        out_specs=pl.BlockSpec((tm, tn), lambda i, j: (i, j)),
        out_shape=jax.ShapeDtypeStruct((m_, n_), F32),
        compiler_params=pltpu.CompilerParams(vmem_limit_bytes=VMEM_LIMIT),
        name="pmm",
    )(x, w)


ROUTE_LANES = 128
R_E, R_RANK, R_W = 0, 2, 4
DMA_UNROLL = 4


def _route_kernel(lg_ref, rec_ref, cnt_ref, carry):
    i = pl.program_id(0)
    tm = lg_ref.shape[0]

    @pl.when(i == 0)
    def _():
        carry[...] = jnp.zeros_like(carry)

    lg = lg_ref[...]
    lane = lax.broadcasted_iota(jnp.int32, lg.shape, 1)
    neg = -jnp.inf
    first = lambda m: jnp.min(jnp.where(m, lane, ROUTE_LANES), axis=1, keepdims=True)
    is_g = lane < N_GROUPS
    gmax = jnp.max(jnp.where(is_g, lg, neg), axis=1, keepdims=True)
    gsel = first(is_g & (lg == gmax))
    p_g = 1.0 / jnp.sum(jnp.where(is_g, jnp.exp(lg - gmax), 0.0), axis=1, keepdims=True)
    in_grp = (lane >= N_GROUPS) & (lane < N_GROUPS + N_EXPERTS) & ((lane - N_GROUPS) // EXPERTS_PER_GROUP == gsel)
    el = jnp.where(in_grp, lg, neg)
    v1 = jnp.max(el, axis=1, keepdims=True)
    i1 = first(in_grp & (el == v1))
    rest = in_grp & (lane != i1)
    el2 = jnp.where(rest, lg, neg)
    v2 = jnp.max(el2, axis=1, keepdims=True)
    i2 = first(rest & (el2 == v2))
    e21 = jnp.exp(v2 - v1)
    w1 = p_g / (1.0 + e21)
    w2 = w1 * e21
    oh1, oh2 = lane == i1, lane == i2
    both = jnp.where(oh1 | oh2, 1.0, 0.0)
    row = lax.broadcasted_iota(jnp.int32, (tm, tm), 0)
    col = lax.broadcasted_iota(jnp.int32, (tm, tm), 1)
    before = jnp.where(col < row, 1.0, 0.0).astype(BF16)
    cnt = _dot(before, both.astype(BF16)) + carry[...]
    rank1 = jnp.sum(jnp.where(oh1, cnt, 0.0), axis=1, keepdims=True)
    rank2 = jnp.sum(jnp.where(oh2, cnt, 0.0), axis=1, keepdims=True)
    carry[...] = carry[...] + jnp.sum(both, axis=0, keepdims=True)
    rec = jnp.zeros_like(lg)
    for k, val in ((R_E, (i1 - N_GROUPS).astype(F32)), (R_E + 1, (i2 - N_GROUPS).astype(F32)),
                   (R_RANK, rank1), (R_RANK + 1, rank2), (R_W, w1), (R_W + 1, w2)):
        rec = jnp.where(lane == k, val, rec)
    rec_ref[...] = rec
    cnt_ref[...] = carry[...]


def route(logits, tm=256):
    t_ = logits.shape[0]
    return pl.pallas_call(
        _route_kernel,
        grid=(t_ // tm,),
        in_specs=[pl.BlockSpec((tm, ROUTE_LANES), lambda i: (i, 0))],
        out_specs=[pl.BlockSpec((tm, ROUTE_LANES), lambda i: (i, 0)), pl.BlockSpec((1, ROUTE_LANES), lambda i: (0, 0))],
        out_shape=[jax.ShapeDtypeStruct((t_, ROUTE_LANES), F32), jax.ShapeDtypeStruct((1, ROUTE_LANES), F32)],
        scratch_shapes=[pltpu.VMEM((1, ROUTE_LANES), F32)],
        compiler_params=pltpu.CompilerParams(dimension_semantics=("arbitrary",)),
        name="route",
    )(logits)


def _dispatch_kernel(dest_ref, h_ref, xs_in_ref, xs_ref, sem):
    del xs_in_ref
    tm = h_ref.shape[0]
    base = pl.program_id(0) * tm * TOP_K

    def copy(r, s):
        return pltpu.make_async_copy(h_ref.at[pl.ds(r, 1)], xs_ref.at[pl.ds(dest_ref[base + r * TOP_K + s], 1)], sem)

    def start(r, carry):
        for s in range(TOP_K):
            copy(r, s).start()
        return carry

    def wait(r, carry):
        for s in range(TOP_K):
            copy(r, s).wait()
        return carry

    lax.fori_loop(0, tm, start, 0, unroll=DMA_UNROLL)
    lax.fori_loop(0, tm, wait, 0, unroll=DMA_UNROLL)


def dispatch(dest, h, n_pad, tm=512):
    t_, d_ = h.shape
    grid_spec = pltpu.PrefetchScalarGridSpec(
        num_scalar_prefetch=1,
        grid=(t_ // tm,),
        in_specs=[pl.BlockSpec((tm, d_), lambda i, dest: (i, 0)), pl.BlockSpec(memory_space=pl.ANY)],
        out_specs=pl.BlockSpec(memory_space=pl.ANY),
        scratch_shapes=[pltpu.SemaphoreType.DMA(())],
    )
    return pl.pallas_call(
        _dispatch_kernel,
        grid_spec=grid_spec,
        out_shape=jax.ShapeDtypeStruct((n_pad, d_), F32),
        input_output_aliases={2: 0},
        compiler_params=pltpu.CompilerParams(dimension_semantics=("arbitrary",), has_side_effects=True),
        name="dispatch",
    )(dest, h, jnp.zeros((n_pad, d_), F32))


def _moe_kernel(be_ref, nv_ref, x_ref, w1_ref, w3_ref, w2_ref, o_ref, w1b, w3b, w2b):
    blk = pl.program_id(0)

    @pl.when((blk == 0) | (be_ref[blk] != be_ref[jnp.maximum(blk - 1, 0)]))
    def _():
        w1b[...] = w1_ref[0].astype(BF16)
        w3b[...] = w3_ref[0].astype(BF16)
        w2b[...] = w2_ref[0].astype(BF16)

    @pl.when(nv_ref[blk] > 0)
    def _():
        x = x_ref[...].astype(BF16)
        h1 = _dot(x, w1b[...])
        h3 = _dot(x, w3b[...])
        hh = h1 * _sigmoid(h1) * h3
        o_ref[...] = _dot(hh, w2b[...])

    @pl.when(nv_ref[blk] == 0)
    def _():
        o_ref[...] = jnp.zeros_like(o_ref)


def moe_experts(xs, blk_e, blk_n, w1, w3, w2):
    n_pad, d_ = xs.shape
    n_blocks = n_pad // MOE_ROWS
    de = w1.shape[2]
    grid_spec = pltpu.PrefetchScalarGridSpec(
        num_scalar_prefetch=2,
        grid=(n_blocks,),
        in_specs=[pl.BlockSpec((MOE_ROWS, d_), lambda i, be, nv: (i, 0)),
                  pl.BlockSpec((1, d_, de), lambda i, be, nv: (be[i], 0, 0)),
                  pl.BlockSpec((1, d_, de), lambda i, be, nv: (be[i], 0, 0)),
                  pl.BlockSpec((1, de, d_), lambda i, be, nv: (be[i], 0, 0))],
        out_specs=pl.BlockSpec((MOE_ROWS, d_), lambda i, be, nv: (i, 0)),
        scratch_shapes=[pltpu.VMEM((d_, de), BF16), pltpu.VMEM((d_, de), BF16), pltpu.VMEM((de, d_), BF16)],
    )
    return pl.pallas_call(
        _moe_kernel,
        grid_spec=grid_spec,
        out_shape=jax.ShapeDtypeStruct((n_pad, d_), F32),
        compiler_params=pltpu.CompilerParams(dimension_semantics=("arbitrary",), vmem_limit_bytes=VMEM_LIMIT),
        name="moe_experts",
    )(blk_e, blk_n, xs, w1, w3, w2)


def _combine_kernel(dest_ref, x_ref, rec_ref, gt_ref, g_ref, b_ref, ys_ref, o_ref, buf, sem):
    tm = x_ref.shape[1]
    base = (pl.program_id(0) * pl.num_programs(1) + pl.program_id(1)) * tm * TOP_K

    def copy(r, s):
        return pltpu.make_async_copy(ys_ref.at[pl.ds(dest_ref[base + r * TOP_K + s], 1)], buf.at[s, pl.ds(r, 1)], sem)

    def start(r, carry):
        for s in range(TOP_K):
            copy(r, s).start()
        return carry

    def wait(r, carry):
        for s in range(TOP_K):
            copy(r, s).wait()
        return carry

    lax.fori_loop(0, tm, start, 0, unroll=DMA_UNROLL)
    lax.fori_loop(0, tm, wait, 0, unroll=DMA_UNROLL)
    rec = rec_ref[...]
    moe = rec[:, R_W:R_W + 1] * buf[0]
    for s in range(1, TOP_K):
        moe = moe + rec[:, R_W + s:R_W + s + 1] * buf[s]
    o_ref[0] = _ln(ALPHA * x_ref[0] + gt_ref[0] * moe) * g_ref[...] + b_ref[...]


def combine_ln(dest, x, rec, gt, g, b, ys, tm=512):
    b_, l_, d_ = x.shape
    n_i = l_ // tm
    grid_spec = pltpu.PrefetchScalarGridSpec(
        num_scalar_prefetch=1,
        grid=(b_, n_i),
        in_specs=[pl.BlockSpec((1, tm, d_), lambda bi, i, dest: (bi, i, 0)),
                  pl.BlockSpec((tm, ROUTE_LANES), lambda bi, i, dest: (bi * n_i + i, 0)),
                  pl.BlockSpec((1, 1, d_), lambda bi, i, dest: (bi, 0, 0)),
                  pl.BlockSpec((1, d_), lambda bi, i, dest: (0, 0)),
                  pl.BlockSpec((1, d_), lambda bi, i, dest: (0, 0)),
                  pl.BlockSpec(memory_space=pl.ANY)],
        out_specs=pl.BlockSpec((1, tm, d_), lambda bi, i, dest: (bi, i, 0)),
        scratch_shapes=[pltpu.VMEM((TOP_K, tm, d_), F32), pltpu.SemaphoreType.DMA(())],
    )
    return pl.pallas_call(
        _combine_kernel,
        grid_spec=grid_spec,
        out_shape=jax.ShapeDtypeStruct(x.shape, F32),
        compiler_params=pltpu.CompilerParams(
            dimension_semantics=("arbitrary", "arbitrary"), vmem_limit_bytes=VMEM_LIMIT),
        name="combine_ln",
    )(dest, x, rec, gt, g.reshape(1, d_), b.reshape(1, d_), ys)


def hier_moe_ln(x1, h2, logits, gt, g, b, w1, w3, w2):
    b_, l_, d_ = x1.shape
    t_ = b_ * l_
    rec, cnt = route(logits)
    counts = cnt[0, N_GROUPS:N_GROUPS + N_EXPERTS].astype(jnp.int32)
    padded = (counts + MOE_ROWS - 1) // MOE_ROWS * MOE_ROWS
    pad_end = jnp.cumsum(padded)
    pad_start = pad_end - padded
    eid = rec[:, R_E:R_E + TOP_K].astype(jnp.int32)
    rank = rec[:, R_RANK:R_RANK + TOP_K].astype(jnp.int32)
    onehot = eid[..., None] == jnp.arange(N_EXPERTS, dtype=jnp.int32)
    dest = (rank + jnp.sum(jnp.where(onehot, pad_start, 0), -1)).reshape(-1)
    n_blocks = -(-(t_ * TOP_K) // MOE_ROWS) + N_EXPERTS
    blk_lo = jnp.arange(n_blocks, dtype=jnp.int32) * MOE_ROWS
    blk_e = jnp.minimum(jnp.sum(blk_lo[:, None] >= pad_end[None, :], -1), N_EXPERTS - 1).astype(jnp.int32)
    blk_n = jnp.clip(jnp.sum(jnp.where(blk_e[:, None] == jnp.arange(N_EXPERTS), pad_start + counts, 0), -1) - blk_lo,
                     0, MOE_ROWS).astype(jnp.int32)
    xs = dispatch(dest, h2.reshape(t_, d_), n_blocks * MOE_ROWS)
    ys = moe_experts(xs, blk_e, blk_n, w1, w3, w2)
    return combine_ln(dest, x1, rec, gt, g, b, ys)


def _block_diag_groups(w, grp):
    n = w.shape[0]
    wg = w.reshape(n // grp, grp, HEAD, HEAD)
    eye = jnp.eye(grp, dtype=w.dtype)
    return jnp.einsum("gaij,ab->gaibj", wg, eye).reshape(n // grp, grp * HEAD, grp * HEAD)


def token_scans(h_in, sc, sh, wts, grid_mode, states):
    b_, l_, d_ = h_in.shape
    main, lora = inproj(h_in, sc, sh, wts["w_main"], wts["w_lora"], min(1024, l_))
    mixed, lz = rwkv_mix(main, lora, wts["mu"], wts["mu_l"], wts["k_k"], grid_mode, 256 if grid_mode else l_)
    y_f, y_b, s_new = wkv7(mixed, lz, wts["w_lo"], wts["wa0"], wts["k_a"], states[0])
    hh, h_new = rglru(main, 3, wts["conv_w"], wts["conv_b"], wts["w_gate"], wts["b_gate"], wts["c_lam"], states[1], 512)
    return (main, mixed, lz, y_f, y_b, hh), (s_new, h_new)


def kernel(x, c, ctx, c_ctx, w_ada, b_ada, w_in, mu_a, w0, w2, a0, a2, g2, k_k, k_a, r_k, gn_g, gn_b, conv_w, conv_b, lru_wa, lru_ba, lru_wx, lru_bx, lru_lam, p_a, p_b, w_o, ln1_g, ln1_b, router_g, router_g_b, router_e, router_e_b, e_w1, e_w3, e_w2, ln2_g, ln2_b):
    b_, l_, d_ = x.shape
    heads = d_ // HEAD
    l = 0
    row = lambda v: v.reshape(1, -1)
    cc = jnp.concatenate([c, c_ctx[None]], 0)
    cc = jnp.pad(jax.nn.silu(cc), ((0, 8 - cc.shape[0]), (0, 0)))
    mod = pmm(cc, w_ada[l], 8, 1024, 2, 2)[:b_ + 1] + b_ada[l]
    mods = jnp.split(mod, 6, axis=-1)
    sh1, sc1, gt1, sh2, sc2, gt2 = [m[:b_, None, :] for m in mods]
    csh1, csc1 = [jnp.broadcast_to(m[b_:, None, :], (b_, 1, d_)) for m in mods[:2]]

    a_slab = 3 * d_ + LORA_DECAY + LORA_AAA + LORA_GATE
    n_lora = a_slab - 3 * d_
    wi, mu = w_in[l], mu_a[l]
    zeros_lo = jnp.zeros((LORA_DECAY, d_), F32)
    w_lo = jnp.stack([jnp.concatenate([jnp.concatenate([w2[l, e], zeros_lo], 1),
                                       jnp.concatenate([zeros_lo, a2[l, e]], 1)], 0) for e in range(2)])
    wts = dict(
        w_main=jnp.concatenate([wi[:, :3 * d_], wi[:, a_slab:]], 1).astype(BF16),
        w_lora=jnp.pad(wi[:, 3 * d_:a_slab], ((0, 0), (0, LORA_PAD - n_lora))).astype(BF16),
        mu=row(mu[:3 * d_]), mu_l=row(jnp.pad(mu[3 * d_:], (0, LORA_PAD - n_lora))), k_k=row(k_k[l]), k_a=row(k_a[l]),
        w_lo=w_lo.astype(BF16),
        wa0=jnp.concatenate([w0[l], a0[l]], -1)[:, None, :],
        conv_w=conv_w[l], conv_b=row(conv_b[l]),
        w_gate=jnp.stack([jnp.concatenate([_block_diag_groups(lru_wa[l, e], 4), _block_diag_groups(lru_wx[l, e], 4)], -1)
                          for e in range(2)]).astype(BF16),
        b_gate=jnp.concatenate([lru_ba[l], lru_bx[l]], -1)[:, None, :],
        c_lam=(-LRU_C * jax.nn.softplus(-lru_lam[l]))[:, None, :])

    s0 = jnp.zeros((2, b_, heads // 2, 2 * HEAD, 2 * HEAD), F32)
    h0 = jnp.zeros((2, b_, 1, d_), F32)
    _, ctx_states = token_scans(ctx, csc1, csh1, wts, False, (s0, h0))
    (main, mixed, lz, y_f, y_b, hh), _ = token_scans(x, sc1, sh1, wts, True, ctx_states)

    zeros_a = jnp.zeros((LORA_DECAY, 2 * d_), F32)
    rw = jnp.pad(jnp.concatenate([router_g[l], router_e[l]], 1), ((0, 0), (0, 128 - N_GROUPS - N_EXPERTS)))
    rw_hi = rw.astype(BF16)
    params = [
        jnp.concatenate([zeros_a, jnp.concatenate([a2[l, 0], a2[l, 1]], 1)], 0).astype(BF16),
        row(jnp.concatenate([a0[l, 0], a0[l, 1]])),
        jnp.pad(g2[l], ((0, LORA_PAD - LORA_DECAY - LORA_AAA - LORA_GATE), (0, 0))).astype(BF16),
        row(k_a[l]), row(r_k[l]), row(gn_g[l]), row(gn_b[l]),
        p_a[l].astype(BF16), p_b[l].astype(BF16), w_o[l].astype(BF16), row(ln1_g[l]), row(ln1_b[l]),
        rw_hi, (rw - rw_hi.astype(F32)).astype(BF16),
        row(jnp.pad(jnp.concatenate([router_g_b[l], router_e_b[l]]), (0, 128 - N_GROUPS - N_EXPERTS)))]
    x1, h2, logits = post_mix(y_f, y_b, mixed, lz, hh, main, x, gt1, sc2, sh2, params, 256)

    return hier_moe_ln(x1, h2, logits.reshape(b_ * l_, -1), gt2, ln2_g[l], ln2_b[l], e_w1[l], e_w3[l], e_w2[l])
```

```python
import functools

import jax
import jax.numpy as jnp
from jax import lax
from jax.experimental import pallas as pl
from jax.experimental.pallas import tpu as pltpu

F32 = jnp.float32
BF16 = jnp.bfloat16

GRID_W = 64
HEAD = 64
LORA_DECAY = 64
LORA_AAA = 64
LORA_GATE = 160
LORA_PAD = 384
GN_EPS = 64e-5
LN_EPS = 1e-5
CONV_W = 5
LRU_C = 8.0
N_GROUPS = 4
EXPERTS_PER_GROUP = 8
N_EXPERTS = N_GROUPS * EXPERTS_PER_GROUP
TOP_K = 2
DEPTH = 1
ALPHA = (2 * DEPTH) ** 0.25

WKV_CHUNK = 64
MOE_ROWS = 256
GATE_GROUP = 4 * HEAD
HALO = 8
VMEM_LIMIT = 56 * 1024 * 1024

_NN = (((1,), (0,)), ((), ()))
_NT = (((1,), (1,)), ((), ()))
_TN = (((0,), (0,)), ((), ()))


def _split(x, n):
    if x.dtype == BF16:
        return [x]
    parts, rest = [], x.astype(F32)
    for i in range(n):
        p = rest.astype(BF16)
        parts.append(p)
        if i + 1 < n:
            rest = rest - p.astype(F32)
    return parts


def _dot(a, b, dims=_NN, pa=1, pb=1):
    ap, bp = _split(a, pa), _split(b, pb)
    order = max(len(ap), len(bp))
    acc = None
    for i, x in enumerate(ap):
        for j, y in enumerate(bp):
            if i + j < order:
                t = lax.dot_general(x, y, dims, preferred_element_type=F32)
                acc = t if acc is None else acc + t
    return acc


def _ln(x):
    mu = jnp.mean(x, -1, keepdims=True)
    xc = x - mu
    var = jnp.mean(xc * xc, -1, keepdims=True)
    return xc * lax.rsqrt(var + LN_EPS)


def _sigmoid(x):
    return 0.5 + 0.5 * jnp.tanh(0.5 * x)


def _softplus(x):
    return jnp.maximum(x, 0.0) + jnp.log(1.0 + jnp.exp(-jnp.abs(x)))


def _head_sums(x):
    pw = 2 * HEAD
    row = lax.broadcasted_iota(jnp.int32, (pw, pw), 0)
    col = lax.broadcasted_iota(jnp.int32, (pw, pw), 1)
    ones = jnp.where((row >= HEAD) == (col >= HEAD), 1.0, 0.0).astype(BF16)
    return jnp.concatenate([_dot(x[:, p:p + pw], ones, _NN, 2, 1) for p in range(0, x.shape[1], pw)], axis=1)


def _inproj_kernel(x_ref, sc_ref, sh_ref, wm_ref, wl_ref, main_ref, lora_ref, h_scr, *, nm):
    j = pl.program_id(2)

    @pl.when(j == 0)
    def _():
        h_scr[...] = (_ln(x_ref[0]) * (1.0 + sc_ref[0]) + sh_ref[0]).astype(BF16)

    @pl.when(j < nm)
    def _():
        main_ref[0] = _dot(h_scr[...], wm_ref[...])

    @pl.when(j == nm)
    def _():
        lora_ref[0] = _dot(h_scr[...], wl_ref[...])


def inproj(x, sc, sh, w_main, w_lora, tm):
    b_, l_, d_ = x.shape
    nm = w_main.shape[1] // d_
    row = pl.BlockSpec((1, tm, d_), lambda b, i, j: (b, i, 0))
    vec = pl.BlockSpec((1, 1, d_), lambda b, i, j: (b, 0, 0))
    return pl.pallas_call(
        functools.partial(_inproj_kernel, nm=nm),
        grid=(b_, l_ // tm, nm + 1),
        in_specs=[row, vec, vec,
                  pl.BlockSpec((d_, d_), lambda b, i, j: (0, jnp.minimum(j, nm - 1))),
                  pl.BlockSpec((d_, LORA_PAD), lambda b, i, j: (0, 0))],
        out_specs=[pl.BlockSpec((1, tm, d_), lambda b, i, j: (b, i, jnp.minimum(j, nm - 1))),
                   pl.BlockSpec((1, tm, LORA_PAD), lambda b, i, j: (b, i, 0))],
        out_shape=[jax.ShapeDtypeStruct((b_, l_, nm * d_), F32), jax.ShapeDtypeStruct((b_, l_, LORA_PAD), F32)],
        scratch_shapes=[pltpu.VMEM((tm, d_), BF16)],
        compiler_params=pltpu.CompilerParams(vmem_limit_bytes=VMEM_LIMIT),
        name="inproj",
    )(x, sc, sh, w_main, w_lora)


def _mix_kernel(cur_ref, prev_ref, next_ref, lcur_ref, lprev_ref, lnext_ref, mu_ref, mul_ref, kk_ref,
                mixed_ref, lz_ref, *, grid_mode):
    i = pl.program_id(1)
    n_i = pl.num_programs(1)
    tm = cur_ref.shape[1]
    d_ = kk_ref.shape[1]

    def shifted(cur, prev, nxt):
        rows, ch = cur.shape
        rowi = lax.broadcasted_iota(jnp.int32, (rows, ch), 0)
        lane = lax.broadcasted_iota(jnp.int32, (rows, ch), 1)
        if grid_mode:
            prev = jnp.where(i > 0, prev, 0.0)
            nxt = jnp.where(i < n_i - 1, nxt, 0.0)
            up = jnp.concatenate([prev, cur[:rows - GRID_W]], axis=0)
            down = jnp.concatenate([cur[GRID_W:], nxt], axis=0)
            col = rowi % GRID_W
            left = jnp.where(col == 0, 0.0, pltpu.roll(cur, 1, 0))
            right = jnp.where(col == GRID_W - 1, 0.0, pltpu.roll(cur, rows - 1, 0))
            l4 = lane % 4
            return jnp.where(l4 == 0, left, jnp.where(l4 == 1, right, jnp.where(l4 == 2, up, down)))
        before = jnp.where(rowi == 0, 0.0, pltpu.roll(cur, 1, 0))
        after = jnp.where(rowi == rows - 1, 0.0, pltpu.roll(cur, rows - 1, 0))
        return jnp.where(lane % 2 == 0, before, after)

    def mixed(cur, prev, nxt, mu):
        return cur + mu * (shifted(cur, prev, nxt) - cur)

    for s in range(3):
        sl = slice(s * d_, (s + 1) * d_)
        z = mixed(cur_ref[0, :, sl], prev_ref[0, :, sl], next_ref[0, :, sl], mu_ref[:, sl])
        mixed_ref[0, :, sl] = z.astype(BF16)
        if s == 1:
            kq = z * kk_ref[...]
            kq = kq * lax.rsqrt(_head_sums(kq * kq) + 1e-12)
            mixed_ref[0, :, 3 * d_:4 * d_] = kq.astype(BF16)
    lz = mixed(lcur_ref[0], lprev_ref[0], lnext_ref[0], mul_ref[...])
    wa = lz[:, :LORA_DECAY + LORA_AAA]
    lane = lax.broadcasted_iota(jnp.int32, wa.shape, 1)
    lz_ref[0, :, :LORA_DECAY + LORA_AAA] = jnp.where(lane < LORA_DECAY, jnp.tanh(wa), wa).astype(BF16)
    lz_ref[0, :, LORA_DECAY + LORA_AAA:] = _sigmoid(lz[:, LORA_DECAY + LORA_AAA:]).astype(BF16)


def rwkv_mix(main, lora, mu, mu_l, k_k, grid_mode, tm):
    b_, l_, _ = main.shape
    d_ = k_k.shape[-1]
    if grid_mode:
        assert tm % GRID_W == 0 and l_ % tm == 0
        halo, per = GRID_W, tm // GRID_W
    else:
        assert tm == l_
        halo, per = 8, tm // 8
    n_h = l_ // halo
    cur = lambda w: pl.BlockSpec((1, tm, w), lambda b, i: (b, i, 0))
    prv = lambda w: pl.BlockSpec((1, halo, w), lambda b, i: (b, jnp.maximum(i * per - 1, 0), 0))
    nxt = lambda w: pl.BlockSpec((1, halo, w), lambda b, i: (b, jnp.minimum((i + 1) * per, n_h - 1), 0))
    par = lambda w: pl.BlockSpec((1, w), lambda b, i: (0, 0))
    return pl.pallas_call(
        functools.partial(_mix_kernel, grid_mode=grid_mode),
        grid=(b_, l_ // tm),
        in_specs=[cur(3 * d_), prv(3 * d_), nxt(3 * d_), cur(LORA_PAD), prv(LORA_PAD), nxt(LORA_PAD),
                  par(3 * d_), par(LORA_PAD), par(d_)],
        out_specs=[cur(4 * d_), cur(LORA_PAD)],
        out_shape=[jax.ShapeDtypeStruct((b_, l_, 4 * d_), BF16), jax.ShapeDtypeStruct((b_, l_, LORA_PAD), BF16)],
        compiler_params=pltpu.CompilerParams(vmem_limit_bytes=VMEM_LIMIT),
        name="rwkv_mix",
    )(main, main, main, lora, lora, lora, mu, mu_l, k_k)


def _wkv_kernel(rf_ref, kf_ref, vf_ref, kkf_ref, lzf_ref, rb_ref, kb_ref, vb_ref, kkb_ref, lzb_ref,
                wlo_ref, wa0_ref, ka_ref, s0_ref, yf_ref, yb_ref, sT_ref, s_scr):
    c = pl.program_id(1)
    n_c = pl.num_programs(1)
    cs, d_ = rf_ref.shape[1], rf_ref.shape[2]
    pairs = s_scr.shape[1]
    pw = 2 * HEAD

    @pl.when(c == 0)
    def _():
        s_scr[...] = s0_ref[:, 0]

    row = lax.broadcasted_iota(jnp.int32, (cs, cs), 0)
    col = lax.broadcasted_iota(jnp.int32, (cs, cs), 1)
    row2 = lax.broadcasted_iota(jnp.int32, (2 * cs, pw), 0)
    col2 = lax.broadcasted_iota(jnp.int32, (2 * cs, pw), 1)
    same_head = (row2 >= cs) == (col2 >= HEAD)
    lane = lax.broadcasted_iota(jnp.int32, (cs, pw), 1)
    m0, m1 = lane < HEAD, lane >= HEAD
    mm0, mm1 = col2 < HEAD, col2 >= HEAD
    cat = lambda x, y: jnp.concatenate([x, y], axis=0)
    dot = lambda x, y, dims=_NN: lax.dot_general(x, y, dims, preferred_element_type=F32)
    sel = lambda m, x: jnp.where(m, x, jnp.zeros_like(x))

    streams = []
    for d, (r_ref, k_ref, v_ref, kk_ref, lz_ref, y_ref) in enumerate(
            ((rf_ref, kf_ref, vf_ref, kkf_ref, lzf_ref, yf_ref), (rb_ref, kb_ref, vb_ref, kkb_ref, lzb_ref, yb_ref))):
        tdiff = (row2 % cs - col2 % cs) * (1 - 2 * d)
        amask = (tdiff > 0) | ((row2 >= cs) & (tdiff == 0))
        tri = jnp.where((row - col) * (1 - 2 * d) >= 0, 1.0, 0.0).astype(BF16)
        z = _dot(lz_ref[0, :, :LORA_DECAY + LORA_AAA], wlo_ref[d]) + wa0_ref[d]
        lw = -jnp.exp(-_softplus(-z[:, :d_]) - 0.5)
        a = _sigmoid(z[:, d_:])
        r, kk = r_ref[0].astype(F32), kk_ref[0].astype(F32)
        k = k_ref[0].astype(F32) * (1.0 + (a - 1.0) * ka_ref[...])
        cum = _dot(tri, lw, _NN, 1, 3)
        tot = jnp.sum(lw, axis=0, keepdims=True)
        b = kk * a
        e_neg = jnp.exp(-cum)
        e_end = jnp.exp(tot - cum)
        rt = (r * jnp.exp(cum)).astype(BF16)
        at = (-kk * jnp.exp(cum - lw)).astype(BF16)
        bt = (b * e_neg).astype(BF16)
        kt = (k * e_neg).astype(BF16)
        bd = (b * e_end).astype(BF16)
        kd = (k * e_end).astype(BF16)
        vb = v_ref[0]
        p_end = jnp.exp(tot)
        for p in range(pairs):
            sl = slice(p * pw, (p + 1) * pw)
            streams.append(dict(d=d, p=p, sl=sl, y_ref=y_ref, amask=amask, ar=cat(at[:, sl], rt[:, sl]),
                                bt=bt[:, sl], kt=kt[:, sl], bdkd=cat(bd[:, sl], kd[:, sl]), v=vb[:, sl],
                                p_end=p_end[:, sl]))

    bd = lambda x: cat(sel(m0, x), sel(m1, x))
    lcat = lambda x, y: jnp.concatenate([x, y], axis=1)
    eye_p = jnp.where(lax.broadcasted_iota(jnp.int32, (cs, pw), 0) == lane % cs, 1.0, 0.0)
    for st in streams:
        q = dot(cat(sel(mm0, st["ar"]), sel(mm1, st["ar"])),
                cat(jnp.where(m0, st["bt"], st["kt"]), jnp.where(m0, st["kt"], st["bt"])), _NT)
        q0, q1 = sel(st["amask"], q[:2 * cs]), sel(st["amask"], q[2 * cs:])
        st["nil"] = jnp.where(m0, q0[:cs], q1[:cs])
        q0, q1 = q0.astype(BF16), q1.astype(BF16)
        st["ak"] = jnp.where(m0, q1[:cs], q0[:cs])
        st["rbk"] = lcat(q0[cs:], q1[cs:])
    for st in streams:
        nb = st["nil"].astype(BF16)
        st["t"] = eye_p + st["nil"]
        st["nb"] = dot(nb, bd(nb)).astype(BF16)
    for i in range(max(cs.bit_length() - 3, 0)):
        for st in streams:
            xx = dot(cat(st["t"].astype(BF16), st["nb"]), bd(st["nb"]))
            st["t"] = st["t"] + xx[:cs]
            st["nb"] = xx[cs:].astype(BF16)
    for st in streams:
        st["t"] = (st["t"] + dot(st["t"].astype(BF16), bd(st["nb"]))).astype(BF16)

    zero = jnp.zeros((cs, pw), BF16)
    for st in streams:
        st["s"] = s_scr[st["d"], st["p"]]
        st["as"] = dot(lcat(st["ar"], cat(st["ak"], zero)),
                       cat(st["s"].astype(BF16), cat(sel(m1, st["v"]), sel(m0, st["v"]))))
    for st in streams:
        st["u"] = dot(st["t"], bd(st["as"][:cs].astype(BF16))).astype(BF16)
    for st in streams:
        u_p, v_p = st["u"], st["v"]
        y = st["as"][cs:] + dot(st["rbk"], cat(cat(sel(m0, u_p), sel(m0, v_p)), cat(sel(m1, v_p), sel(m1, u_p))))
        st["y_ref"][0, :, st["sl"]] = y
        p_col = jnp.broadcast_to(st["p_end"], (pw, pw)).T
        upd = dot(st["bdkd"], cat(u_p, v_p), _TN)
        s_scr[st["d"], st["p"]] = p_col * st["s"] + sel(same_head, upd)

    @pl.when(c == n_c - 1)
    def _():
        sT_ref[:, 0] = s_scr[...]


def _pair_states(s):
    st = jnp.swapaxes(s, -1, -2)
    st = st.reshape(s.shape[:-3] + (s.shape[-3] // 2, 2, HEAD, HEAD))
    eye = jnp.eye(2, dtype=s.dtype)
    out = jnp.einsum("...pakv,ab->...pakbv", st, eye)
    return out.reshape(s.shape[:-3] + (s.shape[-3] // 2, 2 * HEAD, 2 * HEAD))


def wkv7(mixed, lz, w_lo, wa0, k_a, s0p):
    b_, l_, d4 = mixed.shape
    d_ = d4 // 4
    pairs, pw = d_ // (2 * HEAD), 2 * HEAD
    cs = WKV_CHUNK
    assert cs == HEAD and l_ % cs == 0
    n_c = l_ // cs

    fwd = lambda w, j: pl.BlockSpec((1, cs, w), lambda b, c: (b, c, j))
    bwd = lambda w, j: pl.BlockSpec((1, cs, w), lambda b, c: (b, n_c - 1 - c, j))
    full = lambda a: pl.BlockSpec(a.shape, lambda b, c: (0,) * a.ndim)
    state = pl.BlockSpec((2, 1, pairs, pw, pw), lambda b, c: (0, b, 0, 0, 0))
    return pl.pallas_call(
        _wkv_kernel,
        grid=(b_, n_c),
        in_specs=[fwd(d_, 0), fwd(d_, 1), fwd(d_, 2), fwd(d_, 3), fwd(LORA_PAD, 0),
                  bwd(d_, 0), bwd(d_, 1), bwd(d_, 2), bwd(d_, 3), bwd(LORA_PAD, 0),
                  full(w_lo), full(wa0), full(k_a), state],
        out_specs=[fwd(d_, 0), bwd(d_, 0), state],
        out_shape=[jax.ShapeDtypeStruct((b_, l_, d_), F32), jax.ShapeDtypeStruct((b_, l_, d_), F32),
                   jax.ShapeDtypeStruct((2, b_, pairs, pw, pw), F32)],
        scratch_shapes=[pltpu.VMEM((2, pairs, pw, pw), F32)],
        compiler_params=pltpu.CompilerParams(
            dimension_semantics=("arbitrary", "arbitrary"), vmem_limit_bytes=VMEM_LIMIT),
        name="wkv7",
    )(mixed, mixed, mixed, mixed, lz, mixed, mixed, mixed, mixed, lz, w_lo, wa0, k_a, s0p)


def _lru_kernel(cur_ref, prev_ref, next_ref, cw_ref, cb_ref, wg_ref, bg_ref, cl_ref, h0_ref,
                h_ref, hT_ref, a_scr, u_scr, h_scr):
    d = pl.program_id(0)
    c = pl.program_id(2)
    n_c = pl.num_programs(2)
    ti = c + d * (n_c - 1 - 2 * c)
    tm, d_ = a_scr.shape
    gw = GATE_GROUP

    @pl.when(c == 0)
    def _():
        h_scr[...] = h0_ref[0, 0]

    prev = jnp.where(ti > 0, prev_ref[0], 0.0)
    nxt = jnp.where(ti < n_c - 1, next_ref[0], 0.0)
    ext = jnp.concatenate([prev, cur_ref[0], nxt], axis=0)
    xc = cb_ref[...]
    for j in range(CONV_W):
        o = HALO - CONV_W // 2 + j
        xc = xc + cw_ref[j:j + 1, :] * ext[o:o + tm]
    for g in range(d_ // gw):
        sl = slice(g * gw, (g + 1) * gw)
        z = _dot(xc[:, sl], wg_ref[0, g])
        rg = _sigmoid(z[:, :gw] + bg_ref[0, :, sl])
        ig = _sigmoid(z[:, gw:] + bg_ref[0, :, d_ + g * gw:d_ + (g + 1) * gw])
        log_a = cl_ref[0, :, sl] * rg
        a_scr[:, sl] = jnp.exp(log_a)
        u_scr[:, sl] = jnp.sqrt(1.0 - jnp.exp(2.0 * log_a)) * ig * xc[:, sl]

    rowid = lax.broadcasted_iota(jnp.int32, (8, d_), 0)

    def tile_scan(i, h, rev):
        t8 = (tm // 8 - 1 - i) if rev else i
        r0 = pl.multiple_of(t8 * 8, 8)
        a8 = a_scr[pl.ds(r0, 8), :]
        u8 = u_scr[pl.ds(r0, 8), :]
        for s in (1, 2, 4):
            ok = (rowid < 8 - s) if rev else (rowid >= s)
            sh = (8 - s) if rev else s
            a_sh = jnp.where(ok, pltpu.roll(a8, sh, 0), 1.0)
            u_sh = jnp.where(ok, pltpu.roll(u8, sh, 0), 0.0)
            u8 = a8 * u_sh + u8
            a8 = a8 * a_sh
        h8 = u8 + a8 * h
        h_ref[0, 0, pl.ds(r0, 8), :] = h8
        return h8[0:1] if rev else h8[7:8]

    @pl.when(d == 0)
    def _():
        h_scr[...] = lax.fori_loop(0, tm // 8, lambda i, h: tile_scan(i, h, False), h_scr[...], unroll=2)

    @pl.when(d == 1)
    def _():
        h_scr[...] = lax.fori_loop(0, tm // 8, lambda i, h: tile_scan(i, h, True), h_scr[...], unroll=2)

    @pl.when(c == n_c - 1)
    def _():
        hT_ref[0, 0] = h_scr[...]


def rglru(main, xb_col, conv_w, conv_b, w_gate, b_gate, c_lam, h0, tm):
    b_, l_, _ = main.shape
    d_ = conv_w.shape[1]
    tm = min(tm, l_)
    n_c = l_ // tm
    per = tm // HALO
    n_h = l_ // HALO
    tmap = lambda d, c: c + d * (n_c - 1 - 2 * c)
    dirp = lambda shape: pl.BlockSpec((1,) + shape, lambda d, b, c: (d,) + (0,) * len(shape))
    st = pl.BlockSpec((1, 1, 1, d_), lambda d, b, c: (d, b, 0, 0))
    return pl.pallas_call(
        _lru_kernel,
        grid=(2, b_, n_c),
        in_specs=[pl.BlockSpec((1, tm, d_), lambda d, b, c: (b, tmap(d, c), xb_col)),
                  pl.BlockSpec((1, HALO, d_), lambda d, b, c: (b, jnp.maximum(tmap(d, c) * per - 1, 0), xb_col)),
                  pl.BlockSpec((1, HALO, d_), lambda d, b, c: (b, jnp.minimum((tmap(d, c) + 1) * per, n_h - 1), xb_col)),
                  pl.BlockSpec((CONV_W, d_), lambda d, b, c: (0, 0)),
                  pl.BlockSpec((1, d_), lambda d, b, c: (0, 0)),
                  dirp(w_gate.shape[1:]), dirp((1, 2 * d_)), dirp((1, d_)), st],
        out_specs=[pl.BlockSpec((1, 1, tm, d_), lambda d, b, c: (d, b, tmap(d, c), 0)), st],
        out_shape=[jax.ShapeDtypeStruct((2, b_, l_, d_), F32), jax.ShapeDtypeStruct((2, b_, 1, d_), F32)],
        scratch_shapes=[pltpu.VMEM((tm, d_), F32), pltpu.VMEM((tm, d_), F32), pltpu.VMEM((1, d_), F32)],
        compiler_params=pltpu.CompilerParams(
            dimension_semantics=("arbitrary", "arbitrary", "arbitrary"), vmem_limit_bytes=VMEM_LIMIT),
        name="rglru",
    )(main, main, main, conv_w, conv_b, w_gate, b_gate, c_lam, h0)


def _post_kernel(yf_ref, yb_ref, r_ref, k_ref, v_ref, lz_ref, hf_ref, hb_ref, gb_ref, ga_ref, gm_ref, x_ref,
                 gt_ref, sc_ref, sh_ref, wa2_ref, a0_ref, g2_ref, ka_ref, rk_ref, gg_ref, gnb_ref,
                 pa_ref, pb_ref, wo_ref, lg_ref, lb_ref, rwh_ref, rwl_ref, rb_ref,
                 x1_ref, h2_ref, lg_out_ref):
    d_ = x_ref.shape[2]
    y = yf_ref[0] + yb_ref[0]
    mu = _head_sums(y) * (1.0 / HEAD)
    yc = y - mu
    var = _head_sums(yc * yc) * (1.0 / HEAD)
    yn = yc * lax.rsqrt(var + GN_EPS) * gg_ref[...] + gnb_ref[...]
    lz = lz_ref[0]
    a2 = _sigmoid(_dot(lz[:, :LORA_DECAY + LORA_AAA], wa2_ref[...]) + a0_ref[...])
    r, k, v = r_ref[0].astype(F32), k_ref[0].astype(F32), v_ref[0].astype(F32)
    ksum = k * (2.0 + (a2[:, :d_] + a2[:, d_:] - 2.0) * ka_ref[...])
    bonus = _head_sums(r * ksum * rk_ref[...]) * v
    g = _dot(lz[:, LORA_DECAY + LORA_AAA:], g2_ref[...])
    y_a = ((yn + bonus) * g).astype(BF16)
    gb = gb_ref[0]
    gelu = 0.5 * gb * (1.0 + jnp.tanh(0.7978845608028654 * (gb + 0.044715 * gb * gb * gb)))
    y_b = ((hf_ref[0, 0] + hb_ref[0, 0]) * gelu).astype(BF16)
    m = _sigmoid(ga_ref[0]) * _dot(y_a, pa_ref[...]) + _sigmoid(gm_ref[0]) * _dot(y_b, pb_ref[...])
    mix = _dot(m, wo_ref[...])
    x1 = _ln(ALPHA * _ln(x_ref[0]) + gt_ref[0] * mix) * lg_ref[...] + lb_ref[...]
    x1_ref[0] = x1
    h2 = x1 * (1.0 + sc_ref[0]) + sh_ref[0]
    h2_ref[0] = h2
    hh, hl = _split(h2, 2)
    lg_out_ref[0] = (_dot(hh, rwh_ref[...]) + _dot(hl, rwh_ref[...]) + _dot(hh, rwl_ref[...])) + rb_ref[...]


def post_mix(y_f, y_b, mixed, lz, hh, main, x, gt, sc, sh, params, tm):
    b_, l_, d_ = x.shape
    dsec = lambda e: pl.BlockSpec((1, 1, tm, d_), lambda b, i: (e, b, i, 0))
    col = lambda j: pl.BlockSpec((1, tm, d_), lambda b, i: (b, i, j))
    vec = pl.BlockSpec((1, 1, d_), lambda b, i: (b, 0, 0))
    full = lambda a: pl.BlockSpec(a.shape, lambda b, i: (0,) * a.ndim)
    return pl.pallas_call(
        _post_kernel,
        grid=(b_, l_ // tm),
        in_specs=[col(0), col(0), col(0), col(1), col(2),
                  pl.BlockSpec((1, tm, LORA_PAD), lambda b, i: (b, i, 0)),
                  dsec(0), dsec(1), col(4), col(5), col(6), col(0), vec, vec, vec] + [full(p) for p in params],
        out_specs=[col(0), col(0), pl.BlockSpec((1, tm, 128), lambda b, i: (b, i, 0))],
        out_shape=[jax.ShapeDtypeStruct((b_, l_, d_), F32), jax.ShapeDtypeStruct((b_, l_, d_), F32),
                   jax.ShapeDtypeStruct((b_, l_, 128), F32)],
        compiler_params=pltpu.CompilerParams(vmem_limit_bytes=VMEM_LIMIT),
        name="post_mix",
    )(y_f, y_b, mixed, mixed, mixed, lz, hh, hh, main, main, main, x, gt, sc, sh, *params)


def _mm_kernel(x_ref, w_ref, o_ref, *, pa, pb):
    o_ref[...] = _dot(x_ref[...], w_ref[...], _NN, pa, pb)


def pmm(x, w, tm=512, tn=1024, pa=1, pb=1):
    m_, k_ = x.shape
    n_ = w.shape[1]
    tm, tn = min(tm, m_), min(tn, n_)
    assert m_ % tm == 0 and n_ % tn == 0, (x.shape, w.shape, tm, tn)
    return pl.pallas_call(
        functools.partial(_mm_kernel, pa=pa, pb=pb),
        grid=(m_ // tm, n_ // tn),
        in_specs=[pl.BlockSpec((tm, k_), lambda i, j: (i, 0)), pl.BlockSpec((k_, tn), lambda i, j: (0, j))],
        out_specs=pl.BlockSpec((tm, tn), lambda i, j: (i, j)),
        out_shape=jax.ShapeDtypeStruct((m_, n_), F32),
        compiler_params=pltpu.CompilerParams(vmem_limit_bytes=VMEM_LIMIT),
        name="pmm",
    )(x, w)


ROUTE_LANES = 128
R_E, R_RANK, R_W = 0, 2, 4
DMA_UNROLL = 4


def _route_kernel(lg_ref, rec_ref, cnt_ref, carry):
    i = pl.program_id(0)
    tm = lg_ref.shape[0]

    @pl.when(i == 0)
    def _():
        carry[...] = jnp.zeros_like(carry)

    lg = lg_ref[...]
    lane = lax.broadcasted_iota(jnp.int32, lg.shape, 1)
    neg = -jnp.inf
    first = lambda m: jnp.min(jnp.where(m, lane, ROUTE_LANES), axis=1, keepdims=True)
    is_g = lane < N_GROUPS
    gmax = jnp.max(jnp.where(is_g, lg, neg), axis=1, keepdims=True)
    gsel = first(is_g & (lg == gmax))
    p_g = 1.0 / jnp.sum(jnp.where(is_g, jnp.exp(lg - gmax), 0.0), axis=1, keepdims=True)
    in_grp = (lane >= N_GROUPS) & (lane < N_GROUPS + N_EXPERTS) & ((lane - N_GROUPS) // EXPERTS_PER_GROUP == gsel)
    el = jnp.where(in_grp, lg, neg)
    v1 = jnp.max(el, axis=1, keepdims=True)
    i1 = first(in_grp & (el == v1))
    rest = in_grp & (lane != i1)
    el2 = jnp.where(rest, lg, neg)
    v2 = jnp.max(el2, axis=1, keepdims=True)
    i2 = first(rest & (el2 == v2))
    e21 = jnp.exp(v2 - v1)
    w1 = p_g / (1.0 + e21)
    w2 = w1 * e21
    oh1, oh2 = lane == i1, lane == i2
    both = jnp.where(oh1 | oh2, 1.0, 0.0)
    row = lax.broadcasted_iota(jnp.int32, (tm, tm), 0)
    col = lax.broadcasted_iota(jnp.int32, (tm, tm), 1)
    before = jnp.where(col < row, 1.0, 0.0).astype(BF16)
    cnt = _dot(before, both.astype(BF16)) + carry[...]
    rank1 = jnp.sum(jnp.where(oh1, cnt, 0.0), axis=1, keepdims=True)
    rank2 = jnp.sum(jnp.where(oh2, cnt, 0.0), axis=1, keepdims=True)
    carry[...] = carry[...] + jnp.sum(both, axis=0, keepdims=True)
    rec = jnp.zeros_like(lg)
    for k, val in ((R_E, (i1 - N_GROUPS).astype(F32)), (R_E + 1, (i2 - N_GROUPS).astype(F32)),
                   (R_RANK, rank1), (R_RANK + 1, rank2), (R_W, w1), (R_W + 1, w2)):
        rec = jnp.where(lane == k, val, rec)
    rec_ref[...] = rec
    cnt_ref[...] = carry[...]


def route(logits, tm=256):
    t_ = logits.shape[0]
    return pl.pallas_call(
        _route_kernel,
        grid=(t_ // tm,),
        in_specs=[pl.BlockSpec((tm, ROUTE_LANES), lambda i: (i, 0))],
        out_specs=[pl.BlockSpec((tm, ROUTE_LANES), lambda i: (i, 0)), pl.BlockSpec((1, ROUTE_LANES), lambda i: (0, 0))],
        out_shape=[jax.ShapeDtypeStruct((t_, ROUTE_LANES), F32), jax.ShapeDtypeStruct((1, ROUTE_LANES), F32)],
        scratch_shapes=[pltpu.VMEM((1, ROUTE_LANES), F32)],
        compiler_params=pltpu.CompilerParams(dimension_semantics=("arbitrary",)),
        name="route",
    )(logits)


def _dispatch_kernel(dest_ref, zrow_ref, h_ref, xs_ref, zbuf, sem, zsem):
    tm = h_ref.shape[0]
    base = pl.program_id(0) * tm * TOP_K

    @pl.when(pl.program_id(0) == 0)
    def _():
        zbuf[...] = jnp.zeros_like(zbuf)
        n_blocks = xs_ref.shape[0] // MOE_ROWS

        def zero_copy(row0):
            return pltpu.make_async_copy(zbuf, xs_ref.at[pl.ds(pl.multiple_of(row0, MOE_ROWS), MOE_ROWS)], zsem)

        def tail(fn):
            return lax.fori_loop(zrow_ref[N_EXPERTS] // MOE_ROWS, n_blocks, lambda b, c: (fn(zero_copy(b * MOE_ROWS)), c)[1], 0)

        for e in range(N_EXPERTS):
            @pl.when(zrow_ref[e] >= 0)
            def _():
                zero_copy(zrow_ref[e]).start()
        tail(lambda cp: cp.start())
        for e in range(N_EXPERTS):
            @pl.when(zrow_ref[e] >= 0)
            def _():
                zero_copy(zrow_ref[e]).wait()
        tail(lambda cp: cp.wait())

    def copy(r, s):
        return pltpu.make_async_copy(h_ref.at[pl.ds(r, 1)], xs_ref.at[pl.ds(dest_ref[base + r * TOP_K + s], 1)], sem)

    def start(r, carry):
        for s in range(TOP_K):
            copy(r, s).start()
        return carry

    def wait(r, carry):
        for s in range(TOP_K):
            copy(r, s).wait()
        return carry

    lax.fori_loop(0, tm, start, 0, unroll=DMA_UNROLL)
    lax.fori_loop(0, tm, wait, 0, unroll=DMA_UNROLL)


def dispatch(dest, zero_row, h, n_pad, tm=512):
    t_, d_ = h.shape
    grid_spec = pltpu.PrefetchScalarGridSpec(
        num_scalar_prefetch=2,
        grid=(t_ // tm,),
        in_specs=[pl.BlockSpec((tm, d_), lambda i, dest, zr: (i, 0))],
        out_specs=pl.BlockSpec(memory_space=pl.ANY),
        scratch_shapes=[pltpu.VMEM((MOE_ROWS, d_), F32), pltpu.SemaphoreType.DMA(()), pltpu.SemaphoreType.DMA(())],
    )
    return pl.pallas_call(
        _dispatch_kernel,
        grid_spec=grid_spec,
        out_shape=jax.ShapeDtypeStruct((n_pad, d_), F32),
        compiler_params=pltpu.CompilerParams(dimension_semantics=("arbitrary",), has_side_effects=True),
        name="dispatch",
    )(dest, zero_row, h)


def _moe_kernel(be_ref, nv_ref, x_ref, w1_ref, w3_ref, w2_ref, o_ref, w1b, w3b, w2b):
    blk = pl.program_id(0)

    @pl.when((blk == 0) | (be_ref[blk] != be_ref[jnp.maximum(blk - 1, 0)]))
    def _():
        w1b[...] = w1_ref[0].astype(BF16)
        w3b[...] = w3_ref[0].astype(BF16)
        w2b[...] = w2_ref[0].astype(BF16)

    @pl.when(nv_ref[blk] > 0)
    def _():
        x = x_ref[...].astype(BF16)
        h1 = _dot(x, w1b[...])
        h3 = _dot(x, w3b[...])
        hh = h1 * _sigmoid(h1) * h3
        o_ref[...] = _dot(hh, w2b[...])

    @pl.when(nv_ref[blk] == 0)
    def _():
        o_ref[...] = jnp.zeros_like(o_ref)


def moe_experts(xs, blk_e, blk_n, w1, w3, w2):
    n_pad, d_ = xs.shape
    n_blocks = n_pad // MOE_ROWS
    de = w1.shape[2]
    grid_spec = pltpu.PrefetchScalarGridSpec(
        num_scalar_prefetch=2,
        grid=(n_blocks,),
        in_specs=[pl.BlockSpec((MOE_ROWS, d_), lambda i, be, nv: (i, 0)),
                  pl.BlockSpec((1, d_, de), lambda i, be, nv: (be[i], 0, 0)),
                  pl.BlockSpec((1, d_, de), lambda i, be, nv: (be[i], 0, 0)),
                  pl.BlockSpec((1, de, d_), lambda i, be, nv: (be[i], 0, 0))],
        out_specs=pl.BlockSpec((MOE_ROWS, d_), lambda i, be, nv: (i, 0)),
        scratch_shapes=[pltpu.VMEM((d_, de), BF16), pltpu.VMEM((d_, de), BF16), pltpu.VMEM((de, d_), BF16)],
    )
    return pl.pallas_call(
        _moe_kernel,
        grid_spec=grid_spec,
        out_shape=jax.ShapeDtypeStruct((n_pad, d_), F32),
        compiler_params=pltpu.CompilerParams(dimension_semantics=("arbitrary",), vmem_limit_bytes=VMEM_LIMIT),
        name="moe_experts",
    )(blk_e, blk_n, xs, w1, w3, w2)


def _combine_kernel(dest_ref, x_ref, rec_ref, gt_ref, g_ref, b_ref, ys_ref, o_ref, buf, sem):
    tm = x_ref.shape[1]
    base = (pl.program_id(0) * pl.num_programs(1) + pl.program_id(1)) * tm * TOP_K

    def copy(r, s):
        return pltpu.make_async_copy(ys_ref.at[pl.ds(dest_ref[base + r * TOP_K + s], 1)], buf.at[s, pl.ds(r, 1)], sem)

    def start(r, carry):
        for s in range(TOP_K):
            copy(r, s).start()
        return carry

    def wait(r, carry):
        for s in range(TOP_K):
            copy(r, s).wait()
        return carry

    lax.fori_loop(0, tm, start, 0, unroll=DMA_UNROLL)
    lax.fori_loop(0, tm, wait, 0, unroll=DMA_UNROLL)
    rec = rec_ref[...]
    moe = rec[:, R_W:R_W + 1] * buf[0]
    for s in range(1, TOP_K):
        moe = moe + rec[:, R_W + s:R_W + s + 1] * buf[s]
    o_ref[0] = _ln(ALPHA * x_ref[0] + gt_ref[0] * moe) * g_ref[...] + b_ref[...]


def combine_ln(dest, x, rec, gt, g, b, ys, tm=512):
    b_, l_, d_ = x.shape
    n_i = l_ // tm
    grid_spec = pltpu.PrefetchScalarGridSpec(
        num_scalar_prefetch=1,
        grid=(b_, n_i),
        in_specs=[pl.BlockSpec((1, tm, d_), lambda bi, i, dest: (bi, i, 0)),
                  pl.BlockSpec((tm, ROUTE_LANES), lambda bi, i, dest: (bi * n_i + i, 0)),
                  pl.BlockSpec((1, 1, d_), lambda bi, i, dest: (bi, 0, 0)),
                  pl.BlockSpec((1, d_), lambda bi, i, dest: (0, 0)),
                  pl.BlockSpec((1, d_), lambda bi, i, dest: (0, 0)),
                  pl.BlockSpec(memory_space=pl.ANY)],
        out_specs=pl.BlockSpec((1, tm, d_), lambda bi, i, dest: (bi, i, 0)),
        scratch_shapes=[pltpu.VMEM((TOP_K, tm, d_), F32), pltpu.SemaphoreType.DMA(())],
    )
    return pl.pallas_call(
        _combine_kernel,
        grid_spec=grid_spec,
        out_shape=jax.ShapeDtypeStruct(x.shape, F32),
        compiler_params=pltpu.CompilerParams(
            dimension_semantics=("arbitrary", "arbitrary"), vmem_limit_bytes=VMEM_LIMIT),
        name="combine_ln",
    )(dest, x, rec, gt, g.reshape(1, d_), b.reshape(1, d_), ys)


def hier_moe_ln(x1, h2, logits, gt, g, b, w1, w3, w2):
    b_, l_, d_ = x1.shape
    t_ = b_ * l_
    rec, cnt = route(logits)
    counts = cnt[0, N_GROUPS:N_GROUPS + N_EXPERTS].astype(jnp.int32)
    padded = (counts + MOE_ROWS - 1) // MOE_ROWS * MOE_ROWS
    pad_end = jnp.cumsum(padded)
    pad_start = pad_end - padded
    eid = rec[:, R_E:R_E + TOP_K].astype(jnp.int32)
    rank = rec[:, R_RANK:R_RANK + TOP_K].astype(jnp.int32)
    onehot = eid[..., None] == jnp.arange(N_EXPERTS, dtype=jnp.int32)
    dest = (rank + jnp.sum(jnp.where(onehot, pad_start, 0), -1)).reshape(-1)
    n_blocks = -(-(t_ * TOP_K) // MOE_ROWS) + N_EXPERTS
    blk_lo = jnp.arange(n_blocks, dtype=jnp.int32) * MOE_ROWS
    blk_e = jnp.minimum(jnp.sum(blk_lo[:, None] >= pad_end[None, :], -1), N_EXPERTS - 1).astype(jnp.int32)
    blk_n = jnp.clip(jnp.sum(jnp.where(blk_e[:, None] == jnp.arange(N_EXPERTS), pad_start + counts, 0), -1) - blk_lo,
                     0, MOE_ROWS).astype(jnp.int32)
    zero_row = jnp.concatenate([jnp.where(counts > 0, pad_end - MOE_ROWS, -1), pad_end[-1:]]).astype(jnp.int32)
    xs = dispatch(dest, zero_row, h2.reshape(t_, d_), n_blocks * MOE_ROWS)
    ys = moe_experts(xs, blk_e, blk_n, w1, w3, w2)
    return combine_ln(dest, x1, rec, gt, g, b, ys)


def _block_diag_groups(w, grp):
    n = w.shape[0]
    wg = w.reshape(n // grp, grp, HEAD, HEAD)
    eye = jnp.eye(grp, dtype=w.dtype)
    return jnp.einsum("gaij,ab->gaibj", wg, eye).reshape(n // grp, grp * HEAD, grp * HEAD)


def token_scans(h_in, sc, sh, wts, grid_mode, states):
    b_, l_, d_ = h_in.shape
    main, lora = inproj(h_in, sc, sh, wts["w_main"], wts["w_lora"], min(1024, l_))
    mixed, lz = rwkv_mix(main, lora, wts["mu"], wts["mu_l"], wts["k_k"], grid_mode, 512 if grid_mode else l_)
    y_f, y_b, s_new = wkv7(mixed, lz, wts["w_lo"], wts["wa0"], wts["k_a"], states[0])
    hh, h_new = rglru(main, 3, wts["conv_w"], wts["conv_b"], wts["w_gate"], wts["b_gate"], wts["c_lam"], states[1], 512)
    return (main, mixed, lz, y_f, y_b, hh), (s_new, h_new)


def kernel(x, c, ctx, c_ctx, w_ada, b_ada, w_in, mu_a, w0, w2, a0, a2, g2, k_k, k_a, r_k, gn_g, gn_b, conv_w, conv_b, lru_wa, lru_ba, lru_wx, lru_bx, lru_lam, p_a, p_b, w_o, ln1_g, ln1_b, router_g, router_g_b, router_e, router_e_b, e_w1, e_w3, e_w2, ln2_g, ln2_b):
    b_, l_, d_ = x.shape
    heads = d_ // HEAD
    l = 0
    row = lambda v: v.reshape(1, -1)
    cc = jnp.concatenate([c, c_ctx[None]], 0)
    cc = jnp.pad(jax.nn.silu(cc), ((0, 8 - cc.shape[0]), (0, 0)))
    mod = pmm(cc, w_ada[l], 8, 1024, 2, 2)[:b_ + 1] + b_ada[l]
    mods = jnp.split(mod, 6, axis=-1)
    sh1, sc1, gt1, sh2, sc2, gt2 = [m[:b_, None, :] for m in mods]
    csh1, csc1 = [jnp.broadcast_to(m[b_:, None, :], (b_, 1, d_)) for m in mods[:2]]

    a_slab = 3 * d_ + LORA_DECAY + LORA_AAA + LORA_GATE
    n_lora = a_slab - 3 * d_
    wi, mu = w_in[l], mu_a[l]
    zeros_lo = jnp.zeros((LORA_DECAY, d_), F32)
    w_lo = jnp.stack([jnp.concatenate([jnp.concatenate([w2[l, e], zeros_lo], 1),
                                       jnp.concatenate([zeros_lo, a2[l, e]], 1)], 0) for e in range(2)])
    wts = dict(
        w_main=jnp.concatenate([wi[:, :3 * d_], wi[:, a_slab:]], 1).astype(BF16),
        w_lora=jnp.pad(wi[:, 3 * d_:a_slab], ((0, 0), (0, LORA_PAD - n_lora))).astype(BF16),
        mu=row(mu[:3 * d_]), mu_l=row(jnp.pad(mu[3 * d_:], (0, LORA_PAD - n_lora))), k_k=row(k_k[l]), k_a=row(k_a[l]),
        w_lo=w_lo.astype(BF16),
        wa0=jnp.concatenate([w0[l], a0[l]], -1)[:, None, :],
        conv_w=conv_w[l], conv_b=row(conv_b[l]),
        w_gate=jnp.stack([jnp.concatenate([_block_diag_groups(lru_wa[l, e], 4), _block_diag_groups(lru_wx[l, e], 4)], -1)
                          for e in range(2)]).astype(BF16),
        b_gate=jnp.concatenate([lru_ba[l], lru_bx[l]], -1)[:, None, :],
        c_lam=(-LRU_C * jax.nn.softplus(-lru_lam[l]))[:, None, :])

    s0 = jnp.zeros((2, b_, heads // 2, 2 * HEAD, 2 * HEAD), F32)
    h0 = jnp.zeros((2, b_, 1, d_), F32)
    _, ctx_states = token_scans(ctx, csc1, csh1, wts, False, (s0, h0))
    (main, mixed, lz, y_f, y_b, hh), _ = token_scans(x, sc1, sh1, wts, True, ctx_states)

    zeros_a = jnp.zeros((LORA_DECAY, 2 * d_), F32)
    rw = jnp.pad(jnp.concatenate([router_g[l], router_e[l]], 1), ((0, 0), (0, 128 - N_GROUPS - N_EXPERTS)))
    rw_hi = rw.astype(BF16)
    params = [
        jnp.concatenate([zeros_a, jnp.concatenate([a2[l, 0], a2[l, 1]], 1)], 0).astype(BF16),
        row(jnp.concatenate([a0[l, 0], a0[l, 1]])),
        jnp.pad(g2[l], ((0, LORA_PAD - LORA_DECAY - LORA_AAA - LORA_GATE), (0, 0))).astype(BF16),
        row(k_a[l]), row(r_k[l]), row(gn_g[l]), row(gn_b[l]),
        p_a[l].astype(BF16), p_b[l].astype(BF16), w_o[l].astype(BF16), row(ln1_g[l]), row(ln1_b[l]),
        rw_hi, (rw - rw_hi.astype(F32)).astype(BF16),
        row(jnp.pad(jnp.concatenate([router_g_b[l], router_e_b[l]]), (0, 128 - N_GROUPS - N_EXPERTS)))]
    x1, h2, logits = post_mix(y_f, y_b, mixed, lz, hh, main, x, gt1, sc2, sh2, params, 256)

    return hier_moe_ln(x1, h2, logits.reshape(b_ * l_, -1), gt2, ln2_g[l], ln2_b[l], e_w1[l], e_w3[l], e_w2[l])
```

```python
import functools

import jax
import jax.numpy as jnp
from jax import lax
from jax.experimental import pallas as pl
from jax.experimental.pallas import tpu as pltpu

F32 = jnp.float32
BF16 = jnp.bfloat16

GRID_W = 64
HEAD = 64
LORA_DECAY = 64
LORA_AAA = 64
LORA_GATE = 160
LORA_PAD = 384
GN_EPS = 64e-5
LN_EPS = 1e-5
CONV_W = 5
LRU_C = 8.0
N_GROUPS = 4
EXPERTS_PER_GROUP = 8
N_EXPERTS = N_GROUPS * EXPERTS_PER_GROUP
TOP_K = 2
DEPTH = 1
ALPHA = (2 * DEPTH) ** 0.25

WKV_CHUNK = 64
MOE_ROWS = 512
GATE_GROUP = 4 * HEAD
HALO = 8
VMEM_LIMIT = 56 * 1024 * 1024

_NN = (((1,), (0,)), ((), ()))
_NT = (((1,), (1,)), ((), ()))
_TN = (((0,), (0,)), ((), ()))


def _split(x, n):
    if x.dtype == BF16:
        return [x]
    parts, rest = [], x.astype(F32)
    for i in range(n):
        p = rest.astype(BF16)
        parts.append(p)
        if i + 1 < n:
            rest = rest - p.astype(F32)
    return parts


def _dot(a, b, dims=_NN, pa=1, pb=1):
    ap, bp = _split(a, pa), _split(b, pb)
    order = max(len(ap), len(bp))
    acc = None
    for i, x in enumerate(ap):
        for j, y in enumerate(bp):
            if i + j < order:
                t = lax.dot_general(x, y, dims, preferred_element_type=F32)
                acc = t if acc is None else acc + t
    return acc


def _ln(x):
    mu = jnp.mean(x, -1, keepdims=True)
    xc = x - mu
    var = jnp.mean(xc * xc, -1, keepdims=True)
    return xc * lax.rsqrt(var + LN_EPS)


def _sigmoid(x):
    return 0.5 + 0.5 * jnp.tanh(0.5 * x)


def _softplus(x):
    return jnp.maximum(x, 0.0) + jnp.log(1.0 + jnp.exp(-jnp.abs(x)))


def _head_sums(x):
    pw = 2 * HEAD
    row = lax.broadcasted_iota(jnp.int32, (pw, pw), 0)
    col = lax.broadcasted_iota(jnp.int32, (pw, pw), 1)
    ones = jnp.where((row >= HEAD) == (col >= HEAD), 1.0, 0.0).astype(BF16)
    return jnp.concatenate([_dot(x[:, p:p + pw], ones, _NN, 2, 1) for p in range(0, x.shape[1], pw)], axis=1)


def _inproj_kernel(x_ref, sc_ref, sh_ref, wm_ref, wl_ref, main_ref, lora_ref, h_scr, *, nm):
    j = pl.program_id(2)

    @pl.when(j == 0)
    def _():
        h_scr[...] = (_ln(x_ref[0]) * (1.0 + sc_ref[0]) + sh_ref[0]).astype(BF16)

    @pl.when(j < nm)
    def _():
        main_ref[0] = _dot(h_scr[...], wm_ref[...])

    @pl.when(j == nm)
    def _():
        lora_ref[0] = _dot(h_scr[...], wl_ref[...])


def inproj(x, sc, sh, w_main, w_lora, tm):
    b_, l_, d_ = x.shape
    nm = w_main.shape[1] // d_
    row = pl.BlockSpec((1, tm, d_), lambda b, i, j: (b, i, 0))
    vec = pl.BlockSpec((1, 1, d_), lambda b, i, j: (b, 0, 0))
    return pl.pallas_call(
        functools.partial(_inproj_kernel, nm=nm),
        grid=(b_, l_ // tm, nm + 1),
        in_specs=[row, vec, vec,
                  pl.BlockSpec((d_, d_), lambda b, i, j: (0, jnp.minimum(j, nm - 1))),
                  pl.BlockSpec((d_, LORA_PAD), lambda b, i, j: (0, 0))],
        out_specs=[pl.BlockSpec((1, tm, d_), lambda b, i, j: (b, i, jnp.minimum(j, nm - 1))),
                   pl.BlockSpec((1, tm, LORA_PAD), lambda b, i, j: (b, i, 0))],
        out_shape=[jax.ShapeDtypeStruct((b_, l_, nm * d_), F32), jax.ShapeDtypeStruct((b_, l_, LORA_PAD), F32)],
        scratch_shapes=[pltpu.VMEM((tm, d_), BF16)],
        compiler_params=pltpu.CompilerParams(vmem_limit_bytes=VMEM_LIMIT),
        name="inproj",
    )(x, sc, sh, w_main, w_lora)


def _mix_kernel(cur_ref, prev_ref, next_ref, lcur_ref, lprev_ref, lnext_ref, xcur_ref, xprev_ref, xnext_ref,
                mu_ref, mul_ref, kk_ref, cw_ref, cb_ref, mixed_ref, lz_ref, xc_ref, *, grid_mode):
    i = pl.program_id(1)
    n_i = pl.num_programs(1)
    tm = cur_ref.shape[1]
    d_ = kk_ref.shape[1]

    def shifted(cur, prev, nxt):
        rows, ch = cur.shape
        rowi = lax.broadcasted_iota(jnp.int32, (rows, ch), 0)
        lane = lax.broadcasted_iota(jnp.int32, (rows, ch), 1)
        if grid_mode:
            prev = jnp.where(i > 0, prev, 0.0)
            nxt = jnp.where(i < n_i - 1, nxt, 0.0)
            up = jnp.concatenate([prev, cur[:rows - GRID_W]], axis=0)
            down = jnp.concatenate([cur[GRID_W:], nxt], axis=0)
            col = rowi % GRID_W
            left = jnp.where(col == 0, 0.0, pltpu.roll(cur, 1, 0))
            right = jnp.where(col == GRID_W - 1, 0.0, pltpu.roll(cur, rows - 1, 0))
            l4 = lane % 4
            return jnp.where(l4 == 0, left, jnp.where(l4 == 1, right, jnp.where(l4 == 2, up, down)))
        before = jnp.where(rowi == 0, 0.0, pltpu.roll(cur, 1, 0))
        after = jnp.where(rowi == rows - 1, 0.0, pltpu.roll(cur, rows - 1, 0))
        return jnp.where(lane % 2 == 0, before, after)

    def mixed(cur, prev, nxt, mu):
        return cur + mu * (shifted(cur, prev, nxt) - cur)

    for s in range(3):
        sl = slice(s * d_, (s + 1) * d_)
        z = mixed(cur_ref[0, :, sl], prev_ref[0, :, sl], next_ref[0, :, sl], mu_ref[:, sl])
        mixed_ref[0, :, sl] = z.astype(BF16)
        if s == 1:
            kq = z * kk_ref[...]
            kq = kq * lax.rsqrt(_head_sums(kq * kq) + 1e-12)
            mixed_ref[0, :, 3 * d_:4 * d_] = kq.astype(BF16)
    lz = mixed(lcur_ref[0], lprev_ref[0], lnext_ref[0], mul_ref[...])
    wa = lz[:, :LORA_DECAY + LORA_AAA]
    lane = lax.broadcasted_iota(jnp.int32, wa.shape, 1)
    lz_ref[0, :, :LORA_DECAY + LORA_AAA] = jnp.where(lane < LORA_DECAY, jnp.tanh(wa), wa).astype(BF16)
    lz_ref[0, :, LORA_DECAY + LORA_AAA:] = _sigmoid(lz[:, LORA_DECAY + LORA_AAA:]).astype(BF16)

    if grid_mode:
        before = jnp.where(i > 0, xprev_ref[0, GRID_W - HALO:, :], 0.0)
        after = jnp.where(i < n_i - 1, xnext_ref[0, :HALO, :], 0.0)
    else:
        before = after = jnp.zeros((HALO, d_), F32)
    ext = jnp.concatenate([before, xcur_ref[0], after], axis=0)
    xc = cb_ref[...]
    for j in range(CONV_W):
        o = HALO - CONV_W // 2 + j
        xc = xc + cw_ref[j:j + 1, :] * ext[o:o + tm]
    xc_ref[0] = xc


def rwkv_mix(main, lora, xb_col, mu, mu_l, k_k, conv_w, conv_b, grid_mode, tm):
    b_, l_, _ = main.shape
    d_ = k_k.shape[-1]
    if grid_mode:
        assert tm % GRID_W == 0 and l_ % tm == 0
        halo, per = GRID_W, tm // GRID_W
    else:
        assert tm == l_
        halo, per = 8, tm // 8
    n_h = l_ // halo
    cur = lambda w: pl.BlockSpec((1, tm, w), lambda b, i: (b, i, 0))
    prv = lambda w: pl.BlockSpec((1, halo, w), lambda b, i: (b, jnp.maximum(i * per - 1, 0), 0))
    nxt = lambda w: pl.BlockSpec((1, halo, w), lambda b, i: (b, jnp.minimum((i + 1) * per, n_h - 1), 0))
    par = lambda w: pl.BlockSpec((1, w), lambda b, i: (0, 0))
    xcur = pl.BlockSpec((1, tm, d_), lambda b, i: (b, i, xb_col))
    xprv = pl.BlockSpec((1, halo, d_), lambda b, i: (b, jnp.maximum(i * per - 1, 0), xb_col))
    xnxt = pl.BlockSpec((1, halo, d_), lambda b, i: (b, jnp.minimum((i + 1) * per, n_h - 1), xb_col))
    return pl.pallas_call(
        functools.partial(_mix_kernel, grid_mode=grid_mode),
        grid=(b_, l_ // tm),
        in_specs=[cur(3 * d_), prv(3 * d_), nxt(3 * d_), cur(LORA_PAD), prv(LORA_PAD), nxt(LORA_PAD), xcur, xprv, xnxt,
                  par(3 * d_), par(LORA_PAD), par(d_), pl.BlockSpec((CONV_W, d_), lambda b, i: (0, 0)), par(d_)],
        out_specs=[cur(4 * d_), cur(LORA_PAD), cur(d_)],
        out_shape=[jax.ShapeDtypeStruct((b_, l_, 4 * d_), BF16), jax.ShapeDtypeStruct((b_, l_, LORA_PAD), BF16),
                   jax.ShapeDtypeStruct((b_, l_, d_), F32)],
        compiler_params=pltpu.CompilerParams(vmem_limit_bytes=VMEM_LIMIT),
        name="rwkv_mix",
    )(main, main, main, lora, lora, lora, main, main, main, mu, mu_l, k_k, conv_w, conv_b)


def _wkv_kernel(rf_ref, kf_ref, vf_ref, kkf_ref, lzf_ref, rb_ref, kb_ref, vb_ref, kkb_ref, lzb_ref,
                wlo_ref, wa0_ref, ka_ref, s0_ref, yf_ref, yb_ref, sT_ref, s_scr):
    c = pl.program_id(1)
    n_c = pl.num_programs(1)
    cs, d_ = rf_ref.shape[1], rf_ref.shape[2]
    pairs = s_scr.shape[1]
    pw = 2 * HEAD

    @pl.when(c == 0)
    def _():
        s_scr[...] = s0_ref[:, 0]

    row = lax.broadcasted_iota(jnp.int32, (cs, cs), 0)
    col = lax.broadcasted_iota(jnp.int32, (cs, cs), 1)
    row2 = lax.broadcasted_iota(jnp.int32, (2 * cs, pw), 0)
    col2 = lax.broadcasted_iota(jnp.int32, (2 * cs, pw), 1)
    same_head = (row2 >= cs) == (col2 >= HEAD)
    lane = lax.broadcasted_iota(jnp.int32, (cs, pw), 1)
    m0, m1 = lane < HEAD, lane >= HEAD
    mm0, mm1 = col2 < HEAD, col2 >= HEAD
    cat = lambda x, y: jnp.concatenate([x, y], axis=0)
    dot = lambda x, y, dims=_NN: lax.dot_general(x, y, dims, preferred_element_type=F32)
    sel = lambda m, x: jnp.where(m, x, jnp.zeros_like(x))

    streams = []
    for d, (r_ref, k_ref, v_ref, kk_ref, lz_ref, y_ref) in enumerate(
            ((rf_ref, kf_ref, vf_ref, kkf_ref, lzf_ref, yf_ref), (rb_ref, kb_ref, vb_ref, kkb_ref, lzb_ref, yb_ref))):
        tdiff = (row2 % cs - col2 % cs) * (1 - 2 * d)
        amask = (tdiff > 0) | ((row2 >= cs) & (tdiff == 0))
        tri = jnp.where((row - col) * (1 - 2 * d) >= 0, 1.0, 0.0).astype(BF16)
        z = _dot(lz_ref[0, :, :LORA_DECAY + LORA_AAA], wlo_ref[d]) + wa0_ref[d]
        lw = -jnp.exp(-_softplus(-z[:, :d_]) - 0.5)
        a = _sigmoid(z[:, d_:])
        r, kk = r_ref[0].astype(F32), kk_ref[0].astype(F32)
        k = k_ref[0].astype(F32) * (1.0 + (a - 1.0) * ka_ref[...])
        cum = _dot(tri, lw, _NN, 1, 3)
        tot = jnp.sum(lw, axis=0, keepdims=True)
        b = kk * a
        e_neg = jnp.exp(-cum)
        e_end = jnp.exp(tot - cum)
        rt = (r * jnp.exp(cum)).astype(BF16)
        at = (-kk * jnp.exp(cum - lw)).astype(BF16)
        bt = (b * e_neg).astype(BF16)
        kt = (k * e_neg).astype(BF16)
        bd = (b * e_end).astype(BF16)
        kd = (k * e_end).astype(BF16)
        vb = v_ref[0]
        p_end = jnp.exp(tot)
        for p in range(pairs):
            sl = slice(p * pw, (p + 1) * pw)
            streams.append(dict(d=d, p=p, sl=sl, y_ref=y_ref, amask=amask, ar=cat(at[:, sl], rt[:, sl]),
                                bt=bt[:, sl], kt=kt[:, sl], bdkd=cat(bd[:, sl], kd[:, sl]), v=vb[:, sl],
                                p_end=p_end[:, sl]))

    bd = lambda x: cat(sel(m0, x), sel(m1, x))
    lcat = lambda x, y: jnp.concatenate([x, y], axis=1)
    eye_p = jnp.where(lax.broadcasted_iota(jnp.int32, (cs, pw), 0) == lane % cs, 1.0, 0.0)
    for st in streams:
        q = dot(cat(sel(mm0, st["ar"]), sel(mm1, st["ar"])),
                cat(jnp.where(m0, st["bt"], st["kt"]), jnp.where(m0, st["kt"], st["bt"])), _NT)
        q0, q1 = sel(st["amask"], q[:2 * cs]), sel(st["amask"], q[2 * cs:])
        st["nil"] = jnp.where(m0, q0[:cs], q1[:cs])
        q0, q1 = q0.astype(BF16), q1.astype(BF16)
        st["ak"] = jnp.where(m0, q1[:cs], q0[:cs])
        st["rbk"] = lcat(q0[cs:], q1[cs:])
    for st in streams:
        nb = st["nil"].astype(BF16)
        st["t"] = eye_p + st["nil"]
        st["nb"] = dot(nb, bd(nb)).astype(BF16)
    for i in range(max(cs.bit_length() - 3, 0)):
        for st in streams:
            xx = dot(cat(st["t"].astype(BF16), st["nb"]), bd(st["nb"]))
            st["t"] = st["t"] + xx[:cs]
            st["nb"] = xx[cs:].astype(BF16)
    for st in streams:
        st["t"] = (st["t"] + dot(st["t"].astype(BF16), bd(st["nb"]))).astype(BF16)

    zero = jnp.zeros((cs, pw), BF16)
    for st in streams:
        st["s"] = s_scr[st["d"], st["p"]]
        st["as"] = dot(lcat(st["ar"], cat(st["ak"], zero)),
                       cat(st["s"].astype(BF16), cat(sel(m1, st["v"]), sel(m0, st["v"]))))
    for st in streams:
        st["u"] = dot(st["t"], bd(st["as"][:cs].astype(BF16))).astype(BF16)
    for st in streams:
        u_p, v_p = st["u"], st["v"]
        y = st["as"][cs:] + dot(st["rbk"], cat(cat(sel(m0, u_p), sel(m0, v_p)), cat(sel(m1, v_p), sel(m1, u_p))))
        st["y_ref"][0, :, st["sl"]] = y
        p_col = jnp.broadcast_to(st["p_end"], (pw, pw)).T
        upd = dot(st["bdkd"], cat(u_p, v_p), _TN)
        s_scr[st["d"], st["p"]] = p_col * st["s"] + sel(same_head, upd)

    @pl.when(c == n_c - 1)
    def _():
        sT_ref[:, 0] = s_scr[...]


def _pair_states(s):
    st = jnp.swapaxes(s, -1, -2)
    st = st.reshape(s.shape[:-3] + (s.shape[-3] // 2, 2, HEAD, HEAD))
    eye = jnp.eye(2, dtype=s.dtype)
    out = jnp.einsum("...pakv,ab->...pakbv", st, eye)
    return out.reshape(s.shape[:-3] + (s.shape[-3] // 2, 2 * HEAD, 2 * HEAD))


def wkv7(mixed, lz, w_lo, wa0, k_a, s0p):
    b_, l_, d4 = mixed.shape
    d_ = d4 // 4
    pairs, pw = d_ // (2 * HEAD), 2 * HEAD
    cs = WKV_CHUNK
    assert cs == HEAD and l_ % cs == 0
    n_c = l_ // cs

    fwd = lambda w, j: pl.BlockSpec((1, cs, w), lambda b, c: (b, c, j))
    bwd = lambda w, j: pl.BlockSpec((1, cs, w), lambda b, c: (b, n_c - 1 - c, j))
    full = lambda a: pl.BlockSpec(a.shape, lambda b, c: (0,) * a.ndim)
    state = pl.BlockSpec((2, 1, pairs, pw, pw), lambda b, c: (0, b, 0, 0, 0))
    return pl.pallas_call(
        _wkv_kernel,
        grid=(b_, n_c),
        in_specs=[fwd(d_, 0), fwd(d_, 1), fwd(d_, 2), fwd(d_, 3), fwd(LORA_PAD, 0),
                  bwd(d_, 0), bwd(d_, 1), bwd(d_, 2), bwd(d_, 3), bwd(LORA_PAD, 0),
                  full(w_lo), full(wa0), full(k_a), state],
        out_specs=[fwd(d_, 0), bwd(d_, 0), state],
        out_shape=[jax.ShapeDtypeStruct((b_, l_, d_), F32), jax.ShapeDtypeStruct((b_, l_, d_), F32),
                   jax.ShapeDtypeStruct((2, b_, pairs, pw, pw), F32)],
        scratch_shapes=[pltpu.VMEM((2, pairs, pw, pw), F32)],
        compiler_params=pltpu.CompilerParams(
            dimension_semantics=("arbitrary", "arbitrary"), vmem_limit_bytes=VMEM_LIMIT),
        name="wkv7",
    )(mixed, mixed, mixed, mixed, lz, mixed, mixed, mixed, mixed, lz, w_lo, wa0, k_a, s0p)


def _lru_kernel(xc_ref, wg_ref, bg_ref, cl_ref, h0_ref, h_ref, hT_ref, a_scr, u_scr, h_scr):
    d = pl.program_id(0)
    c = pl.program_id(2)
    n_c = pl.num_programs(2)
    tm, d_ = a_scr.shape
    gw = GATE_GROUP

    @pl.when(c == 0)
    def _():
        h_scr[...] = h0_ref[0, 0]

    xc = xc_ref[0]
    for g in range(d_ // gw):
        sl = slice(g * gw, (g + 1) * gw)
        z = _dot(xc[:, sl], wg_ref[0, g])
        rg = _sigmoid(z[:, :gw] + bg_ref[0, :, sl])
        ig = _sigmoid(z[:, gw:] + bg_ref[0, :, d_ + g * gw:d_ + (g + 1) * gw])
        log_a = cl_ref[0, :, sl] * rg
        a_scr[:, sl] = jnp.exp(log_a)
        u_scr[:, sl] = jnp.sqrt(1.0 - jnp.exp(2.0 * log_a)) * ig * xc[:, sl]

    rowid = lax.broadcasted_iota(jnp.int32, (8, d_), 0)

    def tile_scan(i, h, rev):
        t8 = (tm // 8 - 1 - i) if rev else i
        r0 = pl.multiple_of(t8 * 8, 8)
        a8 = a_scr[pl.ds(r0, 8), :]
        u8 = u_scr[pl.ds(r0, 8), :]
        for s in (1, 2, 4):
            ok = (rowid < 8 - s) if rev else (rowid >= s)
            sh = (8 - s) if rev else s
            a_sh = jnp.where(ok, pltpu.roll(a8, sh, 0), 1.0)
            u_sh = jnp.where(ok, pltpu.roll(u8, sh, 0), 0.0)
            u8 = a8 * u_sh + u8
            a8 = a8 * a_sh
        h8 = u8 + a8 * h
        h_ref[0, 0, pl.ds(r0, 8), :] = h8
        return h8[0:1] if rev else h8[7:8]

    @pl.when(d == 0)
    def _():
        h_scr[...] = lax.fori_loop(0, tm // 8, lambda i, h: tile_scan(i, h, False), h_scr[...], unroll=2)

    @pl.when(d == 1)
    def _():
        h_scr[...] = lax.fori_loop(0, tm // 8, lambda i, h: tile_scan(i, h, True), h_scr[...], unroll=2)

    @pl.when(c == n_c - 1)
    def _():
        hT_ref[0, 0] = h_scr[...]


def rglru(xc, w_gate, b_gate, c_lam, h0, tm):
    b_, l_, d_ = xc.shape
    tm = min(tm, l_)
    n_c = l_ // tm
    tmap = lambda d, c: c + d * (n_c - 1 - 2 * c)
    dirp = lambda shape: pl.BlockSpec((1,) + shape, lambda d, b, c: (d,) + (0,) * len(shape))
    st = pl.BlockSpec((1, 1, 1, d_), lambda d, b, c: (d, b, 0, 0))
    return pl.pallas_call(
        _lru_kernel,
        grid=(2, b_, n_c),
        in_specs=[pl.BlockSpec((1, tm, d_), lambda d, b, c: (b, tmap(d, c), 0)),
                  dirp(w_gate.shape[1:]), dirp((1, 2 * d_)), dirp((1, d_)), st],
        out_specs=[pl.BlockSpec((1, 1, tm, d_), lambda d, b, c: (d, b, tmap(d, c), 0)), st],
        out_shape=[jax.ShapeDtypeStruct((2, b_, l_, d_), F32), jax.ShapeDtypeStruct((2, b_, 1, d_), F32)],
        scratch_shapes=[pltpu.VMEM((tm, d_), F32), pltpu.VMEM((tm, d_), F32), pltpu.VMEM((1, d_), F32)],
        compiler_params=pltpu.CompilerParams(
            dimension_semantics=("arbitrary", "arbitrary", "arbitrary"), vmem_limit_bytes=VMEM_LIMIT),
        name="rglru",
    )(xc, w_gate, b_gate, c_lam, h0)


def _post_kernel(yf_ref, yb_ref, r_ref, k_ref, v_ref, lz_ref, hf_ref, hb_ref, gb_ref, ga_ref, gm_ref, x_ref,
                 gt_ref, sc_ref, sh_ref, wa2_ref, a0_ref, g2_ref, ka_ref, rk_ref, gg_ref, gnb_ref,
                 pa_ref, pb_ref, wo_ref, lg_ref, lb_ref, rwh_ref, rwl_ref, rb_ref,
                 x1_ref, h2_ref, lg_out_ref):
    d_ = x_ref.shape[2]
    y = yf_ref[0] + yb_ref[0]
    mu = _head_sums(y) * (1.0 / HEAD)
    yc = y - mu
    var = _head_sums(yc * yc) * (1.0 / HEAD)
    yn = yc * lax.rsqrt(var + GN_EPS) * gg_ref[...] + gnb_ref[...]
    lz = lz_ref[0]
    a2 = _sigmoid(_dot(lz[:, :LORA_DECAY + LORA_AAA], wa2_ref[...]) + a0_ref[...])
    r, k, v = r_ref[0].astype(F32), k_ref[0].astype(F32), v_ref[0].astype(F32)
    ksum = k * (2.0 + (a2[:, :d_] + a2[:, d_:] - 2.0) * ka_ref[...])
    bonus = _head_sums(r * ksum * rk_ref[...]) * v
    g = _dot(lz[:, LORA_DECAY + LORA_AAA:], g2_ref[...])
    y_a = ((yn + bonus) * g).astype(BF16)
    gb = gb_ref[0]
    gelu = 0.5 * gb * (1.0 + jnp.tanh(0.7978845608028654 * (gb + 0.044715 * gb * gb * gb)))
    y_b = ((hf_ref[0, 0] + hb_ref[0, 0]) * gelu).astype(BF16)
    m = _sigmoid(ga_ref[0]) * _dot(y_a, pa_ref[...]) + _sigmoid(gm_ref[0]) * _dot(y_b, pb_ref[...])
    mix = _dot(m, wo_ref[...])
    x1 = _ln(ALPHA * _ln(x_ref[0]) + gt_ref[0] * mix) * lg_ref[...] + lb_ref[...]
    x1_ref[0] = x1
    h2 = x1 * (1.0 + sc_ref[0]) + sh_ref[0]
    h2_ref[0] = h2
    hh, hl = _split(h2, 2)
    lg_out_ref[0] = (_dot(hh, rwh_ref[...]) + _dot(hl, rwh_ref[...]) + _dot(hh, rwl_ref[...])) + rb_ref[...]


def post_mix(y_f, y_b, mixed, lz, hh, main, x, gt, sc, sh, params, tm):
    b_, l_, d_ = x.shape
    dsec = lambda e: pl.BlockSpec((1, 1, tm, d_), lambda b, i: (e, b, i, 0))
    col = lambda j: pl.BlockSpec((1, tm, d_), lambda b, i: (b, i, j))
    vec = pl.BlockSpec((1, 1, d_), lambda b, i: (b, 0, 0))
    full = lambda a: pl.BlockSpec(a.shape, lambda b, i: (0,) * a.ndim)
    return pl.pallas_call(
        _post_kernel,
        grid=(b_, l_ // tm),
        in_specs=[col(0), col(0), col(0), col(1), col(2),
                  pl.BlockSpec((1, tm, LORA_PAD), lambda b, i: (b, i, 0)),
                  dsec(0), dsec(1), col(4), col(5), col(6), col(0), vec, vec, vec] + [full(p) for p in params],
        out_specs=[col(0), col(0), pl.BlockSpec((1, tm, 128), lambda b, i: (b, i, 0))],
        out_shape=[jax.ShapeDtypeStruct((b_, l_, d_), F32), jax.ShapeDtypeStruct((b_, l_, d_), F32),
                   jax.ShapeDtypeStruct((b_, l_, 128), F32)],
        compiler_params=pltpu.CompilerParams(vmem_limit_bytes=VMEM_LIMIT),
        name="post_mix",
    )(y_f, y_b, mixed, mixed, mixed, lz, hh, hh, main, main, main, x, gt, sc, sh, *params)


def _mm_kernel(x_ref, w_ref, o_ref, *, pa, pb):
    o_ref[...] = _dot(x_ref[...], w_ref[...], _NN, pa, pb)


def pmm(x, w, tm=512, tn=1024, pa=1, pb=1):
    m_, k_ = x.shape
    n_ = w.shape[1]
    tm, tn = min(tm, m_), min(tn, n_)
    assert m_ % tm == 0 and n_ % tn == 0, (x.shape, w.shape, tm, tn)
    return pl.pallas_call(
        functools.partial(_mm_kernel, pa=pa, pb=pb),
        grid=(m_ // tm, n_ // tn),
        in_specs=[pl.BlockSpec((tm, k_), lambda i, j: (i, 0)), pl.BlockSpec((k_, tn), lambda i, j: (0, j))],
        out_specs=pl.BlockSpec((tm, tn), lambda i, j: (i, j)),
        out_shape=jax.ShapeDtypeStruct((m_, n_), F32),
        compiler_params=pltpu.CompilerParams(vmem_limit_bytes=VMEM_LIMIT),
        name="pmm",
    )(x, w)


ROUTE_LANES = 128
R_E, R_RANK, R_W = 0, 2, 4
DMA_UNROLL = 4


def _route_kernel(lg_ref, rec_ref, cnt_ref, carry):
    i = pl.program_id(0)
    tm = lg_ref.shape[0]

    @pl.when(i == 0)
    def _():
        carry[...] = jnp.zeros_like(carry)

    lg = lg_ref[...]
    lane = lax.broadcasted_iota(jnp.int32, lg.shape, 1)
    neg = -jnp.inf
    first = lambda m: jnp.min(jnp.where(m, lane, ROUTE_LANES), axis=1, keepdims=True)
    is_g = lane < N_GROUPS
    gmax = jnp.max(jnp.where(is_g, lg, neg), axis=1, keepdims=True)
    gsel = first(is_g & (lg == gmax))
    p_g = 1.0 / jnp.sum(jnp.where(is_g, jnp.exp(lg - gmax), 0.0), axis=1, keepdims=True)
    in_grp = (lane >= N_GROUPS) & (lane < N_GROUPS + N_EXPERTS) & ((lane - N_GROUPS) // EXPERTS_PER_GROUP == gsel)
    el = jnp.where(in_grp, lg, neg)
    v1 = jnp.max(el, axis=1, keepdims=True)
    i1 = first(in_grp & (el == v1))
    rest = in_grp & (lane != i1)
    el2 = jnp.where(rest, lg, neg)
    v2 = jnp.max(el2, axis=1, keepdims=True)
    i2 = first(rest & (el2 == v2))
    e21 = jnp.exp(v2 - v1)
    w1 = p_g / (1.0 + e21)
    w2 = w1 * e21
    oh1, oh2 = lane == i1, lane == i2
    both = jnp.where(oh1 | oh2, 1.0, 0.0)
    row = lax.broadcasted_iota(jnp.int32, (tm, tm), 0)
    col = lax.broadcasted_iota(jnp.int32, (tm, tm), 1)
    before = jnp.where(col < row, 1.0, 0.0).astype(BF16)
    cnt = _dot(before, both.astype(BF16)) + carry[...]
    rank1 = jnp.sum(jnp.where(oh1, cnt, 0.0), axis=1, keepdims=True)
    rank2 = jnp.sum(jnp.where(oh2, cnt, 0.0), axis=1, keepdims=True)
    carry[...] = carry[...] + jnp.sum(both, axis=0, keepdims=True)
    rec = jnp.zeros_like(lg)
    for k, val in ((R_E, (i1 - N_GROUPS).astype(F32)), (R_E + 1, (i2 - N_GROUPS).astype(F32)),
                   (R_RANK, rank1), (R_RANK + 1, rank2), (R_W, w1), (R_W + 1, w2)):
        rec = jnp.where(lane == k, val, rec)
    rec_ref[...] = rec
    cnt_ref[...] = carry[...]


def route(logits, tm=256):
    t_ = logits.shape[0]
    return pl.pallas_call(
        _route_kernel,
        grid=(t_ // tm,),
        in_specs=[pl.BlockSpec((tm, ROUTE_LANES), lambda i: (i, 0))],
        out_specs=[pl.BlockSpec((tm, ROUTE_LANES), lambda i: (i, 0)), pl.BlockSpec((1, ROUTE_LANES), lambda i: (0, 0))],
        out_shape=[jax.ShapeDtypeStruct((t_, ROUTE_LANES), F32), jax.ShapeDtypeStruct((1, ROUTE_LANES), F32)],
        scratch_shapes=[pltpu.VMEM((1, ROUTE_LANES), F32)],
        compiler_params=pltpu.CompilerParams(dimension_semantics=("arbitrary",)),
        name="route",
    )(logits)


def _dispatch_kernel(dest_ref, zrow_ref, h_ref, xs_ref, zbuf, sem, zsem):
    tm = h_ref.shape[0]
    base = pl.program_id(0) * tm * TOP_K

    @pl.when(pl.program_id(0) == 0)
    def _():
        zbuf[...] = jnp.zeros_like(zbuf)
        n_blocks = xs_ref.shape[0] // MOE_ROWS

        def zero_copy(row0):
            return pltpu.make_async_copy(zbuf, xs_ref.at[pl.ds(pl.multiple_of(row0, MOE_ROWS), MOE_ROWS)], zsem)

        def tail(fn):
            return lax.fori_loop(zrow_ref[N_EXPERTS] // MOE_ROWS, n_blocks, lambda b, c: (fn(zero_copy(b * MOE_ROWS)), c)[1], 0)

        for e in range(N_EXPERTS):
            @pl.when(zrow_ref[e] >= 0)
            def _():
                zero_copy(zrow_ref[e]).start()
        tail(lambda cp: cp.start())
        for e in range(N_EXPERTS):
            @pl.when(zrow_ref[e] >= 0)
            def _():
                zero_copy(zrow_ref[e]).wait()
        tail(lambda cp: cp.wait())

    def copy(r, s):
        return pltpu.make_async_copy(h_ref.at[pl.ds(r, 1)], xs_ref.at[pl.ds(dest_ref[base + r * TOP_K + s], 1)], sem)

    def start(r, carry):
        for s in range(TOP_K):
            copy(r, s).start()
        return carry

    def wait(r, carry):
        for s in range(TOP_K):
            copy(r, s).wait()
        return carry

    lax.fori_loop(0, tm, start, 0, unroll=DMA_UNROLL)
    lax.fori_loop(0, tm, wait, 0, unroll=DMA_UNROLL)


def dispatch(dest, zero_row, h, n_pad, tm=512):
    t_, d_ = h.shape
    grid_spec = pltpu.PrefetchScalarGridSpec(
        num_scalar_prefetch=2,
        grid=(t_ // tm,),
        in_specs=[pl.BlockSpec((tm, d_), lambda i, dest, zr: (i, 0))],
        out_specs=pl.BlockSpec(memory_space=pl.ANY),
        scratch_shapes=[pltpu.VMEM((MOE_ROWS, d_), F32), pltpu.SemaphoreType.DMA(()), pltpu.SemaphoreType.DMA(())],
    )
    return pl.pallas_call(
        _dispatch_kernel,
        grid_spec=grid_spec,
        out_shape=jax.ShapeDtypeStruct((n_pad, d_), F32),
        compiler_params=pltpu.CompilerParams(dimension_semantics=("arbitrary",), has_side_effects=True),
        name="dispatch",
    )(dest, zero_row, h)


def _moe_kernel(be_ref, nv_ref, x_ref, w1_ref, w3_ref, w2_ref, o_ref, w1b, w3b, w2b):
    blk = pl.program_id(0)

    @pl.when((blk == 0) | (be_ref[blk] != be_ref[jnp.maximum(blk - 1, 0)]))
    def _():
        w1b[...] = w1_ref[0].astype(BF16)
        w3b[...] = w3_ref[0].astype(BF16)
        w2b[...] = w2_ref[0].astype(BF16)

    @pl.when(nv_ref[blk] > 0)
    def _():
        x = x_ref[...].astype(BF16)
        h1 = _dot(x, w1b[...])
        h3 = _dot(x, w3b[...])
        hh = h1 * _sigmoid(h1) * h3
        o_ref[...] = _dot(hh, w2b[...])

    @pl.when(nv_ref[blk] == 0)
    def _():
        o_ref[...] = jnp.zeros_like(o_ref)


def moe_experts(xs, blk_e, blk_n, w1, w3, w2):
    n_pad, d_ = xs.shape
    n_blocks = n_pad // MOE_ROWS
    de = w1.shape[2]
    grid_spec = pltpu.PrefetchScalarGridSpec(
        num_scalar_prefetch=2,
        grid=(n_blocks,),
        in_specs=[pl.BlockSpec((MOE_ROWS, d_), lambda i, be, nv: (i, 0)),
                  pl.BlockSpec((1, d_, de), lambda i, be, nv: (be[i], 0, 0)),
                  pl.BlockSpec((1, d_, de), lambda i, be, nv: (be[i], 0, 0)),
                  pl.BlockSpec((1, de, d_), lambda i, be, nv: (be[i], 0, 0))],
        out_specs=pl.BlockSpec((MOE_ROWS, d_), lambda i, be, nv: (i, 0)),
        scratch_shapes=[pltpu.VMEM((d_, de), BF16), pltpu.VMEM((d_, de), BF16), pltpu.VMEM((de, d_), BF16)],
    )
    return pl.pallas_call(
        _moe_kernel,
        grid_spec=grid_spec,
        out_shape=jax.ShapeDtypeStruct((n_pad, d_), F32),
        compiler_params=pltpu.CompilerParams(dimension_semantics=("arbitrary",), vmem_limit_bytes=VMEM_LIMIT),
        name="moe_experts",
    )(blk_e, blk_n, xs, w1, w3, w2)


def _combine_kernel(dest_ref, x_ref, rec_ref, gt_ref, g_ref, b_ref, ys_ref, o_ref, buf, sem):
    tm = x_ref.shape[1]
    base = (pl.program_id(0) * pl.num_programs(1) + pl.program_id(1)) * tm * TOP_K

    def copy(r, s):
        return pltpu.make_async_copy(ys_ref.at[pl.ds(dest_ref[base + r * TOP_K + s], 1)], buf.at[s, pl.ds(r, 1)], sem)

    def start(r, carry):
        for s in range(TOP_K):
            copy(r, s).start()
        return carry

    def wait(r, carry):
        for s in range(TOP_K):
            copy(r, s).wait()
        return carry

    lax.fori_loop(0, tm, start, 0, unroll=DMA_UNROLL)
    lax.fori_loop(0, tm, wait, 0, unroll=DMA_UNROLL)
    rec = rec_ref[...]
    moe = rec[:, R_W:R_W + 1] * buf[0]
    for s in range(1, TOP_K):
        moe = moe + rec[:, R_W + s:R_W + s + 1] * buf[s]
    o_ref[0] = _ln(ALPHA * x_ref[0] + gt_ref[0] * moe) * g_ref[...] + b_ref[...]


def combine_ln(dest, x, rec, gt, g, b, ys, tm=512):
    b_, l_, d_ = x.shape
    n_i = l_ // tm
    grid_spec = pltpu.PrefetchScalarGridSpec(
        num_scalar_prefetch=1,
        grid=(b_, n_i),
        in_specs=[pl.BlockSpec((1, tm, d_), lambda bi, i, dest: (bi, i, 0)),
                  pl.BlockSpec((tm, ROUTE_LANES), lambda bi, i, dest: (bi * n_i + i, 0)),
                  pl.BlockSpec((1, 1, d_), lambda bi, i, dest: (bi, 0, 0)),
                  pl.BlockSpec((1, d_), lambda bi, i, dest: (0, 0)),
                  pl.BlockSpec((1, d_), lambda bi, i, dest: (0, 0)),
                  pl.BlockSpec(memory_space=pl.ANY)],
        out_specs=pl.BlockSpec((1, tm, d_), lambda bi, i, dest: (bi, i, 0)),
        scratch_shapes=[pltpu.VMEM((TOP_K, tm, d_), F32), pltpu.SemaphoreType.DMA(())],
    )
    return pl.pallas_call(
        _combine_kernel,
        grid_spec=grid_spec,
        out_shape=jax.ShapeDtypeStruct(x.shape, F32),
        compiler_params=pltpu.CompilerParams(
            dimension_semantics=("arbitrary", "arbitrary"), vmem_limit_bytes=VMEM_LIMIT),
        name="combine_ln",
    )(dest, x, rec, gt, g.reshape(1, d_), b.reshape(1, d_), ys)


def hier_moe_ln(x1, h2, logits, gt, g, b, w1, w3, w2):
    b_, l_, d_ = x1.shape
    t_ = b_ * l_
    rec, cnt = route(logits)
    counts = cnt[0, N_GROUPS:N_GROUPS + N_EXPERTS].astype(jnp.int32)
    padded = (counts + MOE_ROWS - 1) // MOE_ROWS * MOE_ROWS
    pad_end = jnp.cumsum(padded)
    pad_start = pad_end - padded
    eid = rec[:, R_E:R_E + TOP_K].astype(jnp.int32)
    rank = rec[:, R_RANK:R_RANK + TOP_K].astype(jnp.int32)
    onehot = eid[..., None] == jnp.arange(N_EXPERTS, dtype=jnp.int32)
    dest = (rank + jnp.sum(jnp.where(onehot, pad_start, 0), -1)).reshape(-1)
    n_blocks = -(-(t_ * TOP_K) // MOE_ROWS) + N_EXPERTS
    blk_lo = jnp.arange(n_blocks, dtype=jnp.int32) * MOE_ROWS
    blk_e = jnp.minimum(jnp.sum(blk_lo[:, None] >= pad_end[None, :], -1), N_EXPERTS - 1).astype(jnp.int32)
    blk_n = jnp.clip(jnp.sum(jnp.where(blk_e[:, None] == jnp.arange(N_EXPERTS), pad_start + counts, 0), -1) - blk_lo,
                     0, MOE_ROWS).astype(jnp.int32)
    zero_row = jnp.concatenate([jnp.where(counts > 0, pad_end - MOE_ROWS, -1), pad_end[-1:]]).astype(jnp.int32)
    xs = dispatch(dest, zero_row, h2.reshape(t_, d_), n_blocks * MOE_ROWS)
    ys = moe_experts(xs, blk_e, blk_n, w1, w3, w2)
    return combine_ln(dest, x1, rec, gt, g, b, ys)


def _block_diag_groups(w, grp):
    n = w.shape[0]
    wg = w.reshape(n // grp, grp, HEAD, HEAD)
    eye = jnp.eye(grp, dtype=w.dtype)
    return jnp.einsum("gaij,ab->gaibj", wg, eye).reshape(n // grp, grp * HEAD, grp * HEAD)


def token_scans(h_in, sc, sh, wts, grid_mode, states):
    b_, l_, d_ = h_in.shape
    main, lora = inproj(h_in, sc, sh, wts["w_main"], wts["w_lora"], min(1024, l_))
    mixed, lz, xc = rwkv_mix(main, lora, 3, wts["mu"], wts["mu_l"], wts["k_k"], wts["conv_w"], wts["conv_b"], grid_mode,
                             512 if grid_mode else l_)
    y_f, y_b, s_new = wkv7(mixed, lz, wts["w_lo"], wts["wa0"], wts["k_a"], states[0])
    hh, h_new = rglru(xc, wts["w_gate"], wts["b_gate"], wts["c_lam"], states[1], 1024)
    return (main, mixed, lz, y_f, y_b, hh), (s_new, h_new)


def kernel(x, c, ctx, c_ctx, w_ada, b_ada, w_in, mu_a, w0, w2, a0, a2, g2, k_k, k_a, r_k, gn_g, gn_b, conv_w, conv_b, lru_wa, lru_ba, lru_wx, lru_bx, lru_lam, p_a, p_b, w_o, ln1_g, ln1_b, router_g, router_g_b, router_e, router_e_b, e_w1, e_w3, e_w2, ln2_g, ln2_b):
    b_, l_, d_ = x.shape
    heads = d_ // HEAD
    l = 0
    row = lambda v: v.reshape(1, -1)
    cc = jnp.concatenate([c, c_ctx[None]], 0)
    cc = jnp.pad(jax.nn.silu(cc), ((0, 8 - cc.shape[0]), (0, 0)))
    mod = pmm(cc, w_ada[l], 8, 1024, 2, 2)[:b_ + 1] + b_ada[l]
    mods = jnp.split(mod, 6, axis=-1)
    sh1, sc1, gt1, sh2, sc2, gt2 = [m[:b_, None, :] for m in mods]
    csh1, csc1 = [jnp.broadcast_to(m[b_:, None, :], (b_, 1, d_)) for m in mods[:2]]

    a_slab = 3 * d_ + LORA_DECAY + LORA_AAA + LORA_GATE
    n_lora = a_slab - 3 * d_
    wi, mu = w_in[l], mu_a[l]
    zeros_lo = jnp.zeros((LORA_DECAY, d_), F32)
    w_lo = jnp.stack([jnp.concatenate([jnp.concatenate([w2[l, e], zeros_lo], 1),
                                       jnp.concatenate([zeros_lo, a2[l, e]], 1)], 0) for e in range(2)])
    wts = dict(
        w_main=jnp.concatenate([wi[:, :3 * d_], wi[:, a_slab:]], 1).astype(BF16),
        w_lora=jnp.pad(wi[:, 3 * d_:a_slab], ((0, 0), (0, LORA_PAD - n_lora))).astype(BF16),
        mu=row(mu[:3 * d_]), mu_l=row(jnp.pad(mu[3 * d_:], (0, LORA_PAD - n_lora))), k_k=row(k_k[l]), k_a=row(k_a[l]),
        w_lo=w_lo.astype(BF16),
        wa0=jnp.concatenate([w0[l], a0[l]], -1)[:, None, :],
        conv_w=conv_w[l], conv_b=row(conv_b[l]),
        w_gate=jnp.stack([jnp.concatenate([_block_diag_groups(lru_wa[l, e], 4), _block_diag_groups(lru_wx[l, e], 4)], -1)
                          for e in range(2)]).astype(BF16),
        b_gate=jnp.concatenate([lru_ba[l], lru_bx[l]], -1)[:, None, :],
        c_lam=(-LRU_C * jax.nn.softplus(-lru_lam[l]))[:, None, :])

    s0 = jnp.zeros((2, b_, heads // 2, 2 * HEAD, 2 * HEAD), F32)
    h0 = jnp.zeros((2, b_, 1, d_), F32)
    _, ctx_states = token_scans(ctx, csc1, csh1, wts, False, (s0, h0))
    (main, mixed, lz, y_f, y_b, hh), _ = token_scans(x, sc1, sh1, wts, True, ctx_states)

    zeros_a = jnp.zeros((LORA_DECAY, 2 * d_), F32)
    rw = jnp.pad(jnp.concatenate([router_g[l], router_e[l]], 1), ((0, 0), (0, 128 - N_GROUPS - N_EXPERTS)))
    rw_hi = rw.astype(BF16)
    params = [
        jnp.concatenate([zeros_a, jnp.concatenate([a2[l, 0], a2[l, 1]], 1)], 0).astype(BF16),
        row(jnp.concatenate([a0[l, 0], a0[l, 1]])),
        jnp.pad(g2[l], ((0, LORA_PAD - LORA_DECAY - LORA_AAA - LORA_GATE), (0, 0))).astype(BF16),
        row(k_a[l]), row(r_k[l]), row(gn_g[l]), row(gn_b[l]),
        p_a[l].astype(BF16), p_b[l].astype(BF16), w_o[l].astype(BF16), row(ln1_g[l]), row(ln1_b[l]),
        rw_hi, (rw - rw_hi.astype(F32)).astype(BF16),
        row(jnp.pad(jnp.concatenate([router_g_b[l], router_e_b[l]]), (0, 128 - N_GROUPS - N_EXPERTS)))]
    x1, h2, logits = post_mix(y_f, y_b, mixed, lz, hh, main, x, gt1, sc2, sh2, params, 256)

    return hier_moe_ln(x1, h2, logits.reshape(b_ * l_, -1), gt2, ln2_g[l], ln2_b[l], e_w1[l], e_w3[l], e_w2[l])
```

```python
import functools

import jax
import jax.numpy as jnp
from jax import lax
from jax.experimental import pallas as pl
from jax.experimental.pallas import tpu as pltpu

F32 = jnp.float32
BF16 = jnp.bfloat16

GRID_W = 64
HEAD = 64
LORA_DECAY = 64
LORA_AAA = 64
LORA_GATE = 160
LORA_PAD = 384
GN_EPS = 64e-5
LN_EPS = 1e-5
CONV_W = 5
LRU_C = 8.0
N_GROUPS = 4
EXPERTS_PER_GROUP = 8
N_EXPERTS = N_GROUPS * EXPERTS_PER_GROUP
TOP_K = 2
DEPTH = 1
ALPHA = (2 * DEPTH) ** 0.25

WKV_CHUNK = 64
MOE_ROWS = 512
GATE_GROUP = 4 * HEAD
HALO = 8
VMEM_LIMIT = 56 * 1024 * 1024

_NN = (((1,), (0,)), ((), ()))
_NT = (((1,), (1,)), ((), ()))
_TN = (((0,), (0,)), ((), ()))


def _split(x, n):
    if x.dtype == BF16:
        return [x]
    parts, rest = [], x.astype(F32)
    for i in range(n):
        p = rest.astype(BF16)
        parts.append(p)
        if i + 1 < n:
            rest = rest - p.astype(F32)
    return parts


def _dot(a, b, dims=_NN, pa=1, pb=1):
    ap, bp = _split(a, pa), _split(b, pb)
    order = max(len(ap), len(bp))
    acc = None
    for i, x in enumerate(ap):
        for j, y in enumerate(bp):
            if i + j < order:
                t = lax.dot_general(x, y, dims, preferred_element_type=F32)
                acc = t if acc is None else acc + t
    return acc


def _ln(x):
    mu = jnp.mean(x, -1, keepdims=True)
    xc = x - mu
    var = jnp.mean(xc * xc, -1, keepdims=True)
    return xc * lax.rsqrt(var + LN_EPS)


def _sigmoid(x):
    return 0.5 + 0.5 * jnp.tanh(0.5 * x)


def _softplus(x):
    return jnp.maximum(x, 0.0) + jnp.log(1.0 + jnp.exp(-jnp.abs(x)))


def _head_sums(x):
    pw = 2 * HEAD
    row = lax.broadcasted_iota(jnp.int32, (pw, pw), 0)
    col = lax.broadcasted_iota(jnp.int32, (pw, pw), 1)
    ones = jnp.where((row >= HEAD) == (col >= HEAD), 1.0, 0.0).astype(BF16)
    return jnp.concatenate([_dot(x[:, p:p + pw], ones, _NN, 2, 1) for p in range(0, x.shape[1], pw)], axis=1)


def _inproj_kernel(x_ref, sc_ref, sh_ref, wm_ref, wl_ref, main_ref, lora_ref, h_scr, *, nm):
    j = pl.program_id(2)

    @pl.when(j == 0)
    def _():
        h_scr[...] = (_ln(x_ref[0]) * (1.0 + sc_ref[0]) + sh_ref[0]).astype(BF16)

    @pl.when(j < nm)
    def _():
        main_ref[0] = _dot(h_scr[...], wm_ref[j])

    @pl.when(j == nm)
    def _():
        lora_ref[0] = _dot(h_scr[...], wl_ref[...])


def inproj(x, sc, sh, w_main, w_lora, tm):
    b_, l_, d_ = x.shape
    nm = w_main.shape[0]
    row = pl.BlockSpec((1, tm, d_), lambda b, i, j: (b, i, 0))
    vec = pl.BlockSpec((1, 1, d_), lambda b, i, j: (b, 0, 0))
    return pl.pallas_call(
        functools.partial(_inproj_kernel, nm=nm),
        grid=(b_, l_ // tm, nm + 1),
        in_specs=[row, vec, vec,
                  pl.BlockSpec((nm, d_, d_), lambda b, i, j: (0, 0, 0)),
                  pl.BlockSpec((d_, LORA_PAD), lambda b, i, j: (0, 0))],
        out_specs=[pl.BlockSpec((1, tm, d_), lambda b, i, j: (b, i, jnp.minimum(j, nm - 1))),
                   pl.BlockSpec((1, tm, LORA_PAD), lambda b, i, j: (b, i, 0))],
        out_shape=[jax.ShapeDtypeStruct((b_, l_, nm * d_), F32), jax.ShapeDtypeStruct((b_, l_, LORA_PAD), F32)],
        scratch_shapes=[pltpu.VMEM((tm, d_), BF16)],
        compiler_params=pltpu.CompilerParams(vmem_limit_bytes=VMEM_LIMIT),
        name="inproj",
    )(x, sc, sh, w_main, w_lora)


def _mix_kernel(cur_ref, prev_ref, next_ref, lcur_ref, lprev_ref, lnext_ref, xcur_ref, xprev_ref, xnext_ref,
                mu_ref, mul_ref, kk_ref, cw_ref, cb_ref, mixed_ref, lz_ref, xc_ref, *, grid_mode):
    i = pl.program_id(1)
    n_i = pl.num_programs(1)
    tm = cur_ref.shape[1]
    d_ = kk_ref.shape[1]

    def shifted(cur, prev, nxt):
        rows, ch = cur.shape
        rowi = lax.broadcasted_iota(jnp.int32, (rows, ch), 0)
        lane = lax.broadcasted_iota(jnp.int32, (rows, ch), 1)
        if grid_mode:
            prev = jnp.where(i > 0, prev, 0.0)
            nxt = jnp.where(i < n_i - 1, nxt, 0.0)
            up = jnp.concatenate([prev, cur[:rows - GRID_W]], axis=0)
            down = jnp.concatenate([cur[GRID_W:], nxt], axis=0)
            col = rowi % GRID_W
            left = jnp.where(col == 0, 0.0, pltpu.roll(cur, 1, 0))
            right = jnp.where(col == GRID_W - 1, 0.0, pltpu.roll(cur, rows - 1, 0))
            l4 = lane % 4
            return jnp.where(l4 == 0, left, jnp.where(l4 == 1, right, jnp.where(l4 == 2, up, down)))
        before = jnp.where(rowi == 0, 0.0, pltpu.roll(cur, 1, 0))
        after = jnp.where(rowi == rows - 1, 0.0, pltpu.roll(cur, rows - 1, 0))
        return jnp.where(lane % 2 == 0, before, after)

    def mixed(cur, prev, nxt, mu):
        return cur + mu * (shifted(cur, prev, nxt) - cur)

    for s in range(3):
        sl = slice(s * d_, (s + 1) * d_)
        z = mixed(cur_ref[0, :, sl], prev_ref[0, :, sl], next_ref[0, :, sl], mu_ref[:, sl])
        mixed_ref[0, :, sl] = z.astype(BF16)
        if s == 1:
            kq = z * kk_ref[...]
            kq = kq * lax.rsqrt(_head_sums(kq * kq) + 1e-12)
            mixed_ref[0, :, 3 * d_:4 * d_] = kq.astype(BF16)
    lz = mixed(lcur_ref[0], lprev_ref[0], lnext_ref[0], mul_ref[...])
    wa = lz[:, :LORA_DECAY + LORA_AAA]
    lane = lax.broadcasted_iota(jnp.int32, wa.shape, 1)
    lz_ref[0, :, :LORA_DECAY + LORA_AAA] = jnp.where(lane < LORA_DECAY, jnp.tanh(wa), wa).astype(BF16)
    lz_ref[0, :, LORA_DECAY + LORA_AAA:] = _sigmoid(lz[:, LORA_DECAY + LORA_AAA:]).astype(BF16)

    if grid_mode:
        before = jnp.where(i > 0, xprev_ref[0, GRID_W - HALO:, :], 0.0)
        after = jnp.where(i < n_i - 1, xnext_ref[0, :HALO, :], 0.0)
    else:
        before = after = jnp.zeros((HALO, d_), F32)
    ext = jnp.concatenate([before, xcur_ref[0], after], axis=0)
    xc = cb_ref[...]
    for j in range(CONV_W):
        o = HALO - CONV_W // 2 + j
        xc = xc + cw_ref[j:j + 1, :] * ext[o:o + tm]
    xc_ref[0] = xc


def rwkv_mix(main, lora, xb_col, mu, mu_l, k_k, conv_w, conv_b, grid_mode, tm):
    b_, l_, _ = main.shape
    d_ = k_k.shape[-1]
    if grid_mode:
        assert tm % GRID_W == 0 and l_ % tm == 0
        halo, per = GRID_W, tm // GRID_W
    else:
        assert tm == l_
        halo, per = 8, tm // 8
    n_h = l_ // halo
    cur = lambda w: pl.BlockSpec((1, tm, w), lambda b, i: (b, i, 0))
    prv = lambda w: pl.BlockSpec((1, halo, w), lambda b, i: (b, jnp.maximum(i * per - 1, 0), 0))
    nxt = lambda w: pl.BlockSpec((1, halo, w), lambda b, i: (b, jnp.minimum((i + 1) * per, n_h - 1), 0))
    par = lambda w: pl.BlockSpec((1, w), lambda b, i: (0, 0))
    xcur = pl.BlockSpec((1, tm, d_), lambda b, i: (b, i, xb_col))
    xprv = pl.BlockSpec((1, halo, d_), lambda b, i: (b, jnp.maximum(i * per - 1, 0), xb_col))
    xnxt = pl.BlockSpec((1, halo, d_), lambda b, i: (b, jnp.minimum((i + 1) * per, n_h - 1), xb_col))
    return pl.pallas_call(
        functools.partial(_mix_kernel, grid_mode=grid_mode),
        grid=(b_, l_ // tm),
        in_specs=[cur(3 * d_), prv(3 * d_), nxt(3 * d_), cur(LORA_PAD), prv(LORA_PAD), nxt(LORA_PAD), xcur, xprv, xnxt,
                  par(3 * d_), par(LORA_PAD), par(d_), pl.BlockSpec((CONV_W, d_), lambda b, i: (0, 0)), par(d_)],
        out_specs=[cur(4 * d_), cur(LORA_PAD), cur(d_)],
        out_shape=[jax.ShapeDtypeStruct((b_, l_, 4 * d_), BF16), jax.ShapeDtypeStruct((b_, l_, LORA_PAD), BF16),
                   jax.ShapeDtypeStruct((b_, l_, d_), F32)],
        compiler_params=pltpu.CompilerParams(vmem_limit_bytes=VMEM_LIMIT),
        name="rwkv_mix",
    )(main, main, main, lora, lora, lora, main, main, main, mu, mu_l, k_k, conv_w, conv_b)


def _wkv_kernel(rf_ref, kf_ref, vf_ref, kkf_ref, lzf_ref, rb_ref, kb_ref, vb_ref, kkb_ref, lzb_ref,
                wlo_ref, wa0_ref, ka_ref, s0_ref, yf_ref, yb_ref, sT_ref, s_scr):
    c = pl.program_id(1)
    n_c = pl.num_programs(1)
    cs, d_ = rf_ref.shape[1], rf_ref.shape[2]
    pairs = s_scr.shape[1]
    pw = 2 * HEAD

    @pl.when(c == 0)
    def _():
        s_scr[...] = s0_ref[:, 0]

    row = lax.broadcasted_iota(jnp.int32, (cs, cs), 0)
    col = lax.broadcasted_iota(jnp.int32, (cs, cs), 1)
    row2 = lax.broadcasted_iota(jnp.int32, (2 * cs, pw), 0)
    col2 = lax.broadcasted_iota(jnp.int32, (2 * cs, pw), 1)
    same_head = (row2 >= cs) == (col2 >= HEAD)
    lane = lax.broadcasted_iota(jnp.int32, (cs, pw), 1)
    m0, m1 = lane < HEAD, lane >= HEAD
    mm0, mm1 = col2 < HEAD, col2 >= HEAD
    cat = lambda x, y: jnp.concatenate([x, y], axis=0)
    dot = lambda x, y, dims=_NN: lax.dot_general(x, y, dims, preferred_element_type=F32)
    sel = lambda m, x: jnp.where(m, x, jnp.zeros_like(x))

    streams = []
    for d, (r_ref, k_ref, v_ref, kk_ref, lz_ref, y_ref) in enumerate(
            ((rf_ref, kf_ref, vf_ref, kkf_ref, lzf_ref, yf_ref), (rb_ref, kb_ref, vb_ref, kkb_ref, lzb_ref, yb_ref))):
        tdiff = (row2 % cs - col2 % cs) * (1 - 2 * d)
        amask = (tdiff > 0) | ((row2 >= cs) & (tdiff == 0))
        tri = jnp.where((row - col) * (1 - 2 * d) >= 0, 1.0, 0.0).astype(BF16)
        z = _dot(lz_ref[0, :, :LORA_DECAY + LORA_AAA], wlo_ref[d]) + wa0_ref[d]
        lw = -jnp.exp(-_softplus(-z[:, :d_]) - 0.5)
        a = _sigmoid(z[:, d_:])
        r, kk = r_ref[0].astype(F32), kk_ref[0].astype(F32)
        k = k_ref[0].astype(F32) * (1.0 + (a - 1.0) * ka_ref[...])
        cum = _dot(tri, lw, _NN, 1, 3)
        tot = jnp.sum(lw, axis=0, keepdims=True)
        b = kk * a
        e_neg = jnp.exp(-cum)
        e_end = jnp.exp(tot - cum)
        rt = (r * jnp.exp(cum)).astype(BF16)
        at = (-kk * jnp.exp(cum - lw)).astype(BF16)
        bt = (b * e_neg).astype(BF16)
        kt = (k * e_neg).astype(BF16)
        bd = (b * e_end).astype(BF16)
        kd = (k * e_end).astype(BF16)
        vb = v_ref[0]
        p_end = jnp.exp(tot)
        for p in range(pairs):
            sl = slice(p * pw, (p + 1) * pw)
            streams.append(dict(d=d, p=p, sl=sl, y_ref=y_ref, amask=amask, ar=cat(at[:, sl], rt[:, sl]),
                                bt=bt[:, sl], kt=kt[:, sl], bdkd=cat(bd[:, sl], kd[:, sl]), v=vb[:, sl],
                                p_end=p_end[:, sl]))

    bd = lambda x: cat(sel(m0, x), sel(m1, x))
    lcat = lambda x, y: jnp.concatenate([x, y], axis=1)
    eye_p = jnp.where(lax.broadcasted_iota(jnp.int32, (cs, pw), 0) == lane % cs, 1.0, 0.0)
    for st in streams:
        q = dot(cat(sel(mm0, st["ar"]), sel(mm1, st["ar"])),
                cat(jnp.where(m0, st["bt"], st["kt"]), jnp.where(m0, st["kt"], st["bt"])), _NT)
        q0, q1 = sel(st["amask"], q[:2 * cs]), sel(st["amask"], q[2 * cs:])
        st["nil"] = jnp.where(m0, q0[:cs], q1[:cs])
        q0, q1 = q0.astype(BF16), q1.astype(BF16)
        st["ak"] = jnp.where(m0, q1[:cs], q0[:cs])
        st["rbk"] = lcat(q0[cs:], q1[cs:])
    for st in streams:
        nb = st["nil"].astype(BF16)
        st["t"] = eye_p + st["nil"]
        st["nb"] = dot(nb, bd(nb)).astype(BF16)
    for i in range(max(cs.bit_length() - 3, 0)):
        for st in streams:
            xx = dot(cat(st["t"].astype(BF16), st["nb"]), bd(st["nb"]))
            st["t"] = st["t"] + xx[:cs]
            st["nb"] = xx[cs:].astype(BF16)
    for st in streams:
        st["t"] = (st["t"] + dot(st["t"].astype(BF16), bd(st["nb"]))).astype(BF16)

    zero = jnp.zeros((cs, pw), BF16)
    for st in streams:
        st["s"] = s_scr[st["d"], st["p"]]
        st["as"] = dot(lcat(st["ar"], cat(st["ak"], zero)),
                       cat(st["s"].astype(BF16), cat(sel(m1, st["v"]), sel(m0, st["v"]))))
    for st in streams:
        st["u"] = dot(st["t"], bd(st["as"][:cs].astype(BF16))).astype(BF16)
    for st in streams:
        u_p, v_p = st["u"], st["v"]
        y = st["as"][cs:] + dot(st["rbk"], cat(cat(sel(m0, u_p), sel(m0, v_p)), cat(sel(m1, v_p), sel(m1, u_p))))
        st["y_ref"][0, :, st["sl"]] = y
        p_col = jnp.broadcast_to(st["p_end"], (pw, pw)).T
        upd = dot(st["bdkd"], cat(u_p, v_p), _TN)
        s_scr[st["d"], st["p"]] = p_col * st["s"] + sel(same_head, upd)

    @pl.when(c == n_c - 1)
    def _():
        sT_ref[:, 0] = s_scr[...]


def _pair_states(s):
    st = jnp.swapaxes(s, -1, -2)
    st = st.reshape(s.shape[:-3] + (s.shape[-3] // 2, 2, HEAD, HEAD))
    eye = jnp.eye(2, dtype=s.dtype)
    out = jnp.einsum("...pakv,ab->...pakbv", st, eye)
    return out.reshape(s.shape[:-3] + (s.shape[-3] // 2, 2 * HEAD, 2 * HEAD))


def wkv7(mixed, lz, w_lo, wa0, k_a, s0p):
    b_, l_, d4 = mixed.shape
    d_ = d4 // 4
    pairs, pw = d_ // (2 * HEAD), 2 * HEAD
    cs = WKV_CHUNK
    assert cs == HEAD and l_ % cs == 0
    n_c = l_ // cs

    fwd = lambda w, j: pl.BlockSpec((1, cs, w), lambda b, c: (b, c, j))
    bwd = lambda w, j: pl.BlockSpec((1, cs, w), lambda b, c: (b, n_c - 1 - c, j))
    full = lambda a: pl.BlockSpec(a.shape, lambda b, c: (0,) * a.ndim)
    state = pl.BlockSpec((2, 1, pairs, pw, pw), lambda b, c: (0, b, 0, 0, 0))
    return pl.pallas_call(
        _wkv_kernel,
        grid=(b_, n_c),
        in_specs=[fwd(d_, 0), fwd(d_, 1), fwd(d_, 2), fwd(d_, 3), fwd(LORA_PAD, 0),
                  bwd(d_, 0), bwd(d_, 1), bwd(d_, 2), bwd(d_, 3), bwd(LORA_PAD, 0),
                  full(w_lo), full(wa0), full(k_a), state],
        out_specs=[fwd(d_, 0), bwd(d_, 0), state],
        out_shape=[jax.ShapeDtypeStruct((b_, l_, d_), F32), jax.ShapeDtypeStruct((b_, l_, d_), F32),
                   jax.ShapeDtypeStruct((2, b_, pairs, pw, pw), F32)],
        scratch_shapes=[pltpu.VMEM((2, pairs, pw, pw), F32)],
        compiler_params=pltpu.CompilerParams(
            dimension_semantics=("arbitrary", "arbitrary"), vmem_limit_bytes=VMEM_LIMIT),
        name="wkv7",
    )(mixed, mixed, mixed, mixed, lz, mixed, mixed, mixed, mixed, lz, w_lo, wa0, k_a, s0p)


def _lru_kernel(xc_ref, wg_ref, bg_ref, cl_ref, h0_ref, h_ref, hT_ref, a_scr, u_scr, h_scr):
    d = pl.program_id(0)
    c = pl.program_id(2)
    n_c = pl.num_programs(2)
    tm, d_ = a_scr.shape
    gw = GATE_GROUP

    @pl.when(c == 0)
    def _():
        h_scr[...] = h0_ref[0, 0]

    xc = xc_ref[0]
    for g in range(d_ // gw):
        sl = slice(g * gw, (g + 1) * gw)
        z = _dot(xc[:, sl], wg_ref[0, g])
        rg = _sigmoid(z[:, :gw] + bg_ref[0, :, sl])
        ig = _sigmoid(z[:, gw:] + bg_ref[0, :, d_ + g * gw:d_ + (g + 1) * gw])
        log_a = cl_ref[0, :, sl] * rg
        a_scr[:, sl] = jnp.exp(log_a)
        u_scr[:, sl] = jnp.sqrt(1.0 - jnp.exp(2.0 * log_a)) * ig * xc[:, sl]

    rowid = lax.broadcasted_iota(jnp.int32, (8, d_), 0)

    def tile_scan(i, h, rev):
        t8 = (tm // 8 - 1 - i) if rev else i
        r0 = pl.multiple_of(t8 * 8, 8)
        a8 = a_scr[pl.ds(r0, 8), :]
        u8 = u_scr[pl.ds(r0, 8), :]
        for s in (1, 2, 4):
            ok = (rowid < 8 - s) if rev else (rowid >= s)
            sh = (8 - s) if rev else s
            a_sh = jnp.where(ok, pltpu.roll(a8, sh, 0), 1.0)
            u_sh = jnp.where(ok, pltpu.roll(u8, sh, 0), 0.0)
            u8 = a8 * u_sh + u8
            a8 = a8 * a_sh
        h8 = u8 + a8 * h
        h_ref[0, 0, pl.ds(r0, 8), :] = h8
        return h8[0:1] if rev else h8[7:8]

    @pl.when(d == 0)
    def _():
        h_scr[...] = lax.fori_loop(0, tm // 8, lambda i, h: tile_scan(i, h, False), h_scr[...], unroll=2)

    @pl.when(d == 1)
    def _():
        h_scr[...] = lax.fori_loop(0, tm // 8, lambda i, h: tile_scan(i, h, True), h_scr[...], unroll=2)

    @pl.when(c == n_c - 1)
    def _():
        hT_ref[0, 0] = h_scr[...]


def rglru(xc, w_gate, b_gate, c_lam, h0, tm):
    b_, l_, d_ = xc.shape
    tm = min(tm, l_)
    n_c = l_ // tm
    tmap = lambda d, c: c + d * (n_c - 1 - 2 * c)
    dirp = lambda shape: pl.BlockSpec((1,) + shape, lambda d, b, c: (d,) + (0,) * len(shape))
    st = pl.BlockSpec((1, 1, 1, d_), lambda d, b, c: (d, b, 0, 0))
    return pl.pallas_call(
        _lru_kernel,
        grid=(2, b_, n_c),
        in_specs=[pl.BlockSpec((1, tm, d_), lambda d, b, c: (b, tmap(d, c), 0)),
                  dirp(w_gate.shape[1:]), dirp((1, 2 * d_)), dirp((1, d_)), st],
        out_specs=[pl.BlockSpec((1, 1, tm, d_), lambda d, b, c: (d, b, tmap(d, c), 0)), st],
        out_shape=[jax.ShapeDtypeStruct((2, b_, l_, d_), F32), jax.ShapeDtypeStruct((2, b_, 1, d_), F32)],
        scratch_shapes=[pltpu.VMEM((tm, d_), F32), pltpu.VMEM((tm, d_), F32), pltpu.VMEM((1, d_), F32)],
        compiler_params=pltpu.CompilerParams(
            dimension_semantics=("arbitrary", "arbitrary", "arbitrary"), vmem_limit_bytes=VMEM_LIMIT),
        name="rglru",
    )(xc, w_gate, b_gate, c_lam, h0)


def _post_kernel(yf_ref, yb_ref, r_ref, k_ref, v_ref, lz_ref, hf_ref, hb_ref, gb_ref, ga_ref, gm_ref, x_ref,
                 gt_ref, sc_ref, sh_ref, wa2_ref, a0_ref, g2_ref, ka_ref, rk_ref, gg_ref, gnb_ref,
                 pa_ref, pb_ref, wo_ref, lg_ref, lb_ref, rwh_ref, rwl_ref, rb_ref,
                 x1_ref, h2_ref, lg_out_ref):
    d_ = x_ref.shape[2]
    y = yf_ref[0] + yb_ref[0]
    mu = _head_sums(y) * (1.0 / HEAD)
    yc = y - mu
    var = _head_sums(yc * yc) * (1.0 / HEAD)
    yn = yc * lax.rsqrt(var + GN_EPS) * gg_ref[...] + gnb_ref[...]
    lz = lz_ref[0]
    a2 = _sigmoid(_dot(lz[:, :LORA_DECAY + LORA_AAA], wa2_ref[...]) + a0_ref[...])
    r, k, v = r_ref[0].astype(F32), k_ref[0].astype(F32), v_ref[0].astype(F32)
    ksum = k * (2.0 + (a2[:, :d_] + a2[:, d_:] - 2.0) * ka_ref[...])
    bonus = _head_sums(r * ksum * rk_ref[...]) * v
    g = _dot(lz[:, LORA_DECAY + LORA_AAA:], g2_ref[...])
    y_a = ((yn + bonus) * g).astype(BF16)
    gb = gb_ref[0]
    gelu = 0.5 * gb * (1.0 + jnp.tanh(0.7978845608028654 * (gb + 0.044715 * gb * gb * gb)))
    y_b = ((hf_ref[0, 0] + hb_ref[0, 0]) * gelu).astype(BF16)
    m = _sigmoid(ga_ref[0]) * _dot(y_a, pa_ref[...]) + _sigmoid(gm_ref[0]) * _dot(y_b, pb_ref[...])
    mix = _dot(m, wo_ref[...])
    x1 = _ln(ALPHA * _ln(x_ref[0]) + gt_ref[0] * mix) * lg_ref[...] + lb_ref[...]
    x1_ref[0] = x1
    h2 = x1 * (1.0 + sc_ref[0]) + sh_ref[0]
    h2_ref[0] = h2
    hh, hl = _split(h2, 2)
    lg_out_ref[0] = (_dot(hh, rwh_ref[...]) + _dot(hl, rwh_ref[...]) + _dot(hh, rwl_ref[...])) + rb_ref[...]


def post_mix(y_f, y_b, mixed, lz, hh, main, x, gt, sc, sh, params, tm):
    b_, l_, d_ = x.shape
    dsec = lambda e: pl.BlockSpec((1, 1, tm, d_), lambda b, i: (e, b, i, 0))
    col = lambda j: pl.BlockSpec((1, tm, d_), lambda b, i: (b, i, j))
    vec = pl.BlockSpec((1, 1, d_), lambda b, i: (b, 0, 0))
    full = lambda a: pl.BlockSpec(a.shape, lambda b, i: (0,) * a.ndim)
    return pl.pallas_call(
        _post_kernel,
        grid=(b_, l_ // tm),
        in_specs=[col(0), col(0), col(0), col(1), col(2),
                  pl.BlockSpec((1, tm, LORA_PAD), lambda b, i: (b, i, 0)),
                  dsec(0), dsec(1), col(4), col(5), col(6), col(0), vec, vec, vec] + [full(p) for p in params],
        out_specs=[col(0), col(0), pl.BlockSpec((1, tm, 128), lambda b, i: (b, i, 0))],
        out_shape=[jax.ShapeDtypeStruct((b_, l_, d_), F32), jax.ShapeDtypeStruct((b_, l_, d_), F32),
                   jax.ShapeDtypeStruct((b_, l_, 128), F32)],
        compiler_params=pltpu.CompilerParams(vmem_limit_bytes=VMEM_LIMIT),
        name="post_mix",
    )(y_f, y_b, mixed, mixed, mixed, lz, hh, hh, main, main, main, x, gt, sc, sh, *params)


def _mm_kernel(x_ref, w_ref, o_ref, *, pa, pb):
    o_ref[...] = _dot(x_ref[...], w_ref[...], _NN, pa, pb)


def pmm(x, w, tm=512, tn=1024, pa=1, pb=1):
    m_, k_ = x.shape
    n_ = w.shape[1]
    tm, tn = min(tm, m_), min(tn, n_)
    assert m_ % tm == 0 and n_ % tn == 0, (x.shape, w.shape, tm, tn)
    return pl.pallas_call(
        functools.partial(_mm_kernel, pa=pa, pb=pb),
        grid=(m_ // tm, n_ // tn),
        in_specs=[pl.BlockSpec((tm, k_), lambda i, j: (i, 0)), pl.BlockSpec((k_, tn), lambda i, j: (0, j))],
        out_specs=pl.BlockSpec((tm, tn), lambda i, j: (i, j)),
        out_shape=jax.ShapeDtypeStruct((m_, n_), F32),
        compiler_params=pltpu.CompilerParams(vmem_limit_bytes=VMEM_LIMIT),
        name="pmm",
    )(x, w)


ROUTE_LANES = 128
R_E, R_RANK, R_W = 0, 2, 4
DMA_UNROLL = 4


def _route_kernel(lg_ref, rec_ref, cnt_ref, carry):
    i = pl.program_id(0)
    tm = lg_ref.shape[0]

    @pl.when(i == 0)
    def _():
        carry[...] = jnp.zeros_like(carry)

    lg = lg_ref[...]
    lane = lax.broadcasted_iota(jnp.int32, lg.shape, 1)
    neg = -jnp.inf
    first = lambda m: jnp.min(jnp.where(m, lane, ROUTE_LANES), axis=1, keepdims=True)
    is_g = lane < N_GROUPS
    gmax = jnp.max(jnp.where(is_g, lg, neg), axis=1, keepdims=True)
    gsel = first(is_g & (lg == gmax))
    p_g = 1.0 / jnp.sum(jnp.where(is_g, jnp.exp(lg - gmax), 0.0), axis=1, keepdims=True)
    in_grp = (lane >= N_GROUPS) & (lane < N_GROUPS + N_EXPERTS) & ((lane - N_GROUPS) // EXPERTS_PER_GROUP == gsel)
    el = jnp.where(in_grp, lg, neg)
    v1 = jnp.max(el, axis=1, keepdims=True)
    i1 = first(in_grp & (el == v1))
    rest = in_grp & (lane != i1)
    el2 = jnp.where(rest, lg, neg)
    v2 = jnp.max(el2, axis=1, keepdims=True)
    i2 = first(rest & (el2 == v2))
    e21 = jnp.exp(v2 - v1)
    w1 = p_g / (1.0 + e21)
    w2 = w1 * e21
    oh1, oh2 = lane == i1, lane == i2
    both = jnp.where(oh1 | oh2, 1.0, 0.0)
    row = lax.broadcasted_iota(jnp.int32, (tm, tm), 0)
    col = lax.broadcasted_iota(jnp.int32, (tm, tm), 1)
    before = jnp.where(col < row, 1.0, 0.0).astype(BF16)
    cnt = _dot(before, both.astype(BF16)) + carry[...]
    rank1 = jnp.sum(jnp.where(oh1, cnt, 0.0), axis=1, keepdims=True)
    rank2 = jnp.sum(jnp.where(oh2, cnt, 0.0), axis=1, keepdims=True)
    carry[...] = carry[...] + jnp.sum(both, axis=0, keepdims=True)
    rec = jnp.zeros_like(lg)
    for k, val in ((R_E, (i1 - N_GROUPS).astype(F32)), (R_E + 1, (i2 - N_GROUPS).astype(F32)),
                   (R_RANK, rank1), (R_RANK + 1, rank2), (R_W, w1), (R_W + 1, w2)):
        rec = jnp.where(lane == k, val, rec)
    rec_ref[...] = rec
    cnt_ref[...] = carry[...]


def route(logits, tm=256):
    t_ = logits.shape[0]
    return pl.pallas_call(
        _route_kernel,
        grid=(t_ // tm,),
        in_specs=[pl.BlockSpec((tm, ROUTE_LANES), lambda i: (i, 0))],
        out_specs=[pl.BlockSpec((tm, ROUTE_LANES), lambda i: (i, 0)), pl.BlockSpec((1, ROUTE_LANES), lambda i: (0, 0))],
        out_shape=[jax.ShapeDtypeStruct((t_, ROUTE_LANES), F32), jax.ShapeDtypeStruct((1, ROUTE_LANES), F32)],
        scratch_shapes=[pltpu.VMEM((1, ROUTE_LANES), F32)],
        compiler_params=pltpu.CompilerParams(dimension_semantics=("arbitrary",)),
        name="route",
    )(logits)


def _dispatch_kernel(dest_ref, zrow_ref, h_ref, xs_ref, zbuf, sem, zsem):
    tm = h_ref.shape[0]
    base = pl.program_id(0) * tm * TOP_K

    @pl.when(pl.program_id(0) == 0)
    def _():
        zbuf[...] = jnp.zeros_like(zbuf)
        n_blocks = xs_ref.shape[0] // MOE_ROWS

        def zero_copy(row0):
            return pltpu.make_async_copy(zbuf, xs_ref.at[pl.ds(pl.multiple_of(row0, MOE_ROWS), MOE_ROWS)], zsem)

        def tail(fn):
            return lax.fori_loop(zrow_ref[N_EXPERTS] // MOE_ROWS, n_blocks, lambda b, c: (fn(zero_copy(b * MOE_ROWS)), c)[1], 0)

        for e in range(N_EXPERTS):
            @pl.when(zrow_ref[e] >= 0)
            def _():
                zero_copy(zrow_ref[e]).start()
        tail(lambda cp: cp.start())
        for e in range(N_EXPERTS):
            @pl.when(zrow_ref[e] >= 0)
            def _():
                zero_copy(zrow_ref[e]).wait()
        tail(lambda cp: cp.wait())

    def copy(r, s):
        return pltpu.make_async_copy(h_ref.at[pl.ds(r, 1)], xs_ref.at[pl.ds(dest_ref[base + r * TOP_K + s], 1)], sem)

    def start(r, carry):
        for s in range(TOP_K):
            copy(r, s).start()
        return carry

    def wait(r, carry):
        for s in range(TOP_K):
            copy(r, s).wait()
        return carry

    lax.fori_loop(0, tm, start, 0, unroll=DMA_UNROLL)
    lax.fori_loop(0, tm, wait, 0, unroll=DMA_UNROLL)


def dispatch(dest, zero_row, h, n_pad, tm=512):
    t_, d_ = h.shape
    grid_spec = pltpu.PrefetchScalarGridSpec(
        num_scalar_prefetch=2,
        grid=(t_ // tm,),
        in_specs=[pl.BlockSpec((tm, d_), lambda i, dest, zr: (i, 0))],
        out_specs=pl.BlockSpec(memory_space=pl.ANY),
        scratch_shapes=[pltpu.VMEM((MOE_ROWS, d_), F32), pltpu.SemaphoreType.DMA(()), pltpu.SemaphoreType.DMA(())],
    )
    return pl.pallas_call(
        _dispatch_kernel,
        grid_spec=grid_spec,
        out_shape=jax.ShapeDtypeStruct((n_pad, d_), F32),
        compiler_params=pltpu.CompilerParams(dimension_semantics=("arbitrary",), has_side_effects=True),
        name="dispatch",
    )(dest, zero_row, h)


def _moe_kernel(be_ref, nv_ref, x_ref, w1_ref, w3_ref, w2_ref, o_ref, w1b, w3b, w2b):
    blk = pl.program_id(0)

    @pl.when((blk == 0) | (be_ref[blk] != be_ref[jnp.maximum(blk - 1, 0)]))
    def _():
        w1b[...] = w1_ref[0].astype(BF16)
        w3b[...] = w3_ref[0].astype(BF16)
        w2b[...] = w2_ref[0].astype(BF16)

    @pl.when(nv_ref[blk] > 0)
    def _():
        x = x_ref[...].astype(BF16)
        h1 = _dot(x, w1b[...])
        h3 = _dot(x, w3b[...])
        hh = h1 * _sigmoid(h1) * h3
        o_ref[...] = _dot(hh, w2b[...])

    @pl.when(nv_ref[blk] == 0)
    def _():
        o_ref[...] = jnp.zeros_like(o_ref)


def moe_experts(xs, blk_e, blk_n, w1, w3, w2):
    n_pad, d_ = xs.shape
    n_blocks = n_pad // MOE_ROWS
    de = w1.shape[2]
    grid_spec = pltpu.PrefetchScalarGridSpec(
        num_scalar_prefetch=2,
        grid=(n_blocks,),
        in_specs=[pl.BlockSpec((MOE_ROWS, d_), lambda i, be, nv: (i, 0)),
                  pl.BlockSpec((1, d_, de), lambda i, be, nv: (be[i], 0, 0)),
                  pl.BlockSpec((1, d_, de), lambda i, be, nv: (be[i], 0, 0)),
                  pl.BlockSpec((1, de, d_), lambda i, be, nv: (be[i], 0, 0))],
        out_specs=pl.BlockSpec((MOE_ROWS, d_), lambda i, be, nv: (i, 0)),
        scratch_shapes=[pltpu.VMEM((d_, de), BF16), pltpu.VMEM((d_, de), BF16), pltpu.VMEM((de, d_), BF16)],
    )
    return pl.pallas_call(
        _moe_kernel,
        grid_spec=grid_spec,
        out_shape=jax.ShapeDtypeStruct((n_pad, d_), F32),
        compiler_params=pltpu.CompilerParams(dimension_semantics=("arbitrary",), vmem_limit_bytes=VMEM_LIMIT),
        name="moe_experts",
    )(blk_e, blk_n, xs, w1, w3, w2)


def _combine_kernel(dest_ref, x_ref, rec_ref, gt_ref, g_ref, b_ref, ys_ref, o_ref, buf, sem):
    tm = x_ref.shape[1]
    base = (pl.program_id(0) * pl.num_programs(1) + pl.program_id(1)) * tm * TOP_K

    def copy(r, s):
        return pltpu.make_async_copy(ys_ref.at[pl.ds(dest_ref[base + r * TOP_K + s], 1)], buf.at[s, pl.ds(r, 1)], sem)

    def start(r, carry):
        for s in range(TOP_K):
            copy(r, s).start()
        return carry

    def wait(r, carry):
        for s in range(TOP_K):
            copy(r, s).wait()
        return carry

    lax.fori_loop(0, tm, start, 0, unroll=DMA_UNROLL)
    lax.fori_loop(0, tm, wait, 0, unroll=DMA_UNROLL)
    rec = rec_ref[...]
    moe = rec[:, R_W:R_W + 1] * buf[0]
    for s in range(1, TOP_K):
        moe = moe + rec[:, R_W + s:R_W + s + 1] * buf[s]
    o_ref[0] = _ln(ALPHA * x_ref[0] + gt_ref[0] * moe) * g_ref[...] + b_ref[...]


def combine_ln(dest, x, rec, gt, g, b, ys, tm=512):
    b_, l_, d_ = x.shape
    n_i = l_ // tm
    grid_spec = pltpu.PrefetchScalarGridSpec(
        num_scalar_prefetch=1,
        grid=(b_, n_i),
        in_specs=[pl.BlockSpec((1, tm, d_), lambda bi, i, dest: (bi, i, 0)),
                  pl.BlockSpec((tm, ROUTE_LANES), lambda bi, i, dest: (bi * n_i + i, 0)),
                  pl.BlockSpec((1, 1, d_), lambda bi, i, dest: (bi, 0, 0)),
                  pl.BlockSpec((1, d_), lambda bi, i, dest: (0, 0)),
                  pl.BlockSpec((1, d_), lambda bi, i, dest: (0, 0)),
                  pl.BlockSpec(memory_space=pl.ANY)],
        out_specs=pl.BlockSpec((1, tm, d_), lambda bi, i, dest: (bi, i, 0)),
        scratch_shapes=[pltpu.VMEM((TOP_K, tm, d_), F32), pltpu.SemaphoreType.DMA(())],
    )
    return pl.pallas_call(
        _combine_kernel,
        grid_spec=grid_spec,
        out_shape=jax.ShapeDtypeStruct(x.shape, F32),
        compiler_params=pltpu.CompilerParams(
            dimension_semantics=("arbitrary", "arbitrary"), vmem_limit_bytes=VMEM_LIMIT),
        name="combine_ln",
    )(dest, x, rec, gt, g.reshape(1, d_), b.reshape(1, d_), ys)


def hier_moe_ln(x1, h2, logits, gt, g, b, w1, w3, w2):
    b_, l_, d_ = x1.shape
    t_ = b_ * l_
    rec, cnt = route(logits)
    counts = cnt[0, N_GROUPS:N_GROUPS + N_EXPERTS].astype(jnp.int32)
    padded = (counts + MOE_ROWS - 1) // MOE_ROWS * MOE_ROWS
    pad_end = jnp.cumsum(padded)
    pad_start = pad_end - padded
    eid = rec[:, R_E:R_E + TOP_K].astype(jnp.int32)
    rank = rec[:, R_RANK:R_RANK + TOP_K].astype(jnp.int32)
    onehot = eid[..., None] == jnp.arange(N_EXPERTS, dtype=jnp.int32)
    dest = (rank + jnp.sum(jnp.where(onehot, pad_start, 0), -1)).reshape(-1)
    n_blocks = -(-(t_ * TOP_K) // MOE_ROWS) + N_EXPERTS
    blk_lo = jnp.arange(n_blocks, dtype=jnp.int32) * MOE_ROWS
    blk_e = jnp.minimum(jnp.sum(blk_lo[:, None] >= pad_end[None, :], -1), N_EXPERTS - 1).astype(jnp.int32)
    blk_n = jnp.clip(jnp.sum(jnp.where(blk_e[:, None] == jnp.arange(N_EXPERTS), pad_start + counts, 0), -1) - blk_lo,
                     0, MOE_ROWS).astype(jnp.int32)
    zero_row = jnp.concatenate([jnp.where(counts > 0, pad_end - MOE_ROWS, -1), pad_end[-1:]]).astype(jnp.int32)
    xs = dispatch(dest, zero_row, h2.reshape(t_, d_), n_blocks * MOE_ROWS)
    ys = moe_experts(xs, blk_e, blk_n, w1, w3, w2)
    return combine_ln(dest, x1, rec, gt, g, b, ys)


def _block_diag_groups(w, grp):
    n = w.shape[0]
    wg = w.reshape(n // grp, grp, HEAD, HEAD)
    eye = jnp.eye(grp, dtype=w.dtype)
    return jnp.einsum("gaij,ab->gaibj", wg, eye).reshape(n // grp, grp * HEAD, grp * HEAD)


def token_scans(h_in, sc, sh, wts, grid_mode, states):
    b_, l_, d_ = h_in.shape
    main, lora = inproj(h_in, sc, sh, wts["w_main"], wts["w_lora"], min(1024, l_))
    mixed, lz, xc = rwkv_mix(main, lora, 3, wts["mu"], wts["mu_l"], wts["k_k"], wts["conv_w"], wts["conv_b"], grid_mode,
                             512 if grid_mode else l_)
    y_f, y_b, s_new = wkv7(mixed, lz, wts["w_lo"], wts["wa0"], wts["k_a"], states[0])
    hh, h_new = rglru(xc, wts["w_gate"], wts["b_gate"], wts["c_lam"], states[1], 1024)
    return (main, mixed, lz, y_f, y_b, hh), (s_new, h_new)


def kernel(x, c, ctx, c_ctx, w_ada, b_ada, w_in, mu_a, w0, w2, a0, a2, g2, k_k, k_a, r_k, gn_g, gn_b, conv_w, conv_b, lru_wa, lru_ba, lru_wx, lru_bx, lru_lam, p_a, p_b, w_o, ln1_g, ln1_b, router_g, router_g_b, router_e, router_e_b, e_w1, e_w3, e_w2, ln2_g, ln2_b):
    b_, l_, d_ = x.shape
    heads = d_ // HEAD
    l = 0
    row = lambda v: v.reshape(1, -1)
    cc = jnp.concatenate([c, c_ctx[None]], 0)
    cc = jnp.pad(jax.nn.silu(cc), ((0, 8 - cc.shape[0]), (0, 0)))
    mod = pmm(cc, w_ada[l], 8, 1024, 2, 2)[:b_ + 1] + b_ada[l]
    mods = jnp.split(mod, 6, axis=-1)
    sh1, sc1, gt1, sh2, sc2, gt2 = [m[:b_, None, :] for m in mods]
    csh1, csc1 = [jnp.broadcast_to(m[b_:, None, :], (b_, 1, d_)) for m in mods[:2]]

    a_slab = 3 * d_ + LORA_DECAY + LORA_AAA + LORA_GATE
    n_lora = a_slab - 3 * d_
    wi, mu = w_in[l], mu_a[l]
    zeros_lo = jnp.zeros((LORA_DECAY, d_), F32)
    w_lo = jnp.stack([jnp.concatenate([jnp.concatenate([w2[l, e], zeros_lo], 1),
                                       jnp.concatenate([zeros_lo, a2[l, e]], 1)], 0) for e in range(2)])
    wts = dict(
        w_main=jnp.stack([wi[:, o:o + d_] for o in (0, d_, 2 * d_, a_slab, a_slab + d_, a_slab + 2 * d_, a_slab + 3 * d_)]
                         ).astype(BF16),
        w_lora=jnp.pad(wi[:, 3 * d_:a_slab], ((0, 0), (0, LORA_PAD - n_lora))).astype(BF16),
        mu=row(mu[:3 * d_]), mu_l=row(jnp.pad(mu[3 * d_:], (0, LORA_PAD - n_lora))), k_k=row(k_k[l]), k_a=row(k_a[l]),
        w_lo=w_lo.astype(BF16),
        wa0=jnp.concatenate([w0[l], a0[l]], -1)[:, None, :],
        conv_w=conv_w[l], conv_b=row(conv_b[l]),
        w_gate=jnp.stack([jnp.concatenate([_block_diag_groups(lru_wa[l, e], 4), _block_diag_groups(lru_wx[l, e], 4)], -1)
                          for e in range(2)]).astype(BF16),
        b_gate=jnp.concatenate([lru_ba[l], lru_bx[l]], -1)[:, None, :],
        c_lam=(-LRU_C * jax.nn.softplus(-lru_lam[l]))[:, None, :])

    s0 = jnp.zeros((2, b_, heads // 2, 2 * HEAD, 2 * HEAD), F32)
    h0 = jnp.zeros((2, b_, 1, d_), F32)
    _, ctx_states = token_scans(ctx, csc1, csh1, wts, False, (s0, h0))
    (main, mixed, lz, y_f, y_b, hh), _ = token_scans(x, sc1, sh1, wts, True, ctx_states)

    zeros_a = jnp.zeros((LORA_DECAY, 2 * d_), F32)
    rw = jnp.pad(jnp.concatenate([router_g[l], router_e[l]], 1), ((0, 0), (0, 128 - N_GROUPS - N_EXPERTS)))
    rw_hi = rw.astype(BF16)
    params = [
        jnp.concatenate([zeros_a, jnp.concatenate([a2[l, 0], a2[l, 1]], 1)], 0).astype(BF16),
        row(jnp.concatenate([a0[l, 0], a0[l, 1]])),
        jnp.pad(g2[l], ((0, LORA_PAD - LORA_DECAY - LORA_AAA - LORA_GATE), (0, 0))).astype(BF16),
        row(k_a[l]), row(r_k[l]), row(gn_g[l]), row(gn_b[l]),
        p_a[l].astype(BF16), p_b[l].astype(BF16), w_o[l].astype(BF16), row(ln1_g[l]), row(ln1_b[l]),
        rw_hi, (rw - rw_hi.astype(F32)).astype(BF16),
        row(jnp.pad(jnp.concatenate([router_g_b[l], router_e_b[l]]), (0, 128 - N_GROUPS - N_EXPERTS)))]
    x1, h2, logits = post_mix(y_f, y_b, mixed, lz, hh, main, x, gt1, sc2, sh2, params, 256)

    return hier_moe_ln(x1, h2, logits.reshape(b_ * l_, -1), gt2, ln2_g[l], ln2_b[l], e_w1[l], e_w3[l], e_w2[l])
```

```python
import functools

import jax
import jax.numpy as jnp
from jax import lax
from jax.experimental import pallas as pl
from jax.experimental.pallas import tpu as pltpu

F32 = jnp.float32
BF16 = jnp.bfloat16

GRID_W = 64
HEAD = 64
LORA_DECAY = 64
LORA_AAA = 64
LORA_GATE = 160
LORA_PAD = 384
GN_EPS = 64e-5
LN_EPS = 1e-5
CONV_W = 5
LRU_C = 8.0
N_GROUPS = 4
EXPERTS_PER_GROUP = 8
N_EXPERTS = N_GROUPS * EXPERTS_PER_GROUP
TOP_K = 2
DEPTH = 1
ALPHA = (2 * DEPTH) ** 0.25

WKV_CHUNK = 64
MOE_ROWS = 512
GATE_GROUP = 4 * HEAD
HALO = 8
VMEM_LIMIT = 56 * 1024 * 1024

_NN = (((1,), (0,)), ((), ()))
_NT = (((1,), (1,)), ((), ()))
_TN = (((0,), (0,)), ((), ()))


def _split(x, n):
    if x.dtype == BF16:
        return [x]
    parts, rest = [], x.astype(F32)
    for i in range(n):
        p = rest.astype(BF16)
        parts.append(p)
        if i + 1 < n:
            rest = rest - p.astype(F32)
    return parts


def _dot(a, b, dims=_NN, pa=1, pb=1):
    ap, bp = _split(a, pa), _split(b, pb)
    order = max(len(ap), len(bp))
    acc = None
    for i, x in enumerate(ap):
        for j, y in enumerate(bp):
            if i + j < order:
                t = lax.dot_general(x, y, dims, preferred_element_type=F32)
                acc = t if acc is None else acc + t
    return acc


def _ln(x):
    mu = jnp.mean(x, -1, keepdims=True)
    xc = x - mu
    var = jnp.mean(xc * xc, -1, keepdims=True)
    return xc * lax.rsqrt(var + LN_EPS)


def _sigmoid(x):
    return 0.5 + 0.5 * jnp.tanh(0.5 * x)


def _softplus(x):
    return jnp.maximum(x, 0.0) + jnp.log(1.0 + jnp.exp(-jnp.abs(x)))


def _head_sums(x):
    pw = 2 * HEAD
    row = lax.broadcasted_iota(jnp.int32, (pw, pw), 0)
    col = lax.broadcasted_iota(jnp.int32, (pw, pw), 1)
    ones = jnp.where((row >= HEAD) == (col >= HEAD), 1.0, 0.0).astype(BF16)
    return jnp.concatenate([_dot(x[:, p:p + pw], ones, _NN, 2, 1) for p in range(0, x.shape[1], pw)], axis=1)


def _inproj_kernel(x_ref, sc_ref, sh_ref, wm_ref, wl_ref, main_ref, lora_ref, h_scr, *, nm):
    j = pl.program_id(2)

    @pl.when(j == 0)
    def _():
        h_scr[...] = (_ln(x_ref[0]) * (1.0 + sc_ref[0]) + sh_ref[0]).astype(BF16)

    @pl.when(j < nm)
    def _():
        main_ref[0] = _dot(h_scr[...], wm_ref[j])

    @pl.when(j == nm)
    def _():
        lora_ref[0] = _dot(h_scr[...], wl_ref[...])


def inproj(x, sc, sh, w_main, w_lora, tm):
    b_, l_, d_ = x.shape
    nm = w_main.shape[0]
    row = pl.BlockSpec((1, tm, d_), lambda b, i, j: (b, i, 0))
    vec = pl.BlockSpec((1, 1, d_), lambda b, i, j: (b, 0, 0))
    return pl.pallas_call(
        functools.partial(_inproj_kernel, nm=nm),
        grid=(b_, l_ // tm, nm + 1),
        in_specs=[row, vec, vec,
                  pl.BlockSpec((nm, d_, d_), lambda b, i, j: (0, 0, 0)),
                  pl.BlockSpec((d_, LORA_PAD), lambda b, i, j: (0, 0))],
        out_specs=[pl.BlockSpec((1, tm, d_), lambda b, i, j: (b, i, jnp.minimum(j, nm - 1))),
                   pl.BlockSpec((1, tm, LORA_PAD), lambda b, i, j: (b, i, 0))],
        out_shape=[jax.ShapeDtypeStruct((b_, l_, nm * d_), F32), jax.ShapeDtypeStruct((b_, l_, LORA_PAD), F32)],
        scratch_shapes=[pltpu.VMEM((tm, d_), BF16)],
        compiler_params=pltpu.CompilerParams(vmem_limit_bytes=VMEM_LIMIT),
        name="inproj",
    )(x, sc, sh, w_main, w_lora)


def _mix_kernel(cur_ref, prev_ref, next_ref, lcur_ref, lprev_ref, lnext_ref, xcur_ref, xprev_ref, xnext_ref,
                mu_ref, mul_ref, kk_ref, cw_ref, cb_ref, mixed_ref, lz_ref, xc_ref, *, grid_mode):
    i = pl.program_id(1)
    n_i = pl.num_programs(1)
    tm = cur_ref.shape[1]
    d_ = kk_ref.shape[1]

    def shifted(cur, prev, nxt):
        rows, ch = cur.shape
        rowi = lax.broadcasted_iota(jnp.int32, (rows, ch), 0)
        lane = lax.broadcasted_iota(jnp.int32, (rows, ch), 1)
        if grid_mode:
            prev = jnp.where(i > 0, prev, 0.0)
            nxt = jnp.where(i < n_i - 1, nxt, 0.0)
            up = jnp.concatenate([prev, cur[:rows - GRID_W]], axis=0)
            down = jnp.concatenate([cur[GRID_W:], nxt], axis=0)
            col = rowi % GRID_W
            left = jnp.where(col == 0, 0.0, pltpu.roll(cur, 1, 0))
            right = jnp.where(col == GRID_W - 1, 0.0, pltpu.roll(cur, rows - 1, 0))
            l4 = lane % 4
            return jnp.where(l4 == 0, left, jnp.where(l4 == 1, right, jnp.where(l4 == 2, up, down)))
        before = jnp.where(rowi == 0, 0.0, pltpu.roll(cur, 1, 0))
        after = jnp.where(rowi == rows - 1, 0.0, pltpu.roll(cur, rows - 1, 0))
        return jnp.where(lane % 2 == 0, before, after)

    def mixed(cur, prev, nxt, mu):
        return cur + mu * (shifted(cur, prev, nxt) - cur)

    for s in range(3):
        sl = slice(s * d_, (s + 1) * d_)
        z = mixed(cur_ref[0, :, sl], prev_ref[0, :, sl], next_ref[0, :, sl], mu_ref[:, sl])
        mixed_ref[0, :, sl] = z.astype(BF16)
        if s == 1:
            kq = z * kk_ref[...]
            kq = kq * lax.rsqrt(_head_sums(kq * kq) + 1e-12)
            mixed_ref[0, :, 3 * d_:4 * d_] = kq.astype(BF16)
    lz = mixed(lcur_ref[0], lprev_ref[0], lnext_ref[0], mul_ref[...])
    wa = lz[:, :LORA_DECAY + LORA_AAA]
    lane = lax.broadcasted_iota(jnp.int32, wa.shape, 1)
    lz_ref[0, :, :LORA_DECAY + LORA_AAA] = jnp.where(lane < LORA_DECAY, jnp.tanh(wa), wa).astype(BF16)
    lz_ref[0, :, LORA_DECAY + LORA_AAA:] = _sigmoid(lz[:, LORA_DECAY + LORA_AAA:]).astype(BF16)

    if grid_mode:
        before = jnp.where(i > 0, xprev_ref[0, GRID_W - HALO:, :], 0.0)
        after = jnp.where(i < n_i - 1, xnext_ref[0, :HALO, :], 0.0)
    else:
        before = after = jnp.zeros((HALO, d_), F32)
    ext = jnp.concatenate([before, xcur_ref[0], after], axis=0)
    xc = cb_ref[...]
    for j in range(CONV_W):
        o = HALO - CONV_W // 2 + j
        xc = xc + cw_ref[j:j + 1, :] * ext[o:o + tm]
    xc_ref[0] = xc


def rwkv_mix(main, lora, xb_col, mu, mu_l, k_k, conv_w, conv_b, grid_mode, tm):
    b_, l_, _ = main.shape
    d_ = k_k.shape[-1]
    if grid_mode:
        assert tm % GRID_W == 0 and l_ % tm == 0
        halo, per = GRID_W, tm // GRID_W
    else:
        assert tm == l_
        halo, per = 8, tm // 8
    n_h = l_ // halo
    cur = lambda w: pl.BlockSpec((1, tm, w), lambda b, i: (b, i, 0))
    prv = lambda w: pl.BlockSpec((1, halo, w), lambda b, i: (b, jnp.maximum(i * per - 1, 0), 0))
    nxt = lambda w: pl.BlockSpec((1, halo, w), lambda b, i: (b, jnp.minimum((i + 1) * per, n_h - 1), 0))
    par = lambda w: pl.BlockSpec((1, w), lambda b, i: (0, 0))
    xcur = pl.BlockSpec((1, tm, d_), lambda b, i: (b, i, xb_col))
    xprv = pl.BlockSpec((1, halo, d_), lambda b, i: (b, jnp.maximum(i * per - 1, 0), xb_col))
    xnxt = pl.BlockSpec((1, halo, d_), lambda b, i: (b, jnp.minimum((i + 1) * per, n_h - 1), xb_col))
    return pl.pallas_call(
        functools.partial(_mix_kernel, grid_mode=grid_mode),
        grid=(b_, l_ // tm),
        in_specs=[cur(3 * d_), prv(3 * d_), nxt(3 * d_), cur(LORA_PAD), prv(LORA_PAD), nxt(LORA_PAD), xcur, xprv, xnxt,
                  par(3 * d_), par(LORA_PAD), par(d_), pl.BlockSpec((CONV_W, d_), lambda b, i: (0, 0)), par(d_)],
        out_specs=[cur(4 * d_), cur(LORA_PAD), cur(d_)],
        out_shape=[jax.ShapeDtypeStruct((b_, l_, 4 * d_), BF16), jax.ShapeDtypeStruct((b_, l_, LORA_PAD), BF16),
                   jax.ShapeDtypeStruct((b_, l_, d_), F32)],
        compiler_params=pltpu.CompilerParams(vmem_limit_bytes=VMEM_LIMIT),
        name="rwkv_mix",
    )(main, main, main, lora, lora, lora, main, main, main, mu, mu_l, k_k, conv_w, conv_b)


def _wkv_kernel(rf_ref, kf_ref, vf_ref, kkf_ref, lzf_ref, rb_ref, kb_ref, vb_ref, kkb_ref, lzb_ref,
                wlo_ref, wa0_ref, ka_ref, s0_ref, yf_ref, yb_ref, sT_ref, s_scr):
    c = pl.program_id(1)
    n_c = pl.num_programs(1)
    cs, d_ = rf_ref.shape[1], rf_ref.shape[2]
    pairs = s_scr.shape[1]
    pw = 2 * HEAD

    @pl.when(c == 0)
    def _():
        s_scr[...] = s0_ref[:, 0]

    row = lax.broadcasted_iota(jnp.int32, (cs, cs), 0)
    col = lax.broadcasted_iota(jnp.int32, (cs, cs), 1)
    row2 = lax.broadcasted_iota(jnp.int32, (2 * cs, pw), 0)
    col2 = lax.broadcasted_iota(jnp.int32, (2 * cs, pw), 1)
    same_head = (row2 >= cs) == (col2 >= HEAD)
    lane = lax.broadcasted_iota(jnp.int32, (cs, pw), 1)
    m0, m1 = lane < HEAD, lane >= HEAD
    mm0, mm1 = col2 < HEAD, col2 >= HEAD
    cat = lambda x, y: jnp.concatenate([x, y], axis=0)
    dot = lambda x, y, dims=_NN: lax.dot_general(x, y, dims, preferred_element_type=F32)
    sel = lambda m, x: jnp.where(m, x, jnp.zeros_like(x))

    streams = []
    for d, (r_ref, k_ref, v_ref, kk_ref, lz_ref, y_ref) in enumerate(
            ((rf_ref, kf_ref, vf_ref, kkf_ref, lzf_ref, yf_ref), (rb_ref, kb_ref, vb_ref, kkb_ref, lzb_ref, yb_ref))):
        tdiff = (row2 % cs - col2 % cs) * (1 - 2 * d)
        amask = (tdiff > 0) | ((row2 >= cs) & (tdiff == 0))
        tri = jnp.where((row - col) * (1 - 2 * d) >= 0, 1.0, 0.0).astype(BF16)
        z = _dot(lz_ref[0, :, :LORA_DECAY + LORA_AAA], wlo_ref[d]) + wa0_ref[d]
        lw = -jnp.exp(-_softplus(-z[:, :d_]) - 0.5)
        a = _sigmoid(z[:, d_:])
        r, kk = r_ref[0].astype(F32), kk_ref[0].astype(F32)
        k = k_ref[0].astype(F32) * (1.0 + (a - 1.0) * ka_ref[...])
        cum = _dot(tri, lw, _NN, 1, 3)
        tot = jnp.sum(lw, axis=0, keepdims=True)
        b = kk * a
        e_neg = jnp.exp(-cum)
        e_end = jnp.exp(tot - cum)
        rt = (r * jnp.exp(cum)).astype(BF16)
        at = (-kk * jnp.exp(cum - lw)).astype(BF16)
        bt = (b * e_neg).astype(BF16)
        kt = (k * e_neg).astype(BF16)
        bd = (b * e_end).astype(BF16)
        kd = (k * e_end).astype(BF16)
        vb = v_ref[0]
        p_end = jnp.exp(tot)
        for p in range(pairs):
            sl = slice(p * pw, (p + 1) * pw)
            streams.append(dict(d=d, p=p, sl=sl, y_ref=y_ref, amask=amask, ar=cat(at[:, sl], rt[:, sl]),
                                bt=bt[:, sl], kt=kt[:, sl], bdkd=cat(bd[:, sl], kd[:, sl]), v=vb[:, sl],
                                p_end=p_end[:, sl]))

    bd = lambda x: cat(sel(m0, x), sel(m1, x))
    lcat = lambda x, y: jnp.concatenate([x, y], axis=1)
    eye_p = jnp.where(lax.broadcasted_iota(jnp.int32, (cs, pw), 0) == lane % cs, 1.0, 0.0)
    for st in streams:
        q = dot(cat(sel(mm0, st["ar"]), sel(mm1, st["ar"])),
                cat(jnp.where(m0, st["bt"], st["kt"]), jnp.where(m0, st["kt"], st["bt"])), _NT)
        q0, q1 = sel(st["amask"], q[:2 * cs]), sel(st["amask"], q[2 * cs:])
        st["nil"] = jnp.where(m0, q0[:cs], q1[:cs])
        q0, q1 = q0.astype(BF16), q1.astype(BF16)
        st["ak"] = jnp.where(m0, q1[:cs], q0[:cs])
        st["rbk"] = lcat(q0[cs:], q1[cs:])
    for st in streams:
        nb = st["nil"].astype(BF16)
        st["t"] = eye_p + st["nil"]
        st["nb"] = dot(nb, bd(nb)).astype(BF16)
    for i in range(max(cs.bit_length() - 3, 0)):
        for st in streams:
            xx = dot(cat(st["t"].astype(BF16), st["nb"]), bd(st["nb"]))
            st["t"] = st["t"] + xx[:cs]
            st["nb"] = xx[cs:].astype(BF16)
    for st in streams:
        st["t"] = (st["t"] + dot(st["t"].astype(BF16), bd(st["nb"]))).astype(BF16)

    zero = jnp.zeros((cs, pw), BF16)
    for st in streams:
        st["s"] = s_scr[st["d"], st["p"]]
        st["as"] = dot(lcat(st["ar"], cat(st["ak"], zero)),
                       cat(st["s"].astype(BF16), cat(sel(m1, st["v"]), sel(m0, st["v"]))))
    for st in streams:
        st["u"] = dot(st["t"], bd(st["as"][:cs].astype(BF16))).astype(BF16)
    for st in streams:
        u_p, v_p = st["u"], st["v"]
        y = st["as"][cs:] + dot(st["rbk"], cat(cat(sel(m0, u_p), sel(m0, v_p)), cat(sel(m1, v_p), sel(m1, u_p))))
        st["y_ref"][0, :, st["sl"]] = y
        p_col = jnp.broadcast_to(st["p_end"], (pw, pw)).T
        upd = dot(st["bdkd"], cat(u_p, v_p), _TN)
        s_scr[st["d"], st["p"]] = p_col * st["s"] + sel(same_head, upd)

    @pl.when(c == n_c - 1)
    def _():
        sT_ref[:, 0] = s_scr[...]


def _pair_states(s):
    st = jnp.swapaxes(s, -1, -2)
    st = st.reshape(s.shape[:-3] + (s.shape[-3] // 2, 2, HEAD, HEAD))
    eye = jnp.eye(2, dtype=s.dtype)
    out = jnp.einsum("...pakv,ab->...pakbv", st, eye)
    return out.reshape(s.shape[:-3] + (s.shape[-3] // 2, 2 * HEAD, 2 * HEAD))


def wkv7(mixed, lz, w_lo, wa0, k_a, s0p):
    b_, l_, d4 = mixed.shape
    d_ = d4 // 4
    pairs, pw = d_ // (2 * HEAD), 2 * HEAD
    cs = WKV_CHUNK
    assert cs == HEAD and l_ % cs == 0
    n_c = l_ // cs

    fwd = lambda w, j: pl.BlockSpec((1, cs, w), lambda b, c: (b, c, j))
    bwd = lambda w, j: pl.BlockSpec((1, cs, w), lambda b, c: (b, n_c - 1 - c, j))
    full = lambda a: pl.BlockSpec(a.shape, lambda b, c: (0,) * a.ndim)
    state = pl.BlockSpec((2, 1, pairs, pw, pw), lambda b, c: (0, b, 0, 0, 0))
    return pl.pallas_call(
        _wkv_kernel,
        grid=(b_, n_c),
        in_specs=[fwd(d_, 0), fwd(d_, 1), fwd(d_, 2), fwd(d_, 3), fwd(LORA_PAD, 0),
                  bwd(d_, 0), bwd(d_, 1), bwd(d_, 2), bwd(d_, 3), bwd(LORA_PAD, 0),
                  full(w_lo), full(wa0), full(k_a), state],
        out_specs=[fwd(d_, 0), bwd(d_, 0), state],
        out_shape=[jax.ShapeDtypeStruct((b_, l_, d_), F32), jax.ShapeDtypeStruct((b_, l_, d_), F32),
                   jax.ShapeDtypeStruct((2, b_, pairs, pw, pw), F32)],
        scratch_shapes=[pltpu.VMEM((2, pairs, pw, pw), F32)],
        compiler_params=pltpu.CompilerParams(
            dimension_semantics=("arbitrary", "arbitrary"), vmem_limit_bytes=VMEM_LIMIT),
        name="wkv7",
    )(mixed, mixed, mixed, mixed, lz, mixed, mixed, mixed, mixed, lz, w_lo, wa0, k_a, s0p)


def _lru_kernel(xc_ref, wg_ref, bg_ref, cl_ref, h0_ref, h_ref, hT_ref, a_scr, u_scr, h_scr):
    d = pl.program_id(0)
    c = pl.program_id(2)
    n_c = pl.num_programs(2)
    tm, d_ = a_scr.shape
    gw = GATE_GROUP

    @pl.when(c == 0)
    def _():
        h_scr[...] = h0_ref[0, 0]

    xc = xc_ref[0]
    for g in range(d_ // gw):
        sl = slice(g * gw, (g + 1) * gw)
        z = _dot(xc[:, sl], wg_ref[0, g])
        rg = _sigmoid(z[:, :gw] + bg_ref[0, :, sl])
        ig = _sigmoid(z[:, gw:] + bg_ref[0, :, d_ + g * gw:d_ + (g + 1) * gw])
        log_a = cl_ref[0, :, sl] * rg
        a_scr[:, sl] = jnp.exp(log_a)
        u_scr[:, sl] = jnp.sqrt(1.0 - jnp.exp(2.0 * log_a)) * ig * xc[:, sl]

    rowid = lax.broadcasted_iota(jnp.int32, (8, d_), 0)

    def tile_scan(i, h, rev):
        t8 = (tm // 8 - 1 - i) if rev else i
        r0 = pl.multiple_of(t8 * 8, 8)
        a8 = a_scr[pl.ds(r0, 8), :]
        u8 = u_scr[pl.ds(r0, 8), :]
        for s in (1, 2, 4):
            ok = (rowid < 8 - s) if rev else (rowid >= s)
            sh = (8 - s) if rev else s
            a_sh = jnp.where(ok, pltpu.roll(a8, sh, 0), 1.0)
            u_sh = jnp.where(ok, pltpu.roll(u8, sh, 0), 0.0)
            u8 = a8 * u_sh + u8
            a8 = a8 * a_sh
        h8 = u8 + a8 * h
        h_ref[0, 0, pl.ds(r0, 8), :] = h8
        return h8[0:1] if rev else h8[7:8]

    @pl.when(d == 0)
    def _():
        h_scr[...] = lax.fori_loop(0, tm // 8, lambda i, h: tile_scan(i, h, False), h_scr[...], unroll=2)

    @pl.when(d == 1)
    def _():
        h_scr[...] = lax.fori_loop(0, tm // 8, lambda i, h: tile_scan(i, h, True), h_scr[...], unroll=2)

    @pl.when(c == n_c - 1)
    def _():
        hT_ref[0, 0] = h_scr[...]


def rglru(xc, w_gate, b_gate, c_lam, h0, tm):
    b_, l_, d_ = xc.shape
    tm = min(tm, l_)
    n_c = l_ // tm
    tmap = lambda d, c: c + d * (n_c - 1 - 2 * c)
    dirp = lambda shape: pl.BlockSpec((1,) + shape, lambda d, b, c: (d,) + (0,) * len(shape))
    st = pl.BlockSpec((1, 1, 1, d_), lambda d, b, c: (d, b, 0, 0))
    return pl.pallas_call(
        _lru_kernel,
        grid=(2, b_, n_c),
        in_specs=[pl.BlockSpec((1, tm, d_), lambda d, b, c: (b, tmap(d, c), 0)),
                  dirp(w_gate.shape[1:]), dirp((1, 2 * d_)), dirp((1, d_)), st],
        out_specs=[pl.BlockSpec((1, 1, tm, d_), lambda d, b, c: (d, b, tmap(d, c), 0)), st],
        out_shape=[jax.ShapeDtypeStruct((2, b_, l_, d_), F32), jax.ShapeDtypeStruct((2, b_, 1, d_), F32)],
        scratch_shapes=[pltpu.VMEM((tm, d_), F32), pltpu.VMEM((tm, d_), F32), pltpu.VMEM((1, d_), F32)],
        compiler_params=pltpu.CompilerParams(
            dimension_semantics=("arbitrary", "arbitrary", "arbitrary"), vmem_limit_bytes=VMEM_LIMIT),
        name="rglru",
    )(xc, w_gate, b_gate, c_lam, h0)


def _post_kernel(yf_ref, yb_ref, r_ref, k_ref, v_ref, lz_ref, hf_ref, hb_ref, gb_ref, ga_ref, gm_ref, x_ref,
                 gt_ref, sc_ref, sh_ref, wa2_ref, a0_ref, g2_ref, ka_ref, rk_ref, gg_ref, gnb_ref,
                 pa_ref, pb_ref, wo_ref, lg_ref, lb_ref, rwh_ref, rwl_ref, rb_ref,
                 x1_ref, h2_ref, lg_out_ref):
    d_ = x_ref.shape[2]
    y = yf_ref[0] + yb_ref[0]
    mu = _head_sums(y) * (1.0 / HEAD)
    yc = y - mu
    var = _head_sums(yc * yc) * (1.0 / HEAD)
    yn = yc * lax.rsqrt(var + GN_EPS) * gg_ref[...] + gnb_ref[...]
    lz = lz_ref[0]
    a2 = _sigmoid(_dot(lz[:, :LORA_DECAY + LORA_AAA], wa2_ref[...]) + a0_ref[...])
    r, k, v = r_ref[0].astype(F32), k_ref[0].astype(F32), v_ref[0].astype(F32)
    ksum = k * (2.0 + (a2[:, :d_] + a2[:, d_:] - 2.0) * ka_ref[...])
    bonus = _head_sums(r * ksum * rk_ref[...]) * v
    g = _dot(lz[:, LORA_DECAY + LORA_AAA:], g2_ref[...])
    y_a = ((yn + bonus) * g).astype(BF16)
    gb = gb_ref[0]
    gelu = 0.5 * gb * (1.0 + jnp.tanh(0.7978845608028654 * (gb + 0.044715 * gb * gb * gb)))
    y_b = ((hf_ref[0, 0] + hb_ref[0, 0]) * gelu).astype(BF16)
    m = _sigmoid(ga_ref[0]) * _dot(y_a, pa_ref[...]) + _sigmoid(gm_ref[0]) * _dot(y_b, pb_ref[...])
    mix = _dot(m, wo_ref[...])
    x1 = _ln(ALPHA * _ln(x_ref[0]) + gt_ref[0] * mix) * lg_ref[...] + lb_ref[...]
    x1_ref[0] = x1
    h2 = x1 * (1.0 + sc_ref[0]) + sh_ref[0]
    h2_ref[0] = h2
    hh, hl = _split(h2, 2)
    lg_out_ref[0] = (_dot(hh, rwh_ref[...]) + _dot(hl, rwh_ref[...]) + _dot(hh, rwl_ref[...])) + rb_ref[...]


def post_mix(y_f, y_b, mixed, lz, hh, main, x, gt, sc, sh, params, tm):
    b_, l_, d_ = x.shape
    dsec = lambda e: pl.BlockSpec((1, 1, tm, d_), lambda b, i: (e, b, i, 0))
    col = lambda j: pl.BlockSpec((1, tm, d_), lambda b, i: (b, i, j))
    vec = pl.BlockSpec((1, 1, d_), lambda b, i: (b, 0, 0))
    full = lambda a: pl.BlockSpec(a.shape, lambda b, i: (0,) * a.ndim)
    return pl.pallas_call(
        _post_kernel,
        grid=(b_, l_ // tm),
        in_specs=[col(0), col(0), col(0), col(1), col(2),
                  pl.BlockSpec((1, tm, LORA_PAD), lambda b, i: (b, i, 0)),
                  dsec(0), dsec(1), col(4), col(5), col(6), col(0), vec, vec, vec] + [full(p) for p in params],
        out_specs=[col(0), col(0), pl.BlockSpec((1, tm, 128), lambda b, i: (b, i, 0))],
        out_shape=[jax.ShapeDtypeStruct((b_, l_, d_), F32), jax.ShapeDtypeStruct((b_, l_, d_), F32),
                   jax.ShapeDtypeStruct((b_, l_, 128), F32)],
        compiler_params=pltpu.CompilerParams(vmem_limit_bytes=VMEM_LIMIT),
        name="post_mix",
    )(y_f, y_b, mixed, mixed, mixed, lz, hh, hh, main, main, main, x, gt, sc, sh, *params)


def _mm_kernel(x_ref, w_ref, o_ref, *, pa, pb):
    o_ref[...] = _dot(x_ref[...], w_ref[...], _NN, pa, pb)


def pmm(x, w, tm=512, tn=1024, pa=1, pb=1):
    m_, k_ = x.shape
    n_ = w.shape[1]
    tm, tn = min(tm, m_), min(tn, n_)
    assert m_ % tm == 0 and n_ % tn == 0, (x.shape, w.shape, tm, tn)
    return pl.pallas_call(
        functools.partial(_mm_kernel, pa=pa, pb=pb),
        grid=(m_ // tm, n_ // tn),
        in_specs=[pl.BlockSpec((tm, k_), lambda i, j: (i, 0)), pl.BlockSpec((k_, tn), lambda i, j: (0, j))],
        out_specs=pl.BlockSpec((tm, tn), lambda i, j: (i, j)),
        out_shape=jax.ShapeDtypeStruct((m_, n_), F32),
        compiler_params=pltpu.CompilerParams(vmem_limit_bytes=VMEM_LIMIT),
        name="pmm",
    )(x, w)


ROUTE_LANES = 128
R_E, R_RANK, R_W = 0, 2, 4
DMA_UNROLL = 4


def _route_kernel(lg_ref, rec_ref, cnt_ref, carry):
    i = pl.program_id(0)
    tm = lg_ref.shape[0]

    @pl.when(i == 0)
    def _():
        carry[...] = jnp.zeros_like(carry)

    lg = lg_ref[...]
    lane = lax.broadcasted_iota(jnp.int32, lg.shape, 1)
    neg = -jnp.inf
    first = lambda m: jnp.min(jnp.where(m, lane, ROUTE_LANES), axis=1, keepdims=True)
    is_g = lane < N_GROUPS
    gmax = jnp.max(jnp.where(is_g, lg, neg), axis=1, keepdims=True)
    gsel = first(is_g & (lg == gmax))
    p_g = 1.0 / jnp.sum(jnp.where(is_g, jnp.exp(lg - gmax), 0.0), axis=1, keepdims=True)
    in_grp = (lane >= N_GROUPS) & (lane < N_GROUPS + N_EXPERTS) & ((lane - N_GROUPS) // EXPERTS_PER_GROUP == gsel)
    el = jnp.where(in_grp, lg, neg)
    v1 = jnp.max(el, axis=1, keepdims=True)
    i1 = first(in_grp & (el == v1))
    rest = in_grp & (lane != i1)
    el2 = jnp.where(rest, lg, neg)
    v2 = jnp.max(el2, axis=1, keepdims=True)
    i2 = first(rest & (el2 == v2))
    e21 = jnp.exp(v2 - v1)
    w1 = p_g / (1.0 + e21)
    w2 = w1 * e21
    oh1, oh2 = lane == i1, lane == i2
    both = jnp.where(oh1 | oh2, 1.0, 0.0)
    row = lax.broadcasted_iota(jnp.int32, (tm, tm), 0)
    col = lax.broadcasted_iota(jnp.int32, (tm, tm), 1)
    before = jnp.where(col < row, 1.0, 0.0).astype(BF16)
    cnt = _dot(before, both.astype(BF16)) + carry[...]
    rank1 = jnp.sum(jnp.where(oh1, cnt, 0.0), axis=1, keepdims=True)
    rank2 = jnp.sum(jnp.where(oh2, cnt, 0.0), axis=1, keepdims=True)
    carry[...] = carry[...] + jnp.sum(both, axis=0, keepdims=True)
    rec = jnp.zeros_like(lg)
    for k, val in ((R_E, (i1 - N_GROUPS).astype(F32)), (R_E + 1, (i2 - N_GROUPS).astype(F32)),
                   (R_RANK, rank1), (R_RANK + 1, rank2), (R_W, w1), (R_W + 1, w2)):
        rec = jnp.where(lane == k, val, rec)
    rec_ref[...] = rec
    cnt_ref[...] = carry[...]


def route(logits, tm=256):
    t_ = logits.shape[0]
    return pl.pallas_call(
        _route_kernel,
        grid=(t_ // tm,),
        in_specs=[pl.BlockSpec((tm, ROUTE_LANES), lambda i: (i, 0))],
        out_specs=[pl.BlockSpec((tm, ROUTE_LANES), lambda i: (i, 0)), pl.BlockSpec((1, ROUTE_LANES), lambda i: (0, 0))],
        out_shape=[jax.ShapeDtypeStruct((t_, ROUTE_LANES), F32), jax.ShapeDtypeStruct((1, ROUTE_LANES), F32)],
        scratch_shapes=[pltpu.VMEM((1, ROUTE_LANES), F32)],
        compiler_params=pltpu.CompilerParams(dimension_semantics=("arbitrary",)),
        name="route",
    )(logits)


def _dispatch_kernel(dest_ref, zrow_ref, h_ref, xs_ref, zbuf, sem, zsem):
    tm = h_ref.shape[0]
    base = pl.program_id(0) * tm * TOP_K

    @pl.when(pl.program_id(0) == 0)
    def _():
        zbuf[...] = jnp.zeros_like(zbuf)
        n_blocks = xs_ref.shape[0] // MOE_ROWS

        def zero_copy(row0):
            return pltpu.make_async_copy(zbuf, xs_ref.at[pl.ds(pl.multiple_of(row0, MOE_ROWS), MOE_ROWS)], zsem)

        def tail(fn):
            return lax.fori_loop(zrow_ref[N_EXPERTS] // MOE_ROWS, n_blocks, lambda b, c: (fn(zero_copy(b * MOE_ROWS)), c)[1], 0)

        for e in range(N_EXPERTS):
            @pl.when(zrow_ref[e] >= 0)
            def _():
                zero_copy(zrow_ref[e]).start()
        tail(lambda cp: cp.start())
        for e in range(N_EXPERTS):
            @pl.when(zrow_ref[e] >= 0)
            def _():
                zero_copy(zrow_ref[e]).wait()
        tail(lambda cp: cp.wait())

    def copy(r, s):
        return pltpu.make_async_copy(h_ref.at[pl.ds(r, 1)], xs_ref.at[pl.ds(dest_ref[base + r * TOP_K + s], 1)], sem)

    def start(r, carry):
        for s in range(TOP_K):
            copy(r, s).start()
        return carry

    def wait(r, carry):
        for s in range(TOP_K):
            copy(r, s).wait()
        return carry

    lax.fori_loop(0, tm, start, 0, unroll=DMA_UNROLL)
    lax.fori_loop(0, tm, wait, 0, unroll=DMA_UNROLL)


def dispatch(dest, zero_row, h, n_pad, tm=1024):
    t_, d_ = h.shape
    tm = min(tm, t_)
    grid_spec = pltpu.PrefetchScalarGridSpec(
        num_scalar_prefetch=2,
        grid=(t_ // tm,),
        in_specs=[pl.BlockSpec((tm, d_), lambda i, dest, zr: (i, 0))],
        out_specs=pl.BlockSpec(memory_space=pl.ANY),
        scratch_shapes=[pltpu.VMEM((MOE_ROWS, d_), F32), pltpu.SemaphoreType.DMA(()), pltpu.SemaphoreType.DMA(())],
    )
    return pl.pallas_call(
        _dispatch_kernel,
        grid_spec=grid_spec,
        out_shape=jax.ShapeDtypeStruct((n_pad, d_), F32),
        compiler_params=pltpu.CompilerParams(dimension_semantics=("arbitrary",), has_side_effects=True),
        name="dispatch",
    )(dest, zero_row, h)


def _moe_kernel(be_ref, nv_ref, x_ref, w1_ref, w3_ref, w2_ref, o_ref, w1b, w3b, w2b):
    blk = pl.program_id(0)

    @pl.when((blk == 0) | (be_ref[blk] != be_ref[jnp.maximum(blk - 1, 0)]))
    def _():
        w1b[...] = w1_ref[0].astype(BF16)
        w3b[...] = w3_ref[0].astype(BF16)
        w2b[...] = w2_ref[0].astype(BF16)

    @pl.when(nv_ref[blk] > 0)
    def _():
        x = x_ref[...].astype(BF16)
        h1 = _dot(x, w1b[...])
        h3 = _dot(x, w3b[...])
        hh = h1 * _sigmoid(h1) * h3
        o_ref[...] = _dot(hh, w2b[...])

    @pl.when(nv_ref[blk] == 0)
    def _():
        o_ref[...] = jnp.zeros_like(o_ref)


def moe_experts(xs, blk_e, blk_n, w1, w3, w2):
    n_pad, d_ = xs.shape
    n_blocks = n_pad // MOE_ROWS
    de = w1.shape[2]
    grid_spec = pltpu.PrefetchScalarGridSpec(
        num_scalar_prefetch=2,
        grid=(n_blocks,),
        in_specs=[pl.BlockSpec((MOE_ROWS, d_), lambda i, be, nv: (i, 0)),
                  pl.BlockSpec((1, d_, de), lambda i, be, nv: (be[i], 0, 0)),
                  pl.BlockSpec((1, d_, de), lambda i, be, nv: (be[i], 0, 0)),
                  pl.BlockSpec((1, de, d_), lambda i, be, nv: (be[i], 0, 0))],
        out_specs=pl.BlockSpec((MOE_ROWS, d_), lambda i, be, nv: (i, 0)),
        scratch_shapes=[pltpu.VMEM((d_, de), BF16), pltpu.VMEM((d_, de), BF16), pltpu.VMEM((de, d_), BF16)],
    )
    return pl.pallas_call(
        _moe_kernel,
        grid_spec=grid_spec,
        out_shape=jax.ShapeDtypeStruct((n_pad, d_), F32),
        compiler_params=pltpu.CompilerParams(dimension_semantics=("arbitrary",), vmem_limit_bytes=VMEM_LIMIT),
        name="moe_experts",
    )(blk_e, blk_n, xs, w1, w3, w2)


def _combine_kernel(dest_ref, x_ref, rec_ref, gt_ref, g_ref, b_ref, ys_ref, o_ref, buf, sem):
    tm = x_ref.shape[1]
    base = (pl.program_id(0) * pl.num_programs(1) + pl.program_id(1)) * tm * TOP_K

    def copy(r, s):
        return pltpu.make_async_copy(ys_ref.at[pl.ds(dest_ref[base + r * TOP_K + s], 1)], buf.at[s, pl.ds(r, 1)], sem)

    def start(r, carry):
        for s in range(TOP_K):
            copy(r, s).start()
        return carry

    def wait(r, carry):
        for s in range(TOP_K):
            copy(r, s).wait()
        return carry

    lax.fori_loop(0, tm, start, 0, unroll=DMA_UNROLL)
    lax.fori_loop(0, tm, wait, 0, unroll=DMA_UNROLL)
    rec = rec_ref[...]
    moe = rec[:, R_W:R_W + 1] * buf[0]
    for s in range(1, TOP_K):
        moe = moe + rec[:, R_W + s:R_W + s + 1] * buf[s]
    o_ref[0] = _ln(ALPHA * x_ref[0] + gt_ref[0] * moe) * g_ref[...] + b_ref[...]


def combine_ln(dest, x, rec, gt, g, b, ys, tm=1024):
    b_, l_, d_ = x.shape
    tm = min(tm, l_)
    n_i = l_ // tm
    grid_spec = pltpu.PrefetchScalarGridSpec(
        num_scalar_prefetch=1,
        grid=(b_, n_i),
        in_specs=[pl.BlockSpec((1, tm, d_), lambda bi, i, dest: (bi, i, 0)),
                  pl.BlockSpec((tm, ROUTE_LANES), lambda bi, i, dest: (bi * n_i + i, 0)),
                  pl.BlockSpec((1, 1, d_), lambda bi, i, dest: (bi, 0, 0)),
                  pl.BlockSpec((1, d_), lambda bi, i, dest: (0, 0)),
                  pl.BlockSpec((1, d_), lambda bi, i, dest: (0, 0)),
                  pl.BlockSpec(memory_space=pl.ANY)],
        out_specs=pl.BlockSpec((1, tm, d_), lambda bi, i, dest: (bi, i, 0)),
        scratch_shapes=[pltpu.VMEM((TOP_K, tm, d_), F32), pltpu.SemaphoreType.DMA(())],
    )
    return pl.pallas_call(
        _combine_kernel,
        grid_spec=grid_spec,
        out_shape=jax.ShapeDtypeStruct(x.shape, F32),
        compiler_params=pltpu.CompilerParams(
            dimension_semantics=("arbitrary", "arbitrary"), vmem_limit_bytes=VMEM_LIMIT),
        name="combine_ln",
    )(dest, x, rec, gt, g.reshape(1, d_), b.reshape(1, d_), ys)


def hier_moe_ln(x1, h2, logits, gt, g, b, w1, w3, w2):
    b_, l_, d_ = x1.shape
    t_ = b_ * l_
    rec, cnt = route(logits)
    counts = cnt[0, N_GROUPS:N_GROUPS + N_EXPERTS].astype(jnp.int32)
    padded = (counts + MOE_ROWS - 1) // MOE_ROWS * MOE_ROWS
    pad_end = jnp.cumsum(padded)
    pad_start = pad_end - padded
    eid = rec[:, R_E:R_E + TOP_K].astype(jnp.int32)
    rank = rec[:, R_RANK:R_RANK + TOP_K].astype(jnp.int32)
    onehot = eid[..., None] == jnp.arange(N_EXPERTS, dtype=jnp.int32)
    dest = (rank + jnp.sum(jnp.where(onehot, pad_start, 0), -1)).reshape(-1)
    n_blocks = -(-(t_ * TOP_K) // MOE_ROWS) + N_EXPERTS
    blk_lo = jnp.arange(n_blocks, dtype=jnp.int32) * MOE_ROWS
    blk_e = jnp.minimum(jnp.sum(blk_lo[:, None] >= pad_end[None, :], -1), N_EXPERTS - 1).astype(jnp.int32)
    blk_n = jnp.clip(jnp.sum(jnp.where(blk_e[:, None] == jnp.arange(N_EXPERTS), pad_start + counts, 0), -1) - blk_lo,
                     0, MOE_ROWS).astype(jnp.int32)
    zero_row = jnp.concatenate([jnp.where(counts > 0, pad_end - MOE_ROWS, -1), pad_end[-1:]]).astype(jnp.int32)
    xs = dispatch(dest, zero_row, h2.reshape(t_, d_), n_blocks * MOE_ROWS)
    ys = moe_experts(xs, blk_e, blk_n, w1, w3, w2)
    return combine_ln(dest, x1, rec, gt, g, b, ys)


def _block_diag_groups(w, grp):
    n = w.shape[0]
    wg = w.reshape(n // grp, grp, HEAD, HEAD)
    eye = jnp.eye(grp, dtype=w.dtype)
    return jnp.einsum("gaij,ab->gaibj", wg, eye).reshape(n // grp, grp * HEAD, grp * HEAD)


def token_scans(h_in, sc, sh, wts, grid_mode, states):
    b_, l_, d_ = h_in.shape
    main, lora = inproj(h_in, sc, sh, wts["w_main"], wts["w_lora"], min(1024, l_))
    mixed, lz, xc = rwkv_mix(main, lora, 3, wts["mu"], wts["mu_l"], wts["k_k"], wts["conv_w"], wts["conv_b"], grid_mode,
                             512 if grid_mode else l_)
    y_f, y_b, s_new = wkv7(mixed, lz, wts["w_lo"], wts["wa0"], wts["k_a"], states[0])
    hh, h_new = rglru(xc, wts["w_gate"], wts["b_gate"], wts["c_lam"], states[1], 1024)
    return (main, mixed, lz, y_f, y_b, hh), (s_new, h_new)


def kernel(x, c, ctx, c_ctx, w_ada, b_ada, w_in, mu_a, w0, w2, a0, a2, g2, k_k, k_a, r_k, gn_g, gn_b, conv_w, conv_b, lru_wa, lru_ba, lru_wx, lru_bx, lru_lam, p_a, p_b, w_o, ln1_g, ln1_b, router_g, router_g_b, router_e, router_e_b, e_w1, e_w3, e_w2, ln2_g, ln2_b):
    b_, l_, d_ = x.shape
    heads = d_ // HEAD
    l = 0
    row = lambda v: v.reshape(1, -1)
    cc = jnp.concatenate([c, c_ctx[None]], 0)
    cc = jnp.pad(jax.nn.silu(cc), ((0, 8 - cc.shape[0]), (0, 0)))
    mod = pmm(cc, w_ada[l], 8, 1024, 2, 2)[:b_ + 1] + b_ada[l]
    mods = jnp.split(mod, 6, axis=-1)
    sh1, sc1, gt1, sh2, sc2, gt2 = [m[:b_, None, :] for m in mods]
    csh1, csc1 = [jnp.broadcast_to(m[b_:, None, :], (b_, 1, d_)) for m in mods[:2]]

    a_slab = 3 * d_ + LORA_DECAY + LORA_AAA + LORA_GATE
    n_lora = a_slab - 3 * d_
    wi, mu = w_in[l], mu_a[l]
    zeros_lo = jnp.zeros((LORA_DECAY, d_), F32)
    w_lo = jnp.stack([jnp.concatenate([jnp.concatenate([w2[l, e], zeros_lo], 1),
                                       jnp.concatenate([zeros_lo, a2[l, e]], 1)], 0) for e in range(2)])
    wts = dict(
        w_main=jnp.stack([wi[:, o:o + d_] for o in (0, d_, 2 * d_, a_slab, a_slab + d_, a_slab + 2 * d_, a_slab + 3 * d_)]
                         ).astype(BF16),
        w_lora=jnp.pad(wi[:, 3 * d_:a_slab], ((0, 0), (0, LORA_PAD - n_lora))).astype(BF16),
        mu=row(mu[:3 * d_]), mu_l=row(jnp.pad(mu[3 * d_:], (0, LORA_PAD - n_lora))), k_k=row(k_k[l]), k_a=row(k_a[l]),
        w_lo=w_lo.astype(BF16),
        wa0=jnp.concatenate([w0[l], a0[l]], -1)[:, None, :],
        conv_w=conv_w[l], conv_b=row(conv_b[l]),
        w_gate=jnp.stack([jnp.concatenate([_block_diag_groups(lru_wa[l, e], 4), _block_diag_groups(lru_wx[l, e], 4)], -1)
                          for e in range(2)]).astype(BF16),
        b_gate=jnp.concatenate([lru_ba[l], lru_bx[l]], -1)[:, None, :],
        c_lam=(-LRU_C * jax.nn.softplus(-lru_lam[l]))[:, None, :])

    s0 = jnp.zeros((2, b_, heads // 2, 2 * HEAD, 2 * HEAD), F32)
    h0 = jnp.zeros((2, b_, 1, d_), F32)
    _, ctx_states = token_scans(ctx, csc1, csh1, wts, False, (s0, h0))
    (main, mixed, lz, y_f, y_b, hh), _ = token_scans(x, sc1, sh1, wts, True, ctx_states)

    zeros_a = jnp.zeros((LORA_DECAY, 2 * d_), F32)
    rw = jnp.pad(jnp.concatenate([router_g[l], router_e[l]], 1), ((0, 0), (0, 128 - N_GROUPS - N_EXPERTS)))
    rw_hi = rw.astype(BF16)
    params = [
        jnp.concatenate([zeros_a, jnp.concatenate([a2[l, 0], a2[l, 1]], 1)], 0).astype(BF16),
        row(jnp.concatenate([a0[l, 0], a0[l, 1]])),
        jnp.pad(g2[l], ((0, LORA_PAD - LORA_DECAY - LORA_AAA - LORA_GATE), (0, 0))).astype(BF16),
        row(k_a[l]), row(r_k[l]), row(gn_g[l]), row(gn_b[l]),
        p_a[l].astype(BF16), p_b[l].astype(BF16), w_o[l].astype(BF16), row(ln1_g[l]), row(ln1_b[l]),
        rw_hi, (rw - rw_hi.astype(F32)).astype(BF16),
        row(jnp.pad(jnp.concatenate([router_g_b[l], router_e_b[l]]), (0, 128 - N_GROUPS - N_EXPERTS)))]
    x1, h2, logits = post_mix(y_f, y_b, mixed, lz, hh, main, x, gt1, sc2, sh2, params, 256)

    return hier_moe_ln(x1, h2, logits.reshape(b_ * l_, -1), gt2, ln2_g[l], ln2_b[l], e_w1[l], e_w3[l], e_w2[l])
```

```python
import functools

import jax
import jax.numpy as jnp
from jax import lax
from jax.experimental import pallas as pl
from jax.experimental.pallas import tpu as pltpu

F32 = jnp.float32
BF16 = jnp.bfloat16

GRID_W = 64
HEAD = 64
LORA_DECAY = 64
LORA_AAA = 64
LORA_GATE = 160
LORA_PAD = 384
GN_EPS = 64e-5
LN_EPS = 1e-5
CONV_W = 5
LRU_C = 8.0
N_GROUPS = 4
EXPERTS_PER_GROUP = 8
N_EXPERTS = N_GROUPS * EXPERTS_PER_GROUP
TOP_K = 2
DEPTH = 1
ALPHA = (2 * DEPTH) ** 0.25

WKV_CHUNK = 64
MOE_ROWS = 512
GATE_GROUP = 4 * HEAD
HALO = 8
VMEM_LIMIT = 56 * 1024 * 1024

_NN = (((1,), (0,)), ((), ()))
_NT = (((1,), (1,)), ((), ()))
_TN = (((0,), (0,)), ((), ()))


def _split(x, n):
    if x.dtype == BF16:
        return [x]
    parts, rest = [], x.astype(F32)
    for i in range(n):
        p = rest.astype(BF16)
        parts.append(p)
        if i + 1 < n:
            rest = rest - p.astype(F32)
    return parts


def _dot(a, b, dims=_NN, pa=1, pb=1):
    ap, bp = _split(a, pa), _split(b, pb)
    order = max(len(ap), len(bp))
    acc = None
    for i, x in enumerate(ap):
        for j, y in enumerate(bp):
            if i + j < order:
                t = lax.dot_general(x, y, dims, preferred_element_type=F32)
                acc = t if acc is None else acc + t
    return acc


def _ln(x):
    mu = jnp.mean(x, -1, keepdims=True)
    xc = x - mu
    var = jnp.mean(xc * xc, -1, keepdims=True)
    return xc * lax.rsqrt(var + LN_EPS)


def _sigmoid(x):
    return 0.5 + 0.5 * jnp.tanh(0.5 * x)


def _softplus(x):
    return jnp.maximum(x, 0.0) + jnp.log(1.0 + jnp.exp(-jnp.abs(x)))


def _head_sums(x):
    pw = 2 * HEAD
    row = lax.broadcasted_iota(jnp.int32, (pw, pw), 0)
    col = lax.broadcasted_iota(jnp.int32, (pw, pw), 1)
    ones = jnp.where((row >= HEAD) == (col >= HEAD), 1.0, 0.0).astype(BF16)
    return jnp.concatenate([_dot(x[:, p:p + pw], ones, _NN, 2, 1) for p in range(0, x.shape[1], pw)], axis=1)


def _inproj_kernel(x_ref, sc_ref, sh_ref, wm_ref, wl_ref, main_ref, lora_ref, h_scr, *, nm):
    j = pl.program_id(2)

    @pl.when(j == 0)
    def _():
        h_scr[...] = (_ln(x_ref[0]) * (1.0 + sc_ref[0]) + sh_ref[0]).astype(BF16)

    @pl.when(j < nm)
    def _():
        main_ref[0] = _dot(h_scr[...], wm_ref[j])

    @pl.when(j == nm)
    def _():
        lora_ref[0] = _dot(h_scr[...], wl_ref[...])


def inproj(x, sc, sh, w_main, w_lora, tm):
    b_, l_, d_ = x.shape
    nm = w_main.shape[0]
    row = pl.BlockSpec((1, tm, d_), lambda b, i, j: (b, i, 0))
    vec = pl.BlockSpec((1, 1, d_), lambda b, i, j: (b, 0, 0))
    return pl.pallas_call(
        functools.partial(_inproj_kernel, nm=nm),
        grid=(b_, l_ // tm, nm + 1),
        in_specs=[row, vec, vec,
                  pl.BlockSpec((nm, d_, d_), lambda b, i, j: (0, 0, 0)),
                  pl.BlockSpec((d_, LORA_PAD), lambda b, i, j: (0, 0))],
        out_specs=[pl.BlockSpec((1, tm, d_), lambda b, i, j: (b, i, jnp.minimum(j, nm - 1))),
                   pl.BlockSpec((1, tm, LORA_PAD), lambda b, i, j: (b, i, 0))],
        out_shape=[jax.ShapeDtypeStruct((b_, l_, nm * d_), F32), jax.ShapeDtypeStruct((b_, l_, LORA_PAD), F32)],
        scratch_shapes=[pltpu.VMEM((tm, d_), BF16)],
        compiler_params=pltpu.CompilerParams(vmem_limit_bytes=VMEM_LIMIT),
        name="inproj",
    )(x, sc, sh, w_main, w_lora)


def _mix_kernel(cur_ref, prev_ref, next_ref, lcur_ref, lprev_ref, lnext_ref, xcur_ref, xprev_ref, xnext_ref,
                mu_ref, mul_ref, kk_ref, cw_ref, cb_ref, mixed_ref, lz_ref, xc_ref, *, grid_mode):
    i = pl.program_id(1)
    n_i = pl.num_programs(1)
    tm = cur_ref.shape[1]
    d_ = kk_ref.shape[1]

    def shifted(cur, prev, nxt):
        rows, ch = cur.shape
        rowi = lax.broadcasted_iota(jnp.int32, (rows, ch), 0)
        lane = lax.broadcasted_iota(jnp.int32, (rows, ch), 1)
        if grid_mode:
            prev = jnp.where(i > 0, prev, 0.0)
            nxt = jnp.where(i < n_i - 1, nxt, 0.0)
            up = jnp.concatenate([prev, cur[:rows - GRID_W]], axis=0)
            down = jnp.concatenate([cur[GRID_W:], nxt], axis=0)
            col = rowi % GRID_W
            left = jnp.where(col == 0, 0.0, pltpu.roll(cur, 1, 0))
            right = jnp.where(col == GRID_W - 1, 0.0, pltpu.roll(cur, rows - 1, 0))
            l4 = lane % 4
            return jnp.where(l4 == 0, left, jnp.where(l4 == 1, right, jnp.where(l4 == 2, up, down)))
        before = jnp.where(rowi == 0, 0.0, pltpu.roll(cur, 1, 0))
        after = jnp.where(rowi == rows - 1, 0.0, pltpu.roll(cur, rows - 1, 0))
        return jnp.where(lane % 2 == 0, before, after)

    def mixed(cur, prev, nxt, mu):
        return cur + mu * (shifted(cur, prev, nxt) - cur)

    for s in range(3):
        sl = slice(s * d_, (s + 1) * d_)
        z = mixed(cur_ref[0, :, sl], prev_ref[0, :, sl], next_ref[0, :, sl], mu_ref[:, sl])
        mixed_ref[0, :, sl] = z.astype(BF16)
        if s == 1:
            kq = z * kk_ref[...]
            kq = kq * lax.rsqrt(_head_sums(kq * kq) + 1e-12)
            mixed_ref[0, :, 3 * d_:4 * d_] = kq.astype(BF16)
    lz = mixed(lcur_ref[0], lprev_ref[0], lnext_ref[0], mul_ref[...])
    wa = lz[:, :LORA_DECAY + LORA_AAA]
    lane = lax.broadcasted_iota(jnp.int32, wa.shape, 1)
    lz_ref[0, :, :LORA_DECAY + LORA_AAA] = jnp.where(lane < LORA_DECAY, jnp.tanh(wa), wa).astype(BF16)
    lz_ref[0, :, LORA_DECAY + LORA_AAA:] = _sigmoid(lz[:, LORA_DECAY + LORA_AAA:]).astype(BF16)

    if grid_mode:
        before = jnp.where(i > 0, xprev_ref[0, GRID_W - HALO:, :], 0.0)
        after = jnp.where(i < n_i - 1, xnext_ref[0, :HALO, :], 0.0)
    else:
        before = after = jnp.zeros((HALO, d_), F32)
    ext = jnp.concatenate([before, xcur_ref[0], after], axis=0)
    xc = cb_ref[...]
    for j in range(CONV_W):
        o = HALO - CONV_W // 2 + j
        xc = xc + cw_ref[j:j + 1, :] * ext[o:o + tm]
    xc_ref[0] = xc


def rwkv_mix(main, lora, xb_col, mu, mu_l, k_k, conv_w, conv_b, grid_mode, tm):
    b_, l_, _ = main.shape
    d_ = k_k.shape[-1]
    if grid_mode:
        assert tm % GRID_W == 0 and l_ % tm == 0
        halo, per = GRID_W, tm // GRID_W
    else:
        assert tm == l_
        halo, per = 8, tm // 8
    n_h = l_ // halo
    cur = lambda w: pl.BlockSpec((1, tm, w), lambda b, i: (b, i, 0))
    prv = lambda w: pl.BlockSpec((1, halo, w), lambda b, i: (b, jnp.maximum(i * per - 1, 0), 0))
    nxt = lambda w: pl.BlockSpec((1, halo, w), lambda b, i: (b, jnp.minimum((i + 1) * per, n_h - 1), 0))
    par = lambda w: pl.BlockSpec((1, w), lambda b, i: (0, 0))
    xcur = pl.BlockSpec((1, tm, d_), lambda b, i: (b, i, xb_col))
    xprv = pl.BlockSpec((1, halo, d_), lambda b, i: (b, jnp.maximum(i * per - 1, 0), xb_col))
    xnxt = pl.BlockSpec((1, halo, d_), lambda b, i: (b, jnp.minimum((i + 1) * per, n_h - 1), xb_col))
    return pl.pallas_call(
        functools.partial(_mix_kernel, grid_mode=grid_mode),
        grid=(b_, l_ // tm),
        in_specs=[cur(3 * d_), prv(3 * d_), nxt(3 * d_), cur(LORA_PAD), prv(LORA_PAD), nxt(LORA_PAD), xcur, xprv, xnxt,
                  par(3 * d_), par(LORA_PAD), par(d_), pl.BlockSpec((CONV_W, d_), lambda b, i: (0, 0)), par(d_)],
        out_specs=[cur(4 * d_), cur(LORA_PAD), cur(d_)],
        out_shape=[jax.ShapeDtypeStruct((b_, l_, 4 * d_), BF16), jax.ShapeDtypeStruct((b_, l_, LORA_PAD), BF16),
                   jax.ShapeDtypeStruct((b_, l_, d_), F32)],
        compiler_params=pltpu.CompilerParams(vmem_limit_bytes=VMEM_LIMIT),
        name="rwkv_mix",
    )(main, main, main, lora, lora, lora, main, main, main, mu, mu_l, k_k, conv_w, conv_b)


def _wkv_kernel(rf_ref, kf_ref, vf_ref, kkf_ref, lzf_ref, rb_ref, kb_ref, vb_ref, kkb_ref, lzb_ref,
                wlo_ref, wa0_ref, ka_ref, s0_ref, yf_ref, yb_ref, sT_ref, s_scr):
    c = pl.program_id(1)
    n_c = pl.num_programs(1)
    cs, d_ = rf_ref.shape[1], rf_ref.shape[2]
    pairs = s_scr.shape[1]
    pw = 2 * HEAD

    @pl.when(c == 0)
    def _():
        s_scr[...] = s0_ref[:, 0]

    row = lax.broadcasted_iota(jnp.int32, (cs, cs), 0)
    col = lax.broadcasted_iota(jnp.int32, (cs, cs), 1)
    row2 = lax.broadcasted_iota(jnp.int32, (2 * cs, pw), 0)
    col2 = lax.broadcasted_iota(jnp.int32, (2 * cs, pw), 1)
    same_head = (row2 >= cs) == (col2 >= HEAD)
    lane = lax.broadcasted_iota(jnp.int32, (cs, pw), 1)
    m0, m1 = lane < HEAD, lane >= HEAD
    mm0, mm1 = col2 < HEAD, col2 >= HEAD
    cat = lambda x, y: jnp.concatenate([x, y], axis=0)
    dot = lambda x, y, dims=_NN: lax.dot_general(x, y, dims, preferred_element_type=F32)
    sel = lambda m, x: jnp.where(m, x, jnp.zeros_like(x))

    streams = []
    for d, (r_ref, k_ref, v_ref, kk_ref, lz_ref, y_ref) in enumerate(
            ((rf_ref, kf_ref, vf_ref, kkf_ref, lzf_ref, yf_ref), (rb_ref, kb_ref, vb_ref, kkb_ref, lzb_ref, yb_ref))):
        tdiff = (row2 % cs - col2 % cs) * (1 - 2 * d)
        amask = (tdiff > 0) | ((row2 >= cs) & (tdiff == 0))
        tri = jnp.where((row - col) * (1 - 2 * d) >= 0, 1.0, 0.0).astype(BF16)
        z = _dot(lz_ref[0, :, :LORA_DECAY + LORA_AAA], wlo_ref[d]) + wa0_ref[d]
        lw = -jnp.exp(-_softplus(-z[:, :d_]) - 0.5)
        a = _sigmoid(z[:, d_:])
        r, kk = r_ref[0].astype(F32), kk_ref[0].astype(F32)
        k = k_ref[0].astype(F32) * (1.0 + (a - 1.0) * ka_ref[...])
        cum = _dot(tri, lw, _NN, 1, 3)
        tot = jnp.sum(lw, axis=0, keepdims=True)
        b = kk * a
        e_neg = jnp.exp(-cum)
        e_end = jnp.exp(tot - cum)
        rt = (r * jnp.exp(cum)).astype(BF16)
        at = (-kk * jnp.exp(cum - lw)).astype(BF16)
        bt = (b * e_neg).astype(BF16)
        kt = (k * e_neg).astype(BF16)
        bd = (b * e_end).astype(BF16)
        kd = (k * e_end).astype(BF16)
        vb = v_ref[0]
        p_end = jnp.exp(tot)
        for p in range(pairs):
            sl = slice(p * pw, (p + 1) * pw)
            streams.append(dict(d=d, p=p, sl=sl, y_ref=y_ref, amask=amask, ar=cat(at[:, sl], rt[:, sl]),
                                bt=bt[:, sl], kt=kt[:, sl], bdkd=cat(bd[:, sl], kd[:, sl]), v=vb[:, sl],
                                p_end=p_end[:, sl]))

    bd = lambda x: cat(sel(m0, x), sel(m1, x))
    lcat = lambda x, y: jnp.concatenate([x, y], axis=1)
    eye_p = jnp.where(lax.broadcasted_iota(jnp.int32, (cs, pw), 0) == lane % cs, 1.0, 0.0)
    for st in streams:
        q = dot(cat(sel(mm0, st["ar"]), sel(mm1, st["ar"])),
                cat(jnp.where(m0, st["bt"], st["kt"]), jnp.where(m0, st["kt"], st["bt"])), _NT)
        q0, q1 = sel(st["amask"], q[:2 * cs]), sel(st["amask"], q[2 * cs:])
        st["nil"] = jnp.where(m0, q0[:cs], q1[:cs])
        q0, q1 = q0.astype(BF16), q1.astype(BF16)
        st["ak"] = jnp.where(m0, q1[:cs], q0[:cs])
        st["rbk"] = lcat(q0[cs:], q1[cs:])
    for st in streams:
        nb = st["nil"].astype(BF16)
        st["t"] = eye_p + st["nil"]
        st["nb"] = dot(nb, bd(nb)).astype(BF16)
    for i in range(max(cs.bit_length() - 3, 0)):
        for st in streams:
            xx = dot(cat(st["t"].astype(BF16), st["nb"]), bd(st["nb"]))
            st["t"] = st["t"] + xx[:cs]
            st["nb"] = xx[cs:].astype(BF16)
    for st in streams:
        st["t"] = (st["t"] + dot(st["t"].astype(BF16), bd(st["nb"]))).astype(BF16)

    zero = jnp.zeros((cs, pw), BF16)
    for st in streams:
        st["s"] = s_scr[st["d"], st["p"]]
        st["as"] = dot(lcat(st["ar"], cat(st["ak"], zero)),
                       cat(st["s"].astype(BF16), cat(sel(m1, st["v"]), sel(m0, st["v"]))))
    for st in streams:
        st["u"] = dot(st["t"], bd(st["as"][:cs].astype(BF16))).astype(BF16)
    for st in streams:
        u_p, v_p = st["u"], st["v"]
        y = st["as"][cs:] + dot(st["rbk"], cat(cat(sel(m0, u_p), sel(m0, v_p)), cat(sel(m1, v_p), sel(m1, u_p))))
        st["y_ref"][0, :, st["sl"]] = y
        p_col = jnp.broadcast_to(st["p_end"], (pw, pw)).T
        upd = dot(st["bdkd"], cat(u_p, v_p), _TN)
        s_scr[st["d"], st["p"]] = p_col * st["s"] + sel(same_head, upd)

    @pl.when(c == n_c - 1)
    def _():
        sT_ref[:, 0] = s_scr[...]


def _pair_states(s):
    st = jnp.swapaxes(s, -1, -2)
    st = st.reshape(s.shape[:-3] + (s.shape[-3] // 2, 2, HEAD, HEAD))
    eye = jnp.eye(2, dtype=s.dtype)
    out = jnp.einsum("...pakv,ab->...pakbv", st, eye)
    return out.reshape(s.shape[:-3] + (s.shape[-3] // 2, 2 * HEAD, 2 * HEAD))


def wkv7(mixed, lz, w_lo, wa0, k_a, s0p):
    b_, l_, d4 = mixed.shape
    d_ = d4 // 4
    pairs, pw = d_ // (2 * HEAD), 2 * HEAD
    cs = WKV_CHUNK
    assert cs == HEAD and l_ % cs == 0
    n_c = l_ // cs

    fwd = lambda w, j: pl.BlockSpec((1, cs, w), lambda b, c: (b, c, j))
    bwd = lambda w, j: pl.BlockSpec((1, cs, w), lambda b, c: (b, n_c - 1 - c, j))
    full = lambda a: pl.BlockSpec(a.shape, lambda b, c: (0,) * a.ndim)
    state = pl.BlockSpec((2, 1, pairs, pw, pw), lambda b, c: (0, b, 0, 0, 0))
    return pl.pallas_call(
        _wkv_kernel,
        grid=(b_, n_c),
        in_specs=[fwd(d_, 0), fwd(d_, 1), fwd(d_, 2), fwd(d_, 3), fwd(LORA_PAD, 0),
                  bwd(d_, 0), bwd(d_, 1), bwd(d_, 2), bwd(d_, 3), bwd(LORA_PAD, 0),
                  full(w_lo), full(wa0), full(k_a), state],
        out_specs=[fwd(d_, 0), bwd(d_, 0), state],
        out_shape=[jax.ShapeDtypeStruct((b_, l_, d_), F32), jax.ShapeDtypeStruct((b_, l_, d_), F32),
                   jax.ShapeDtypeStruct((2, b_, pairs, pw, pw), F32)],
        scratch_shapes=[pltpu.VMEM((2, pairs, pw, pw), F32)],
        compiler_params=pltpu.CompilerParams(
            dimension_semantics=("arbitrary", "arbitrary"), vmem_limit_bytes=VMEM_LIMIT),
        name="wkv7",
    )(mixed, mixed, mixed, mixed, lz, mixed, mixed, mixed, mixed, lz, w_lo, wa0, k_a, s0p)


def _lru_kernel(xc_ref, wg_ref, bg_ref, cl_ref, h0_ref, h_ref, hT_ref, a_scr, u_scr, h_scr):
    d = pl.program_id(0)
    c = pl.program_id(2)
    n_c = pl.num_programs(2)
    tm, d_ = a_scr.shape
    gw = GATE_GROUP

    @pl.when(c == 0)
    def _():
        h_scr[...] = h0_ref[0, 0]

    xc = xc_ref[0]
    for g in range(d_ // gw):
        sl = slice(g * gw, (g + 1) * gw)
        z = _dot(xc[:, sl], wg_ref[0, g])
        rg = _sigmoid(z[:, :gw] + bg_ref[0, :, sl])
        ig = _sigmoid(z[:, gw:] + bg_ref[0, :, d_ + g * gw:d_ + (g + 1) * gw])
        log_a = cl_ref[0, :, sl] * rg
        a_scr[:, sl] = jnp.exp(log_a)
        u_scr[:, sl] = jnp.sqrt(1.0 - jnp.exp(2.0 * log_a)) * ig * xc[:, sl]

    rowid = lax.broadcasted_iota(jnp.int32, (8, d_), 0)

    def tile_scan(i, h, rev):
        t8 = (tm // 8 - 1 - i) if rev else i
        r0 = pl.multiple_of(t8 * 8, 8)
        a8 = a_scr[pl.ds(r0, 8), :]
        u8 = u_scr[pl.ds(r0, 8), :]
        for s in (1, 2, 4):
            ok = (rowid < 8 - s) if rev else (rowid >= s)
            sh = (8 - s) if rev else s
            a_sh = jnp.where(ok, pltpu.roll(a8, sh, 0), 1.0)
            u_sh = jnp.where(ok, pltpu.roll(u8, sh, 0), 0.0)
            u8 = a8 * u_sh + u8
            a8 = a8 * a_sh
        h8 = u8 + a8 * h
        h_ref[0, 0, pl.ds(r0, 8), :] = h8
        return h8[0:1] if rev else h8[7:8]

    @pl.when(d == 0)
    def _():
        h_scr[...] = lax.fori_loop(0, tm // 8, lambda i, h: tile_scan(i, h, False), h_scr[...], unroll=2)

    @pl.when(d == 1)
    def _():
        h_scr[...] = lax.fori_loop(0, tm // 8, lambda i, h: tile_scan(i, h, True), h_scr[...], unroll=2)

    @pl.when(c == n_c - 1)
    def _():
        hT_ref[0, 0] = h_scr[...]


def rglru(xc, w_gate, b_gate, c_lam, h0, tm):
    b_, l_, d_ = xc.shape
    tm = min(tm, l_)
    n_c = l_ // tm
    tmap = lambda d, c: c + d * (n_c - 1 - 2 * c)
    dirp = lambda shape: pl.BlockSpec((1,) + shape, lambda d, b, c: (d,) + (0,) * len(shape))
    st = pl.BlockSpec((1, 1, 1, d_), lambda d, b, c: (d, b, 0, 0))
    return pl.pallas_call(
        _lru_kernel,
        grid=(2, b_, n_c),
        in_specs=[pl.BlockSpec((1, tm, d_), lambda d, b, c: (b, tmap(d, c), 0)),
                  dirp(w_gate.shape[1:]), dirp((1, 2 * d_)), dirp((1, d_)), st],
        out_specs=[pl.BlockSpec((1, 1, tm, d_), lambda d, b, c: (d, b, tmap(d, c), 0)), st],
        out_shape=[jax.ShapeDtypeStruct((2, b_, l_, d_), F32), jax.ShapeDtypeStruct((2, b_, 1, d_), F32)],
        scratch_shapes=[pltpu.VMEM((tm, d_), F32), pltpu.VMEM((tm, d_), F32), pltpu.VMEM((1, d_), F32)],
        compiler_params=pltpu.CompilerParams(
            dimension_semantics=("arbitrary", "arbitrary", "arbitrary"), vmem_limit_bytes=VMEM_LIMIT),
        name="rglru",
    )(xc, w_gate, b_gate, c_lam, h0)


def _post_kernel(yf_ref, yb_ref, r_ref, k_ref, v_ref, lz_ref, hf_ref, hb_ref, gb_ref, ga_ref, gm_ref, x_ref,
                 gt_ref, sc_ref, sh_ref, wa2_ref, a0_ref, g2_ref, ka_ref, rk_ref, gg_ref, gnb_ref,
                 pa_ref, pb_ref, wo_ref, lg_ref, lb_ref, rwh_ref, rwl_ref, rb_ref,
                 x1_ref, h2_ref, lg_out_ref):
    d_ = x_ref.shape[2]
    y = yf_ref[0] + yb_ref[0]
    mu = _head_sums(y) * (1.0 / HEAD)
    yc = y - mu
    var = _head_sums(yc * yc) * (1.0 / HEAD)
    yn = yc * lax.rsqrt(var + GN_EPS) * gg_ref[...] + gnb_ref[...]
    lz = lz_ref[0]
    a2 = _sigmoid(_dot(lz[:, :LORA_DECAY + LORA_AAA], wa2_ref[...]) + a0_ref[...])
    r, k, v = r_ref[0].astype(F32), k_ref[0].astype(F32), v_ref[0].astype(F32)
    ksum = k * (2.0 + (a2[:, :d_] + a2[:, d_:] - 2.0) * ka_ref[...])
    bonus = _head_sums(r * ksum * rk_ref[...]) * v
    g = _dot(lz[:, LORA_DECAY + LORA_AAA:], g2_ref[...])
    y_a = ((yn + bonus) * g).astype(BF16)
    gb = gb_ref[0]
    gelu = 0.5 * gb * (1.0 + jnp.tanh(0.7978845608028654 * (gb + 0.044715 * gb * gb * gb)))
    y_b = ((hf_ref[0, 0] + hb_ref[0, 0]) * gelu).astype(BF16)
    m = _sigmoid(ga_ref[0]) * _dot(y_a, pa_ref[...]) + _sigmoid(gm_ref[0]) * _dot(y_b, pb_ref[...])
    mix = _dot(m, wo_ref[...])
    x1 = _ln(ALPHA * _ln(x_ref[0]) + gt_ref[0] * mix) * lg_ref[...] + lb_ref[...]
    x1_ref[0] = x1
    h2 = x1 * (1.0 + sc_ref[0]) + sh_ref[0]
    h2_ref[0] = h2
    hh, hl = _split(h2, 2)
    lg_out_ref[0] = (_dot(hh, rwh_ref[...]) + _dot(hl, rwh_ref[...]) + _dot(hh, rwl_ref[...])) + rb_ref[...]


def post_mix(y_f, y_b, mixed, lz, hh, main, x, gt, sc, sh, params, tm):
    b_, l_, d_ = x.shape
    dsec = lambda e: pl.BlockSpec((1, 1, tm, d_), lambda b, i: (e, b, i, 0))
    col = lambda j: pl.BlockSpec((1, tm, d_), lambda b, i: (b, i, j))
    vec = pl.BlockSpec((1, 1, d_), lambda b, i: (b, 0, 0))
    full = lambda a: pl.BlockSpec(a.shape, lambda b, i: (0,) * a.ndim)
    return pl.pallas_call(
        _post_kernel,
        grid=(b_, l_ // tm),
        in_specs=[col(0), col(0), col(0), col(1), col(2),
                  pl.BlockSpec((1, tm, LORA_PAD), lambda b, i: (b, i, 0)),
                  dsec(0), dsec(1), col(4), col(5), col(6), col(0), vec, vec, vec] + [full(p) for p in params],
        out_specs=[col(0), col(0), pl.BlockSpec((1, tm, 128), lambda b, i: (b, i, 0))],
        out_shape=[jax.ShapeDtypeStruct((b_, l_, d_), F32), jax.ShapeDtypeStruct((b_, l_, d_), F32),
                   jax.ShapeDtypeStruct((b_, l_, 128), F32)],
        compiler_params=pltpu.CompilerParams(vmem_limit_bytes=VMEM_LIMIT),
        name="post_mix",
    )(y_f, y_b, mixed, mixed, mixed, lz, hh, hh, main, main, main, x, gt, sc, sh, *params)


def _mm_kernel(x_ref, w_ref, o_ref, *, pa, pb):
    o_ref[...] = _dot(x_ref[...], w_ref[...], _NN, pa, pb)


def pmm(x, w, tm=512, tn=1024, pa=1, pb=1):
    m_, k_ = x.shape
    n_ = w.shape[1]
    tm, tn = min(tm, m_), min(tn, n_)
    assert m_ % tm == 0 and n_ % tn == 0, (x.shape, w.shape, tm, tn)
    return pl.pallas_call(
        functools.partial(_mm_kernel, pa=pa, pb=pb),
        grid=(m_ // tm, n_ // tn),
        in_specs=[pl.BlockSpec((tm, k_), lambda i, j: (i, 0)), pl.BlockSpec((k_, tn), lambda i, j: (0, j))],
        out_specs=pl.BlockSpec((tm, tn), lambda i, j: (i, j)),
        out_shape=jax.ShapeDtypeStruct((m_, n_), F32),
        compiler_params=pltpu.CompilerParams(vmem_limit_bytes=VMEM_LIMIT),
        name="pmm",
    )(x, w)


ROUTE_LANES = 128
R_E, R_RANK, R_W = 0, 2, 4
DMA_UNROLL = 4


def _route_kernel(lg_ref, rec_ref, cnt_ref, carry):
    i = pl.program_id(0)
    tm = lg_ref.shape[0]

    @pl.when(i == 0)
    def _():
        carry[...] = jnp.zeros_like(carry)

    lg = lg_ref[...]
    lane = lax.broadcasted_iota(jnp.int32, lg.shape, 1)
    neg = -jnp.inf
    first = lambda m: jnp.min(jnp.where(m, lane, ROUTE_LANES), axis=1, keepdims=True)
    is_g = lane < N_GROUPS
    gmax = jnp.max(jnp.where(is_g, lg, neg), axis=1, keepdims=True)
    gsel = first(is_g & (lg == gmax))
    p_g = 1.0 / jnp.sum(jnp.where(is_g, jnp.exp(lg - gmax), 0.0), axis=1, keepdims=True)
    in_grp = (lane >= N_GROUPS) & (lane < N_GROUPS + N_EXPERTS) & ((lane - N_GROUPS) // EXPERTS_PER_GROUP == gsel)
    el = jnp.where(in_grp, lg, neg)
    v1 = jnp.max(el, axis=1, keepdims=True)
    i1 = first(in_grp & (el == v1))
    rest = in_grp & (lane != i1)
    el2 = jnp.where(rest, lg, neg)
    v2 = jnp.max(el2, axis=1, keepdims=True)
    i2 = first(rest & (el2 == v2))
    e21 = jnp.exp(v2 - v1)
    w1 = p_g / (1.0 + e21)
    w2 = w1 * e21
    oh1, oh2 = lane == i1, lane == i2
    both = jnp.where(oh1 | oh2, 1.0, 0.0)
    row = lax.broadcasted_iota(jnp.int32, (tm, tm), 0)
    col = lax.broadcasted_iota(jnp.int32, (tm, tm), 1)
    before = jnp.where(col < row, 1.0, 0.0).astype(BF16)
    cnt = _dot(before, both.astype(BF16)) + carry[...]
    rank1 = jnp.sum(jnp.where(oh1, cnt, 0.0), axis=1, keepdims=True)
    rank2 = jnp.sum(jnp.where(oh2, cnt, 0.0), axis=1, keepdims=True)
    carry[...] = carry[...] + jnp.sum(both, axis=0, keepdims=True)
    rec = jnp.zeros_like(lg)
    for k, val in ((R_E, (i1 - N_GROUPS).astype(F32)), (R_E + 1, (i2 - N_GROUPS).astype(F32)),
                   (R_RANK, rank1), (R_RANK + 1, rank2), (R_W, w1), (R_W + 1, w2)):
        rec = jnp.where(lane == k, val, rec)
    rec_ref[...] = rec
    cnt_ref[...] = carry[...]


def route(logits, tm=256):
    t_ = logits.shape[0]
    return pl.pallas_call(
        _route_kernel,
        grid=(t_ // tm,),
        in_specs=[pl.BlockSpec((tm, ROUTE_LANES), lambda i: (i, 0))],
        out_specs=[pl.BlockSpec((tm, ROUTE_LANES), lambda i: (i, 0)), pl.BlockSpec((1, ROUTE_LANES), lambda i: (0, 0))],
        out_shape=[jax.ShapeDtypeStruct((t_, ROUTE_LANES), F32), jax.ShapeDtypeStruct((1, ROUTE_LANES), F32)],
        scratch_shapes=[pltpu.VMEM((1, ROUTE_LANES), F32)],
        compiler_params=pltpu.CompilerParams(dimension_semantics=("arbitrary",)),
        name="route",
    )(logits)


def _dispatch_kernel(dest_ref, zrow_ref, h_ref, xs_ref, zbuf, sem, zsem):
    tm = h_ref.shape[0]
    base = pl.program_id(0) * tm * TOP_K

    @pl.when(pl.program_id(0) == 0)
    def _():
        zbuf[...] = jnp.zeros_like(zbuf)
        n_blocks = xs_ref.shape[0] // MOE_ROWS

        def zero_copy(row0):
            return pltpu.make_async_copy(zbuf, xs_ref.at[pl.ds(pl.multiple_of(row0, MOE_ROWS), MOE_ROWS)], zsem)

        def tail(fn):
            return lax.fori_loop(zrow_ref[N_EXPERTS] // MOE_ROWS, n_blocks, lambda b, c: (fn(zero_copy(b * MOE_ROWS)), c)[1], 0)

        for e in range(N_EXPERTS):
            @pl.when(zrow_ref[e] >= 0)
            def _():
                zero_copy(zrow_ref[e]).start()
        tail(lambda cp: cp.start())
        for e in range(N_EXPERTS):
            @pl.when(zrow_ref[e] >= 0)
            def _():
                zero_copy(zrow_ref[e]).wait()
        tail(lambda cp: cp.wait())

    def copy(r, s):
        return pltpu.make_async_copy(h_ref.at[pl.ds(r, 1)], xs_ref.at[pl.ds(dest_ref[base + r * TOP_K + s], 1)], sem)

    def start(r, carry):
        for s in range(TOP_K):
            copy(r, s).start(priority=s % 2)
        return carry

    def wait(r, carry):
        for s in range(TOP_K):
            copy(r, s).wait()
        return carry

    lax.fori_loop(0, tm, start, 0, unroll=DMA_UNROLL)
    lax.fori_loop(0, tm, wait, 0, unroll=DMA_UNROLL)


def dispatch(dest, zero_row, h, n_pad, tm=1024):
    t_, d_ = h.shape
    tm = min(tm, t_)
    grid_spec = pltpu.PrefetchScalarGridSpec(
        num_scalar_prefetch=2,
        grid=(t_ // tm,),
        in_specs=[pl.BlockSpec((tm, d_), lambda i, dest, zr: (i, 0))],
        out_specs=pl.BlockSpec(memory_space=pl.ANY),
        scratch_shapes=[pltpu.VMEM((MOE_ROWS, d_), F32), pltpu.SemaphoreType.DMA(()), pltpu.SemaphoreType.DMA(())],
    )
    return pl.pallas_call(
        _dispatch_kernel,
        grid_spec=grid_spec,
        out_shape=jax.ShapeDtypeStruct((n_pad, d_), F32),
        compiler_params=pltpu.CompilerParams(dimension_semantics=("arbitrary",), has_side_effects=True),
        name="dispatch",
    )(dest, zero_row, h)


def _moe_kernel(be_ref, nv_ref, x_ref, w1_ref, w3_ref, w2_ref, o_ref, w1b, w3b, w2b):
    blk = pl.program_id(0)

    @pl.when((blk == 0) | (be_ref[blk] != be_ref[jnp.maximum(blk - 1, 0)]))
    def _():
        w1b[...] = w1_ref[0].astype(BF16)
        w3b[...] = w3_ref[0].astype(BF16)
        w2b[...] = w2_ref[0].astype(BF16)

    @pl.when(nv_ref[blk] > 0)
    def _():
        x = x_ref[...].astype(BF16)
        h1 = _dot(x, w1b[...])
        h3 = _dot(x, w3b[...])
        hh = h1 * _sigmoid(h1) * h3
        o_ref[...] = _dot(hh, w2b[...])

    @pl.when(nv_ref[blk] == 0)
    def _():
        o_ref[...] = jnp.zeros_like(o_ref)


def moe_experts(xs, blk_e, blk_n, w1, w3, w2):
    n_pad, d_ = xs.shape
    n_blocks = n_pad // MOE_ROWS
    de = w1.shape[2]
    grid_spec = pltpu.PrefetchScalarGridSpec(
        num_scalar_prefetch=2,
        grid=(n_blocks,),
        in_specs=[pl.BlockSpec((MOE_ROWS, d_), lambda i, be, nv: (i, 0)),
                  pl.BlockSpec((1, d_, de), lambda i, be, nv: (be[i], 0, 0)),
                  pl.BlockSpec((1, d_, de), lambda i, be, nv: (be[i], 0, 0)),
                  pl.BlockSpec((1, de, d_), lambda i, be, nv: (be[i], 0, 0))],
        out_specs=pl.BlockSpec((MOE_ROWS, d_), lambda i, be, nv: (i, 0)),
        scratch_shapes=[pltpu.VMEM((d_, de), BF16), pltpu.VMEM((d_, de), BF16), pltpu.VMEM((de, d_), BF16)],
    )
    return pl.pallas_call(
        _moe_kernel,
        grid_spec=grid_spec,
        out_shape=jax.ShapeDtypeStruct((n_pad, d_), F32),
        compiler_params=pltpu.CompilerParams(dimension_semantics=("arbitrary",), vmem_limit_bytes=VMEM_LIMIT),
        name="moe_experts",
    )(blk_e, blk_n, xs, w1, w3, w2)


def _combine_kernel(dest_ref, x_ref, rec_ref, gt_ref, g_ref, b_ref, ys_ref, o_ref, buf, sem):
    tm = x_ref.shape[1]
    base = (pl.program_id(0) * pl.num_programs(1) + pl.program_id(1)) * tm * TOP_K

    def copy(r, s):
        return pltpu.make_async_copy(ys_ref.at[pl.ds(dest_ref[base + r * TOP_K + s], 1)], buf.at[s, pl.ds(r, 1)], sem)

    def start(r, carry):
        for s in range(TOP_K):
            copy(r, s).start(priority=s % 2)
        return carry

    def wait(r, carry):
        for s in range(TOP_K):
            copy(r, s).wait()
        return carry

    lax.fori_loop(0, tm, start, 0, unroll=DMA_UNROLL)
    lax.fori_loop(0, tm, wait, 0, unroll=DMA_UNROLL)
    rec = rec_ref[...]
    moe = rec[:, R_W:R_W + 1] * buf[0]
    for s in range(1, TOP_K):
        moe = moe + rec[:, R_W + s:R_W + s + 1] * buf[s]
    o_ref[0] = _ln(ALPHA * x_ref[0] + gt_ref[0] * moe) * g_ref[...] + b_ref[...]


def combine_ln(dest, x, rec, gt, g, b, ys, tm=1024):
    b_, l_, d_ = x.shape
    tm = min(tm, l_)
    n_i = l_ // tm
    grid_spec = pltpu.PrefetchScalarGridSpec(
        num_scalar_prefetch=1,
        grid=(b_, n_i),
        in_specs=[pl.BlockSpec((1, tm, d_), lambda bi, i, dest: (bi, i, 0)),
                  pl.BlockSpec((tm, ROUTE_LANES), lambda bi, i, dest: (bi * n_i + i, 0)),
                  pl.BlockSpec((1, 1, d_), lambda bi, i, dest: (bi, 0, 0)),
                  pl.BlockSpec((1, d_), lambda bi, i, dest: (0, 0)),
                  pl.BlockSpec((1, d_), lambda bi, i, dest: (0, 0)),
                  pl.BlockSpec(memory_space=pl.ANY)],
        out_specs=pl.BlockSpec((1, tm, d_), lambda bi, i, dest: (bi, i, 0)),
        scratch_shapes=[pltpu.VMEM((TOP_K, tm, d_), F32), pltpu.SemaphoreType.DMA(())],
    )
    return pl.pallas_call(
        _combine_kernel,
        grid_spec=grid_spec,
        out_shape=jax.ShapeDtypeStruct(x.shape, F32),
        compiler_params=pltpu.CompilerParams(
            dimension_semantics=("arbitrary", "arbitrary"), vmem_limit_bytes=VMEM_LIMIT),
        name="combine_ln",
    )(dest, x, rec, gt, g.reshape(1, d_), b.reshape(1, d_), ys)


def hier_moe_ln(x1, h2, logits, gt, g, b, w1, w3, w2):
    b_, l_, d_ = x1.shape
    t_ = b_ * l_
    rec, cnt = route(logits)
    counts = cnt[0, N_GROUPS:N_GROUPS + N_EXPERTS].astype(jnp.int32)
    padded = (counts + MOE_ROWS - 1) // MOE_ROWS * MOE_ROWS
    pad_end = jnp.cumsum(padded)
    pad_start = pad_end - padded
    eid = rec[:, R_E:R_E + TOP_K].astype(jnp.int32)
    rank = rec[:, R_RANK:R_RANK + TOP_K].astype(jnp.int32)
    onehot = eid[..., None] == jnp.arange(N_EXPERTS, dtype=jnp.int32)
    dest = (rank + jnp.sum(jnp.where(onehot, pad_start, 0), -1)).reshape(-1)
    n_blocks = -(-(t_ * TOP_K) // MOE_ROWS) + N_EXPERTS
    blk_lo = jnp.arange(n_blocks, dtype=jnp.int32) * MOE_ROWS
    blk_e = jnp.minimum(jnp.sum(blk_lo[:, None] >= pad_end[None, :], -1), N_EXPERTS - 1).astype(jnp.int32)
    blk_n = jnp.clip(jnp.sum(jnp.where(blk_e[:, None] == jnp.arange(N_EXPERTS), pad_start + counts, 0), -1) - blk_lo,
                     0, MOE_ROWS).astype(jnp.int32)
    zero_row = jnp.concatenate([jnp.where(counts > 0, pad_end - MOE_ROWS, -1), pad_end[-1:]]).astype(jnp.int32)
    xs = dispatch(dest, zero_row, h2.reshape(t_, d_), n_blocks * MOE_ROWS)
    ys = moe_experts(xs, blk_e, blk_n, w1, w3, w2)
    return combine_ln(dest, x1, rec, gt, g, b, ys)


def _block_diag_groups(w, grp):
    n = w.shape[0]
    wg = w.reshape(n // grp, grp, HEAD, HEAD)
    eye = jnp.eye(grp, dtype=w.dtype)
    return jnp.einsum("gaij,ab->gaibj", wg, eye).reshape(n // grp, grp * HEAD, grp * HEAD)


def token_scans(h_in, sc, sh, wts, grid_mode, states):
    b_, l_, d_ = h_in.shape
    main, lora = inproj(h_in, sc, sh, wts["w_main"], wts["w_lora"], min(1024, l_))
    mixed, lz, xc = rwkv_mix(main, lora, 3, wts["mu"], wts["mu_l"], wts["k_k"], wts["conv_w"], wts["conv_b"], grid_mode,
                             512 if grid_mode else l_)
    y_f, y_b, s_new = wkv7(mixed, lz, wts["w_lo"], wts["wa0"], wts["k_a"], states[0])
    hh, h_new = rglru(xc, wts["w_gate"], wts["b_gate"], wts["c_lam"], states[1], 1024)
    return (main, mixed, lz, y_f, y_b, hh), (s_new, h_new)


def kernel(x, c, ctx, c_ctx, w_ada, b_ada, w_in, mu_a, w0, w2, a0, a2, g2, k_k, k_a, r_k, gn_g, gn_b, conv_w, conv_b, lru_wa, lru_ba, lru_wx, lru_bx, lru_lam, p_a, p_b, w_o, ln1_g, ln1_b, router_g, router_g_b, router_e, router_e_b, e_w1, e_w3, e_w2, ln2_g, ln2_b):
    b_, l_, d_ = x.shape
    heads = d_ // HEAD
    l = 0
    row = lambda v: v.reshape(1, -1)
    cc = jnp.concatenate([c, c_ctx[None]], 0)
    cc = jnp.pad(jax.nn.silu(cc), ((0, 8 - cc.shape[0]), (0, 0)))
    mod = pmm(cc, w_ada[l], 8, 1024, 2, 2)[:b_ + 1] + b_ada[l]
    mods = jnp.split(mod, 6, axis=-1)
    sh1, sc1, gt1, sh2, sc2, gt2 = [m[:b_, None, :] for m in mods]
    csh1, csc1 = [jnp.broadcast_to(m[b_:, None, :], (b_, 1, d_)) for m in mods[:2]]

    a_slab = 3 * d_ + LORA_DECAY + LORA_AAA + LORA_GATE
    n_lora = a_slab - 3 * d_
    wi, mu = w_in[l], mu_a[l]
    zeros_lo = jnp.zeros((LORA_DECAY, d_), F32)
    w_lo = jnp.stack([jnp.concatenate([jnp.concatenate([w2[l, e], zeros_lo], 1),
                                       jnp.concatenate([zeros_lo, a2[l, e]], 1)], 0) for e in range(2)])
    wts = dict(
        w_main=jnp.stack([wi[:, o:o + d_] for o in (0, d_, 2 * d_, a_slab, a_slab + d_, a_slab + 2 * d_, a_slab + 3 * d_)]
                         ).astype(BF16),
        w_lora=jnp.pad(wi[:, 3 * d_:a_slab], ((0, 0), (0, LORA_PAD - n_lora))).astype(BF16),
        mu=row(mu[:3 * d_]), mu_l=row(jnp.pad(mu[3 * d_:], (0, LORA_PAD - n_lora))), k_k=row(k_k[l]), k_a=row(k_a[l]),
        w_lo=w_lo.astype(BF16),
        wa0=jnp.concatenate([w0[l], a0[l]], -1)[:, None, :],
        conv_w=conv_w[l], conv_b=row(conv_b[l]),
        w_gate=jnp.stack([jnp.concatenate([_block_diag_groups(lru_wa[l, e], 4), _block_diag_groups(lru_wx[l, e], 4)], -1)
                          for e in range(2)]).astype(BF16),
        b_gate=jnp.concatenate([lru_ba[l], lru_bx[l]], -1)[:, None, :],
        c_lam=(-LRU_C * jax.nn.softplus(-lru_lam[l]))[:, None, :])

    s0 = jnp.zeros((2, b_, heads // 2, 2 * HEAD, 2 * HEAD), F32)
    h0 = jnp.zeros((2, b_, 1, d_), F32)
    _, ctx_states = token_scans(ctx, csc1, csh1, wts, False, (s0, h0))
    (main, mixed, lz, y_f, y_b, hh), _ = token_scans(x, sc1, sh1, wts, True, ctx_states)

    zeros_a = jnp.zeros((LORA_DECAY, 2 * d_), F32)
    rw = jnp.pad(jnp.concatenate([router_g[l], router_e[l]], 1), ((0, 0), (0, 128 - N_GROUPS - N_EXPERTS)))
    rw_hi = rw.astype(BF16)
    params = [
        jnp.concatenate([zeros_a, jnp.concatenate([a2[l, 0], a2[l, 1]], 1)], 0).astype(BF16),
        row(jnp.concatenate([a0[l, 0], a0[l, 1]])),
        jnp.pad(g2[l], ((0, LORA_PAD - LORA_DECAY - LORA_AAA - LORA_GATE), (0, 0))).astype(BF16),
        row(k_a[l]), row(r_k[l]), row(gn_g[l]), row(gn_b[l]),
        p_a[l].astype(BF16), p_b[l].astype(BF16), w_o[l].astype(BF16), row(ln1_g[l]), row(ln1_b[l]),
        rw_hi, (rw - rw_hi.astype(F32)).astype(BF16),
        row(jnp.pad(jnp.concatenate([router_g_b[l], router_e_b[l]]), (0, 128 - N_GROUPS - N_EXPERTS)))]
    x1, h2, logits = post_mix(y_f, y_b, mixed, lz, hh, main, x, gt1, sc2, sh2, params, 256)

    return hier_moe_ln(x1, h2, logits.reshape(b_ * l_, -1), gt2, ln2_g[l], ln2_b[l], e_w1[l], e_w3[l], e_w2[l])
```
